```python
import jax
import jax.numpy as jnp
from jax import lax

D_MODEL = 2048
BATCH = 2
SEQ = 8192
DEPTH = 1

GRID_W = 64
CTX_LEN = 256

MLA_HEADS = 8
MLA_NOPE = 128
MLA_ROPE = 64
MLA_V = 128
Q_LORA = 512
KV_LORA = 256
ROPE_THETA = 10000.0
Q_BLOCK = 128

ML_HEADS = 4
ML_QK = 128
ML_V = 256
ML_INNER = ML_HEADS * ML_V
ML_CONV = 3
CHUNK = 64
N_GATES = 4 * ML_HEADS

D_FF = 5632
FFN_CONV = 3

MIX_WIDTH = MLA_HEADS * MLA_V + ML_INNER
OFF_CKV = Q_LORA
OFF_KR = Q_LORA + KV_LORA
OFF_U = OFF_KR + MLA_ROPE
OFF_V = OFF_U + ML_INNER
OFF_O = OFF_V + ML_INNER
OFF_G = OFF_O + ML_INNER
IN_COLS = OFF_G + N_GATES

ALPHA = (2.0 * DEPTH) ** 0.25
BETA = (8.0 * DEPTH) ** -0.25
EPS = 1e-6
N_MOD = 6

kernel_name = "hybrid_mla_mlstm_convffn_dit"


def rms_norm(x, g):
    xf = x.astype(jnp.float32)
    y = xf * lax.rsqrt(jnp.mean(xf * xf, axis=-1, keepdims=True) + EPS)
    return (y * g).astype(x.dtype)


def layer_norm(x, g, b):
    xf = x.astype(jnp.float32)
    mu = jnp.mean(xf, axis=-1, keepdims=True)
    var = jnp.mean(jnp.square(xf - mu), axis=-1, keepdims=True)
    return ((xf - mu) * lax.rsqrt(var + EPS) * g + b).astype(x.dtype)


def modulate(x, mod, i):
    return x * (1.0 + mod[..., i + 1, :]) + mod[..., i, :]


def dwconv_centred(x, w, b):
    k = w.shape[0]
    y = lax.conv_general_dilated(
        x, w[:, None, :].astype(x.dtype), window_strides=(1,), padding=[(k // 2, k // 2)],
        dimension_numbers=("NWC", "WIO", "NWC"), feature_group_count=x.shape[-1])
    return y + b


def axial_angles(n_rows):
    row = jnp.repeat(jnp.arange(n_rows), GRID_W).astype(jnp.float32)
    col = jnp.tile(jnp.arange(GRID_W), n_rows).astype(jnp.float32)
    n_freq = MLA_ROPE // 4
    inv = ROPE_THETA ** (-jnp.arange(n_freq, dtype=jnp.float32) / n_freq)
    return row[:, None] * inv, col[:, None] * inv


def rotate_half(x, ang):
    f = ang.shape[-1]
    cos = jnp.cos(ang)[:, None, :]
    sin = jnp.sin(ang)[:, None, :]
    x1, x2 = x[..., :f], x[..., f:]
    return jnp.concatenate([x1 * cos - x2 * sin, x1 * sin + x2 * cos], axis=-1).astype(x.dtype)


def rope2d(x, ang_row, ang_col):
    half = MLA_ROPE // 2
    return jnp.concatenate([rotate_half(x[..., :half], ang_row), rotate_half(x[..., half:], ang_col)], axis=-1)


def to_heads(a):
    return jnp.transpose(a, (0, 2, 1, 3))


def from_heads(a):
    b, h, s, d = a.shape
    return jnp.transpose(a, (0, 2, 1, 3)).reshape(b, s, h * d)


def mla_queries(z, g_q, w_uq):
    b, s, _ = z.shape
    c_q = rms_norm(z[..., :Q_LORA], g_q)
    q = (c_q @ w_uq).reshape(b, s, MLA_HEADS, MLA_NOPE + MLA_ROPE)
    return q[..., :MLA_NOPE], q[..., MLA_NOPE:]


def mla_keys_values(z, g_kv, w_ukv):
    b, s, _ = z.shape
    c_kv = rms_norm(z[..., OFF_CKV:OFF_KR], g_kv)
    k_rope = z[..., OFF_KR:OFF_U][:, :, None, :]
    kv = (c_kv @ w_ukv).reshape(b, s, MLA_HEADS, MLA_NOPE + MLA_V)
    return kv[..., :MLA_NOPE], k_rope, kv[..., MLA_NOPE:]


def key_heads(k_nope, k_rope):
    k_rope = jnp.broadcast_to(k_rope, k_nope.shape[:-1] + (MLA_ROPE,))
    return jnp.concatenate([k_nope, k_rope], axis=-1)


def block_attention(q, k, v):
    b, h, s, dq = q.shape
    nb = s // Q_BLOCK
    scale = dq ** -0.5
    qb = jnp.moveaxis(q.reshape(b, h, nb, Q_BLOCK, dq), 2, 0)

    def one_block(qi):
        sc = jnp.einsum("bhqd,bhkd->bhqk", qi, k, preferred_element_type=jnp.float32) * scale
        p = jax.nn.softmax(sc, axis=-1)
        return jnp.einsum("bhqk,bhkd->bhqd", p.astype(v.dtype), v)

    out = lax.map(one_block, qb)
    return jnp.moveaxis(out, 0, 2).reshape(b, h, s, v.shape[-1])


def mlstm_project(z, conv_w, conv_b, w_mq, w_mk, b_gate):
    b, s, _ = z.shape
    u = jax.nn.silu(dwconv_centred(z[..., OFF_U:OFF_V], conv_w, conv_b)).reshape(b, s, ML_HEADS, ML_V)
    q = jnp.einsum("bshd,hde->bhse", u, w_mq) * (ML_QK ** -0.5)
    k = jnp.einsum("bshd,hde->bhse", u, w_mk)
    v = to_heads(z[..., OFF_V:OFF_O].reshape(b, s, ML_HEADS, ML_V))
    o = z[..., OFF_O:OFF_G]
    gates = (z[..., OFF_G:] + b_gate).astype(jnp.float32)
    gates = jnp.transpose(gates.reshape(b, s, 4, ML_HEADS), (2, 0, 3, 1))
    return q, k, v, o, gates


def mlstm_chunkwise(q, k, v, ig, fg, state):
    b, h, s, dk = q.shape
    dv = v.shape[-1]
    nc = s // CHUNK

    def chunks(a):
        return jnp.moveaxis(a.reshape((b, h, nc, CHUNK) + a.shape[3:]), 2, 0)

    lf = jax.nn.log_sigmoid(fg)
    lower = jnp.tril(jnp.ones((CHUNK, CHUNK), dtype=bool))

    def step(carry, xs):
        c_st, n_st, m_st = carry
        qc, kc, vc, ic, fc = xs
        bc = jnp.cumsum(fc, axis=-1)
        m_inter = bc + m_st[..., None]
        dmat = jnp.where(lower, bc[..., :, None] - bc[..., None, :] + ic[..., None, :], -jnp.inf)
        m_t = jnp.maximum(m_inter, jnp.max(dmat, axis=-1))
        w_inter = jnp.exp(m_inter - m_t)
        p = jnp.exp(dmat - m_t[..., None]) * jnp.einsum("bhtd,bhsd->bhts", qc, kc)
        num = w_inter[..., None] * jnp.einsum("bhtd,bhde->bhte", qc, c_st) + jnp.einsum("bhts,bhse->bhte", p, vc)
        den = w_inter * jnp.einsum("bhtd,bhd->bht", qc, n_st) + jnp.sum(p, axis=-1)
        h_c = num / jnp.maximum(jnp.abs(den), jnp.exp(-m_t))[..., None]
        b_last = bc[..., -1]
        g_log = b_last[..., None] - bc + ic
        m_new = jnp.maximum(b_last + m_st, jnp.max(g_log, axis=-1))
        decay = jnp.exp(b_last + m_st - m_new)
        wk = jnp.exp(g_log - m_new[..., None])
        c_new = decay[..., None, None] * c_st + jnp.einsum("bhs,bhsd,bhse->bhde", wk, kc, vc)
        n_new = decay[..., None] * n_st + jnp.einsum("bhs,bhsd->bhd", wk, kc)
        return (c_new, n_new, m_new), h_c

    f32 = jnp.float32
    xs = (chunks(q.astype(f32)), chunks(k.astype(f32)), chunks(v.astype(f32)), chunks(ig), chunks(lf))
    state, hs = lax.scan(step, state, xs)
    return jnp.moveaxis(hs, 0, 2).reshape(b, h, s, dv), state


def mlstm_bidirectional(qc, kc, vc, gc, q, k, v, g):
    b, h, _, dk = q.shape
    dv = v.shape[-1]
    f32 = jnp.float32
    init = (jnp.zeros((b, h, dk, dv), f32), jnp.zeros((b, h, dk), f32), jnp.zeros((b, h), f32))

    def flip(a):
        return jnp.flip(a, axis=2)

    hc_f, st_f = mlstm_chunkwise(qc, kc, vc, gc[0], gc[1], init)
    hc_b, st_b = mlstm_chunkwise(flip(qc), flip(kc), flip(vc), flip(gc[2]), flip(gc[3]), init)
    h_f, _ = mlstm_chunkwise(q, k, v, g[0], g[1], st_f)
    h_b, _ = mlstm_chunkwise(flip(q), flip(k), flip(v), flip(g[2]), flip(g[3]), st_b)
    return h_f + flip(h_b), hc_f + flip(hc_b)


def mlstm_output(hm, o, g_hn):
    mu = jnp.mean(hm, axis=-1, keepdims=True)
    var = jnp.mean(jnp.square(hm - mu), axis=-1, keepdims=True)
    hn = (hm - mu) * lax.rsqrt(var + EPS) * g_hn[:, None, :]
    return (jax.nn.sigmoid(o.astype(jnp.float32)) * from_heads(hn)).astype(o.dtype)


def token_mixer(h, hc, ang_row, ang_col, w_in, g_q, w_uq, g_kv, w_ukv, ml_conv_w, ml_conv_b,
                w_mq, w_mk, b_gate, g_hn, w_out, update_ctx):
    z = h @ w_in
    zc = hc @ w_in
    qn, qr = mla_queries(z, g_q, w_uq)
    kn, kr, v = mla_keys_values(z, g_kv, w_ukv)
    kn_c, kr_c, v_c = mla_keys_values(zc, g_kv, w_ukv)
    q_lat = to_heads(jnp.concatenate([qn, rope2d(qr, ang_row, ang_col)], axis=-1))
    k_lat = to_heads(key_heads(kn, rope2d(kr, ang_row, ang_col)))
    k_ctx = to_heads(key_heads(kn_c, kr_c))
    v_lat, v_ctx = to_heads(v), to_heads(v_c)
    a = block_attention(q_lat, jnp.concatenate([k_ctx, k_lat], axis=2), jnp.concatenate([v_ctx, v_lat], axis=2))
    mq, mk, mv, o, gt = mlstm_project(z, ml_conv_w, ml_conv_b, w_mq, w_mk, b_gate)
    mq_c, mk_c, mv_c, o_c, gt_c = mlstm_project(zc, ml_conv_w, ml_conv_b, w_mq, w_mk, b_gate)
    hm, hm_c = mlstm_bidirectional(mq_c, mk_c, mv_c, gt_c, mq, mk, mv, gt)
    y = jnp.concatenate([from_heads(a), mlstm_output(hm, o, g_hn)], axis=-1) @ w_out
    if not update_ctx:
        return y, None
    qn_c, qr_c = mla_queries(zc, g_q, w_uq)
    a_c = block_attention(to_heads(jnp.concatenate([qn_c, qr_c], axis=-1)), k_ctx, v_ctx)
    yc = jnp.concatenate([from_heads(a_c), mlstm_output(hm_c, o_c, g_hn)], axis=-1) @ w_out
    return y, yc


def conv_ffn(h, w_up, conv_w, conv_b, w_down):
    gu = h @ w_up
    gate = dwconv_centred(gu[..., :D_FF], conv_w, conv_b)
    return (jax.nn.silu(gate) * gu[..., D_FF:]) @ w_down


def setup_inputs(seed: int = 0) -> dict:
    key = jax.random.key(seed)
    ks = jax.random.split(key, 26)
    f32 = jnp.float32

    def nrm(k, shape, s):
        return jax.random.normal(k, shape, f32) * s

    nl = DEPTH
    forget_bias = jnp.linspace(3.0, 6.0, ML_HEADS, dtype=f32)
    zeros_h = jnp.zeros((ML_HEADS,), f32)
    gate_base = jnp.stack([zeros_h, forget_bias, zeros_h, forget_bias]).reshape(-1)
    return {
        "x": nrm(ks[0], (BATCH, SEQ, D_MODEL), 1.0),
        "c": nrm(ks[1], (BATCH, D_MODEL), 1.0),
        "ctx": nrm(ks[2], (BATCH, CTX_LEN, D_MODEL), 1.0),
        "c_ctx": nrm(ks[3], (D_MODEL,), 1.0),
        "w_ada": nrm(ks[4], (nl, D_MODEL, N_MOD * D_MODEL), 0.5 * D_MODEL ** -0.5),
        "b_ada": nrm(ks[5], (nl, N_MOD * D_MODEL), 0.02),
        "w_in": nrm(ks[6], (nl, D_MODEL, IN_COLS), D_MODEL ** -0.5),
        "g_q": 1.0 + nrm(ks[7], (nl, Q_LORA), 0.02),
        "w_uq": nrm(ks[8], (nl, Q_LORA, MLA_HEADS * (MLA_NOPE + MLA_ROPE)), Q_LORA ** -0.5),
        "g_kv": 1.0 + nrm(ks[9], (nl, KV_LORA), 0.02),
        "w_ukv": nrm(ks[10], (nl, KV_LORA, MLA_HEADS * (MLA_NOPE + MLA_V)), KV_LORA ** -0.5),
        "ml_conv_w": nrm(ks[11], (nl, ML_CONV, ML_INNER), ML_CONV ** -0.5),
        "ml_conv_b": nrm(ks[12], (nl, ML_INNER), 0.02),
        "w_mq": nrm(ks[13], (nl, ML_HEADS, ML_V, ML_QK), ML_V ** -0.5),
        "w_mk": nrm(ks[14], (nl, ML_HEADS, ML_V, ML_QK), ML_V ** -0.5),
        "b_gate": gate_base + nrm(ks[15], (nl, N_GATES), 0.1),
        "g_hn": 1.0 + nrm(ks[16], (nl, ML_HEADS, ML_V), 0.02),
        "w_out": nrm(ks[17], (nl, MIX_WIDTH, D_MODEL), BETA * MIX_WIDTH ** -0.5),
        "ln1_g": 1.0 + nrm(ks[18], (nl, D_MODEL), 0.02),
        "ln1_b": nrm(ks[19], (nl, D_MODEL), 0.02),
        "w_up": nrm(ks[20], (nl, D_MODEL, 2 * D_FF), D_MODEL ** -0.5),
        "ffn_conv_w": nrm(ks[21], (nl, FFN_CONV, D_FF), FFN_CONV ** -0.5),
        "ffn_conv_b": nrm(ks[22], (nl, D_FF), 0.02),
        "w_down": nrm(ks[23], (nl, D_FF, D_MODEL), BETA * D_FF ** -0.5),
        "ln2_g": 1.0 + nrm(ks[24], (nl, D_MODEL), 0.02),
        "ln2_b": nrm(ks[25], (nl, D_MODEL), 0.02),
    }


def reference(x, c, ctx, c_ctx, w_ada, b_ada, w_in, g_q, w_uq, g_kv, w_ukv, ml_conv_w, ml_conv_b,
              w_mq, w_mk, b_gate, g_hn, w_out, ln1_g, ln1_b, w_up, ffn_conv_w, ffn_conv_b, w_down,
              ln2_g, ln2_b):
    n_rows = x.shape[1] // GRID_W
    ang_row, ang_col = axial_angles(n_rows)
    xc = ctx
    for layer in range(DEPTH):
        update_ctx = layer < DEPTH - 1
        mod = (jax.nn.silu(c) @ w_ada[layer] + b_ada[layer]).reshape(c.shape[0], 1, N_MOD, D_MODEL)
        mod_c = (jax.nn.silu(c_ctx) @ w_ada[layer] + b_ada[layer]).reshape(1, 1, N_MOD, D_MODEL)
        y, yc = token_mixer(modulate(x, mod, 0), modulate(xc, mod_c, 0), ang_row, ang_col,
                            w_in[layer], g_q[layer], w_uq[layer], g_kv[layer], w_ukv[layer],
                            ml_conv_w[layer], ml_conv_b[layer], w_mq[layer], w_mk[layer], b_gate[layer],
                            g_hn[layer], w_out[layer], update_ctx)
        x = layer_norm(ALPHA * x + mod[..., 2, :] * y, ln1_g[layer], ln1_b[layer])
        f = conv_ffn(modulate(x, mod, 3), w_up[layer], ffn_conv_w[layer], ffn_conv_b[layer], w_down[layer])
        x = layer_norm(ALPHA * x + mod[..., 5, :] * f, ln2_g[layer], ln2_b[layer])
        if update_ctx:
            xc = layer_norm(ALPHA * xc + mod_c[..., 2, :] * yc, ln1_g[layer], ln1_b[layer])
            fc = conv_ffn(modulate(xc, mod_c, 3), w_up[layer], ffn_conv_w[layer], ffn_conv_b[layer], w_down[layer])
            xc = layer_norm(ALPHA * xc + mod_c[..., 5, :] * fc, ln2_g[layer], ln2_b[layer])
    return x
```

```python
import functools

import jax
import jax.numpy as jnp
from jax import lax
from jax.experimental import pallas as pl
from jax.experimental.pallas import tpu as pltpu

f32 = jnp.float32
bf16 = jnp.bfloat16

GRID_W = 64
MLA_HEADS = 8
MLA_NOPE = 128
MLA_ROPE = 64
MLA_V = 128
Q_LORA = 512
KV_LORA = 256
ROPE_THETA = 10000.0
ML_HEADS = 4
ML_QK = 128
ML_V = 256
ML_INNER = ML_HEADS * ML_V
N_GATES = 4 * ML_HEADS
DEPTH = 1
OFF_CKV = Q_LORA
OFF_KR = Q_LORA + KV_LORA
OFF_U = OFF_KR + MLA_ROPE
OFF_V = OFF_U + ML_INNER
OFF_O = OFF_V + ML_INNER
OFF_G = OFF_O + ML_INNER
ALPHA = (2.0 * DEPTH) ** 0.25
EPS = 1e-6
N_MOD = 6

LANES = 128
SUBLANES = 8
MXU_DIM = 256
VMEM_LIMIT = 56 * 1024 * 1024

QK_PAD = 2 * LANES
GATE_STRIDE = SUBLANES
P_Q = 0
P_CKV = P_Q + Q_LORA
P_KR = P_CKV + KV_LORA
P_KRP = P_KR + LANES
P_G = P_KRP + LANES
P_U = P_G + LANES
P_V = P_U + ML_INNER
P_O = P_V + ML_INNER
P_COLS = P_O + ML_INNER

ROW_TILE = 256
CHUNK = 256
ATT_TQ = 512
ATT_TK_MAX = 768
FFN_TM = 512
FFN_TF = 512
HALO = SUBLANES


def _cparams(sem):
    return pltpu.CompilerParams(dimension_semantics=sem, vmem_limit_bytes=VMEM_LIMIT)


def _silu(x):
    return x * jax.nn.sigmoid(x)


def _dot(a, b):
    return jnp.dot(a, b, preferred_element_type=f32)


def _dot_nt(a, b):
    return lax.dot_general(a, b, (((1,), (1,)), ((), ())), preferred_element_type=f32)


def _layer_norm_rows(r, g, b):
    mu = jnp.mean(r, axis=-1, keepdims=True)
    d = r - mu
    var = jnp.mean(d * d, axis=-1, keepdims=True)
    return d * lax.rsqrt(var + EPS) * g + b


def _mod_kernel(c_ref, w_ref, b_ref, o_ref):
    s = _silu(c_ref[...])
    o_ref[...] = _dot(s.astype(bf16), w_ref[...].astype(bf16)) + b_ref[...]


def _modulation(cc, w_ada, b_ada):
    rows, d = cc.shape
    n = w_ada.shape[1]
    tn = min(n, 1536)
    assert n % tn == 0
    return pl.pallas_call(
        _mod_kernel,
        out_shape=jax.ShapeDtypeStruct((rows, n), f32),
        grid=(n // tn,),
        in_specs=[
            pl.BlockSpec((rows, d), lambda j: (0, 0)),
            pl.BlockSpec((d, tn), lambda j: (0, j)),
            pl.BlockSpec((1, tn), lambda j: (0, j)),
        ],
        out_specs=pl.BlockSpec((rows, tn), lambda j: (0, j)),
        compiler_params=_cparams(("arbitrary",)),
        name="adaln_mod",
    )(cc, w_ada, b_ada)


def _inproj_kernel(x_ref, mod_ref, w_ref, gq_ref, gkv_ref, ck_ref, sk_ref, bg_ref,
                   cq_ref, ckv_ref, kr_ref, g_ref, gt_ref, u_ref, v_ref, o_ref):
    shift = mod_ref[0:1, :]
    scale = mod_ref[1:2, :]
    h = (x_ref[...] * (1.0 + scale) + shift).astype(bf16)

    def sect(a, n):
        return _dot(h, w_ref[:, a:a + n])

    zq = sect(P_Q, Q_LORA)
    cq = zq * lax.rsqrt(jnp.mean(zq * zq, axis=-1, keepdims=True) + EPS) * gq_ref[...]
    cq_ref[...] = cq.astype(bf16)
    zkv = sect(P_CKV, KV_LORA)
    ckv = zkv * lax.rsqrt(jnp.mean(zkv * zkv, axis=-1, keepdims=True) + EPS) * gkv_ref[...]
    ckv_ref[...] = ckv.astype(bf16)
    kr = sect(P_KR, LANES) * ck_ref[...] + sect(P_KRP, LANES) * sk_ref[...]
    kr_ref[...] = kr.astype(bf16)
    g = sect(P_G, LANES) + bg_ref[...]
    g_ref[...] = g
    gt_ref[...] = g.T[0:4 * GATE_STRIDE, :]
    u_ref[...] = sect(P_U, ML_INNER).astype(bf16)
    v_ref[...] = sect(P_V, ML_INNER).astype(bf16)
    o_ref[...] = sect(P_O, ML_INNER).astype(bf16)


def _in_projection(xcat, modr, w_in_p, g_q, g_kv, ck, sk, bg, n_ctx_tiles):
    b, t, d = xcat.shape
    tm = ROW_TILE
    nt = t // tm
    n_batch = b

    def mod_idx(bi, i):
        return (jnp.where(i < n_ctx_tiles, n_batch, bi), 0, 0)

    row = lambda w: pl.BlockSpec((None, tm, w), lambda bi, i: (bi, i, 0))
    const = lambda r, c: pl.BlockSpec((r, c), lambda bi, i: (0, 0))
    out_shapes = (
        jax.ShapeDtypeStruct((b, t, Q_LORA), bf16),
        jax.ShapeDtypeStruct((b, t, KV_LORA), bf16),
        jax.ShapeDtypeStruct((b, t, LANES), bf16),
        jax.ShapeDtypeStruct((b, t, LANES), f32),
        jax.ShapeDtypeStruct((b, 4 * GATE_STRIDE, t), f32),
        jax.ShapeDtypeStruct((b, t, ML_INNER), bf16),
        jax.ShapeDtypeStruct((b, t, ML_INNER), bf16),
        jax.ShapeDtypeStruct((b, t, ML_INNER), bf16),
    )
    out_specs = (
        row(Q_LORA), row(KV_LORA), row(LANES), row(LANES),
        pl.BlockSpec((None, 4 * GATE_STRIDE, tm), lambda bi, i: (bi, 0, i)),
        row(ML_INNER), row(ML_INNER), row(ML_INNER),
    )
    return pl.pallas_call(
        _inproj_kernel,
        out_shape=out_shapes,
        grid=(b, nt),
        in_specs=[
            row(d),
            pl.BlockSpec((None, N_MOD, d), mod_idx),
            const(d, P_COLS),
            const(1, Q_LORA),
            const(1, KV_LORA),
            pl.BlockSpec((tm, LANES), lambda bi, i: (i, 0)),
            pl.BlockSpec((tm, LANES), lambda bi, i: (i, 0)),
            const(1, LANES),
        ],
        out_specs=out_specs,
        compiler_params=_cparams(("arbitrary", "arbitrary")),
        name="in_projection",
    )(xcat, modr, w_in_p, g_q, g_kv, ck, sk, bg)


def _kv_kernel(ckv_ref, kr_ref, wk_ref, wvt_ref, k_ref, vt_ref):
    ckv = ckv_ref[...]
    kn = _dot(ckv, wk_ref[...]).astype(bf16)
    kr = kr_ref[...]
    for h in range(MLA_HEADS):
        k_ref[h, :, 0:MLA_NOPE] = kn[:, h * MLA_NOPE:(h + 1) * MLA_NOPE]
        k_ref[h, :, MLA_NOPE:QK_PAD] = kr
        vt_ref[h] = _dot_nt(wvt_ref[h], ckv).astype(bf16)


def _kv_prep(ckv, kr, wk, wvt):
    b, t, _ = ckv.shape
    tm = ROW_TILE
    return pl.pallas_call(
        _kv_kernel,
        out_shape=(
            jax.ShapeDtypeStruct((b, MLA_HEADS, t, QK_PAD), bf16),
            jax.ShapeDtypeStruct((b, MLA_HEADS, MLA_V, t), bf16),
        ),
        grid=(b, t // tm),
        in_specs=[
            pl.BlockSpec((None, tm, KV_LORA), lambda bi, i: (bi, i, 0)),
            pl.BlockSpec((None, tm, LANES), lambda bi, i: (bi, i, 0)),
            pl.BlockSpec((KV_LORA, MLA_HEADS * MLA_NOPE), lambda bi, i: (0, 0)),
            pl.BlockSpec((MLA_HEADS, MLA_V, KV_LORA), lambda bi, i: (0, 0, 0)),
        ],
        out_specs=(
            pl.BlockSpec((None, MLA_HEADS, tm, QK_PAD), lambda bi, i: (bi, 0, i, 0)),
            pl.BlockSpec((None, MLA_HEADS, MLA_V, tm), lambda bi, i: (bi, 0, 0, i)),
        ),
        compiler_params=_cparams(("arbitrary", "arbitrary")),
        name="kv_prep",
    )(ckv, kr, wk, wvt)


def _q_kernel(cq_ref, wm_ref, wp_ref, ct_ref, st_ref, qt_ref, *, scale):
    cq = cq_ref[...]
    ct = ct_ref[...]
    st = st_ref[...]
    for h in range(MLA_HEADS):
        qm = _dot_nt(wm_ref[h], cq)
        qp = _dot_nt(wp_ref[h], cq)
        qt_ref[h, 0:MLA_NOPE, :] = (qm[0:MLA_NOPE] * scale).astype(bf16)
        rope = qm[MLA_NOPE:QK_PAD] * ct + qp * st
        qt_ref[h, MLA_NOPE:QK_PAD, :] = (rope * scale).astype(bf16)


def _q_prep(cq, wm, wp, ct, st, n_ctx_tiles):
    b, t, _ = cq.shape
    tm = ROW_TILE
    s = t - n_ctx_tiles * tm
    scale = float((MLA_NOPE + MLA_ROPE) ** -0.5)
    return pl.pallas_call(
        functools.partial(_q_kernel, scale=scale),
        out_shape=jax.ShapeDtypeStruct((b, MLA_HEADS, QK_PAD, s), bf16),
        grid=(b, s // tm),
        in_specs=[
            pl.BlockSpec((None, tm, Q_LORA), lambda bi, i: (bi, i + n_ctx_tiles, 0)),
            pl.BlockSpec((MLA_HEADS, QK_PAD, Q_LORA), lambda bi, i: (0, 0, 0)),
            pl.BlockSpec((MLA_HEADS, LANES, Q_LORA), lambda bi, i: (0, 0, 0)),
            pl.BlockSpec((LANES, tm), lambda bi, i: (0, i)),
            pl.BlockSpec((LANES, tm), lambda bi, i: (0, i)),
        ],
        out_specs=pl.BlockSpec((None, MLA_HEADS, QK_PAD, tm), lambda bi, i: (bi, 0, 0, i)),
        compiler_params=_cparams(("arbitrary", "arbitrary")),
        name="q_prep",
    )(cq, wm, wp, ct, st)


def _attn_kernel(qt_ref, k_ref, vt_ref, o_ref, *, tk, nk):
    qt = qt_ref[...]
    tq = qt.shape[1]
    m = jnp.full((1, tq), -jnp.inf, f32)
    l = jnp.zeros((1, tq), f32)
    acc = jnp.zeros((MLA_V, tq), f32)
    for j in range(nk):
        s = _dot(k_ref[j * tk:(j + 1) * tk, :], qt)
        m_new = jnp.maximum(m, jnp.max(s, axis=0, keepdims=True))
        alpha = jnp.exp(m - m_new)
        p = jnp.exp(s - m_new)
        l = alpha * l + jnp.sum(p, axis=0, keepdims=True)
        acc = alpha * acc + _dot(vt_ref[:, j * tk:(j + 1) * tk], p.astype(bf16))
        m = m_new
    o_ref[...] = (acc / l).T.astype(o_ref.dtype)


def _attention(qt, k, vt):
    b, hh, _, s = qt.shape
    t = k.shape[2]
    tq = min(ATT_TQ, s)
    tk = MXU_DIM
    for cand in range(MXU_DIM, ATT_TK_MAX + 1, MXU_DIM):
        if t % cand == 0:
            tk = cand
    assert s % tq == 0 and t % tk == 0
    return pl.pallas_call(
        functools.partial(_attn_kernel, tk=tk, nk=t // tk),
        out_shape=jax.ShapeDtypeStruct((b, s, hh * MLA_V), bf16),
        grid=(b, hh, s // tq),
        in_specs=[
            pl.BlockSpec((None, None, QK_PAD, tq), lambda bi, h, i: (bi, h, 0, i)),
            pl.BlockSpec((None, None, t, QK_PAD), lambda bi, h, i: (bi, h, 0, 0)),
            pl.BlockSpec((None, None, MLA_V, t), lambda bi, h, i: (bi, h, 0, 0)),
        ],
        out_specs=pl.BlockSpec((None, tq, MLA_V), lambda bi, h, i: (bi, i, h)),
        compiler_params=_cparams(("arbitrary", "arbitrary", "arbitrary")),
        name="flash_attention",
    )(qt, k, vt)


def _mproj_kernel(u_ref, up_ref, un_ref, cw_ref, cb_ref, w_ref, qk_ref, *, n_ctx_tiles, n_tiles):
    i = pl.program_id(1)
    u = u_ref[...].astype(f32)
    tm = u.shape[0]
    first = jnp.logical_or(i == 0, i == n_ctx_tiles)
    last = jnp.logical_or(i == n_ctx_tiles - 1, i == n_tiles - 1)
    prev_row = jnp.where(first, 0.0, up_ref[HALO - 1:HALO, :].astype(f32))
    next_row = jnp.where(last, 0.0, un_ref[0:1, :].astype(f32))
    ridx = lax.broadcasted_iota(jnp.int32, (tm, 1), 0)
    u_prev = jnp.where(ridx == 0, prev_row, pltpu.roll(u, 1, 0))
    u_next = jnp.where(ridx == tm - 1, next_row, pltpu.roll(u, tm - 1, 0))
    y = cw_ref[0:1, :] * u_prev + cw_ref[1:2, :] * u + cw_ref[2:3, :] * u_next + cb_ref[...]
    act = _silu(y).astype(bf16)
    lane = lax.broadcasted_iota(jnp.int32, (1, 2 * ML_QK), 1)
    qscale = jnp.where(lane < ML_QK, float(ML_QK ** -0.5), 1.0)
    for h in range(ML_HEADS):
        qk = _dot(act[:, h * ML_V:(h + 1) * ML_V], w_ref[h]) * qscale
        qk_ref[:, h * 2 * ML_QK:(h + 1) * 2 * ML_QK] = qk.astype(bf16)


def _mlstm_projection(u, conv_w, conv_b, w_qk, n_ctx_tiles):
    b, t, _ = u.shape
    tm = ROW_TILE
    nt = t // tm
    hb = tm // HALO
    return pl.pallas_call(
        functools.partial(_mproj_kernel, n_ctx_tiles=n_ctx_tiles, n_tiles=nt),
        out_shape=jax.ShapeDtypeStruct((b, t, ML_HEADS * 2 * ML_QK), bf16),
        grid=(b, nt),
        in_specs=[
            pl.BlockSpec((None, tm, ML_INNER), lambda bi, i: (bi, i, 0)),
            pl.BlockSpec((None, HALO, ML_INNER), lambda bi, i: (bi, jnp.maximum(i * hb - 1, 0), 0)),
            pl.BlockSpec((None, HALO, ML_INNER),
                         lambda bi, i: (bi, jnp.minimum((i + 1) * hb, nt * hb - 1), 0)),
            pl.BlockSpec((3, ML_INNER), lambda bi, i: (0, 0)),
            pl.BlockSpec((1, ML_INNER), lambda bi, i: (0, 0)),
            pl.BlockSpec((ML_HEADS, ML_V, 2 * ML_QK), lambda bi, i: (0, 0, 0)),
        ],
        out_specs=pl.BlockSpec((None, tm, ML_HEADS * 2 * ML_QK), lambda bi, i: (bi, i, 0)),
        compiler_params=_cparams(("arbitrary", "arbitrary")),
        name="mlstm_projection",
    )(u, u, u, conv_w, conv_b, w_qk)


def _scan_kernel(qk_ref, v_ref, g_ref, gt_ref, h_ref, c_scr, n_scr, m_scr):
    d = pl.program_id(0)
    j = pl.program_id(2)
    is_fwd = d == 0
    L = qk_ref.shape[0]
    hi = lax.Precision.HIGHEST

    @pl.when(j == 0)
    def _():
        c_scr[...] = jnp.zeros_like(c_scr)
        n_scr[...] = jnp.zeros_like(n_scr)
        m_scr[...] = jnp.zeros_like(m_scr)

    rr = lax.broadcasted_iota(jnp.int32, (L, L), 0)
    cc = lax.broadcasted_iota(jnp.int32, (L, L), 1)
    order = (rr - cc) * jnp.where(is_fwd, 1, -1)
    causal = order >= 0
    causal_f = jnp.where(causal, 1.0, 0.0)
    anti_f = jnp.where(order <= 0, 1.0, 0.0)

    g = g_ref[...]
    gt = gt_ref[...]
    GS = GATE_STRIDE
    lf_col = jax.nn.log_sigmoid(g)
    lf_row = jax.nn.log_sigmoid(gt)
    bc_col = jnp.dot(causal_f, lf_col, precision=hi, preferred_element_type=f32)
    bc_row = jnp.dot(lf_row, anti_f, precision=hi, preferred_element_type=f32)

    for h in range(ML_HEADS):
        ig_c = jnp.where(is_fwd, g[:, h:h + 1], g[:, 2 * GS + h:2 * GS + h + 1])
        bc_c = jnp.where(is_fwd, bc_col[:, GS + h:GS + h + 1], bc_col[:, 3 * GS + h:3 * GS + h + 1])
        ig_r = jnp.where(is_fwd, gt[h:h + 1, :], gt[2 * GS + h:2 * GS + h + 1, :])
        bc_r = jnp.where(is_fwd, bc_row[GS + h:GS + h + 1, :], bc_row[3 * GS + h:3 * GS + h + 1, :])
        a = jnp.where(is_fwd, bc_r[:, L - 1:L], bc_r[:, 0:1])

        q = qk_ref[:, h * 2 * ML_QK:h * 2 * ML_QK + ML_QK]
        k = qk_ref[:, h * 2 * ML_QK + ML_QK:(h + 1) * 2 * ML_QK]
        v = v_ref[:, h * ML_V:(h + 1) * ML_V]
        c_st = c_scr[h]
        n_st = n_scr[h]
        m_st = m_scr[h]

        dmat = jnp.where(causal, bc_c - bc_r + ig_r, -jnp.inf)
        m_inter = bc_c + m_st
        m_t = jnp.maximum(m_inter, jnp.max(dmat, axis=1, keepdims=True))
        w_inter = jnp.exp(m_inter - m_t)
        p = jnp.exp(dmat - m_t) * _dot_nt(q, k)
        num = w_inter * _dot(q, c_st.astype(bf16)) + _dot(p.astype(bf16), v)
        den = (w_inter * jnp.sum(q.astype(f32) * n_st, axis=1, keepdims=True)
               + jnp.sum(p, axis=1, keepdims=True))
        h_ref[:, h * ML_V:(h + 1) * ML_V] = num / jnp.maximum(jnp.abs(den), jnp.exp(-m_t))

        g_log_r = a - bc_r + ig_r
        m_new = jnp.maximum(a + m_st, jnp.max(g_log_r, axis=1, keepdims=True))
        decay = jnp.exp(a + m_st - m_new)
        wk_c = jnp.exp(a - bc_c + ig_c - m_new)
        kw = k.astype(f32) * wk_c
        c_scr[h] = decay * c_st + _dot(kw.T.astype(bf16), v)
        n_scr[h] = decay * n_st + jnp.sum(kw, axis=0, keepdims=True)
        m_scr[h] = m_new


def _mlstm_scan(qk, v, g, gt):
    b, t, _ = qk.shape
    L = CHUNK
    nc = t // L

    def cidx(d, j):
        return jnp.where(d == 0, j, jnp.where(j == 0, 0, nc - j))

    return pl.pallas_call(
        _scan_kernel,
        out_shape=jax.ShapeDtypeStruct((2, b, t, ML_INNER), f32),
        grid=(2, b, nc),
        in_specs=[
            pl.BlockSpec((None, L, ML_HEADS * 2 * ML_QK), lambda d, bi, j: (bi, cidx(d, j), 0)),
            pl.BlockSpec((None, L, ML_INNER), lambda d, bi, j: (bi, cidx(d, j), 0)),
            pl.BlockSpec((None, L, LANES), lambda d, bi, j: (bi, cidx(d, j), 0)),
            pl.BlockSpec((None, 4 * GATE_STRIDE, L), lambda d, bi, j: (bi, 0, cidx(d, j))),
        ],
        out_specs=pl.BlockSpec((None, None, L, ML_INNER), lambda d, bi, j: (d, bi, cidx(d, j), 0)),
        scratch_shapes=[
            pltpu.VMEM((ML_HEADS, ML_QK, ML_V), f32),
            pltpu.VMEM((ML_HEADS, 1, ML_QK), f32),
            pltpu.VMEM((ML_HEADS, 1, 1), f32),
        ],
        compiler_params=_cparams(("arbitrary", "arbitrary", "arbitrary")),
        name="mlstm_scan",
    )(qk, v, g, gt)


def _outproj_kernel(a_ref, hf_ref, hb_ref, o_ref, x_ref, mod_ref, ghn_ref, w_ref, lg_ref, lb_ref,
                    out_ref):
    hm = hf_ref[...] + hb_ref[...]
    og = jax.nn.sigmoid(o_ref[...].astype(f32))
    ghn = ghn_ref[...]
    y = _dot(a_ref[...], w_ref[0:MLA_HEADS * MLA_V, :])
    for h in range(ML_HEADS):
        sl = slice(h * ML_V, (h + 1) * ML_V)
        seg = hm[:, sl]
        mu = jnp.mean(seg, axis=-1, keepdims=True)
        dd = seg - mu
        var = jnp.mean(dd * dd, axis=-1, keepdims=True)
        mo = og[:, sl] * (dd * lax.rsqrt(var + EPS) * ghn[:, sl])
        r0 = MLA_HEADS * MLA_V + h * ML_V
        y = y + _dot(mo.astype(bf16), w_ref[r0:r0 + ML_V, :])
    gate = mod_ref[2:3, :]
    r = ALPHA * x_ref[...] + gate * y
    out_ref[...] = _layer_norm_rows(r, lg_ref[...], lb_ref[...])


def _out_projection(a, hdir, o, x, modr, g_hn, w_out, ln_g, ln_b, n_ctx_tiles):
    b, s, d = x.shape
    tm = ROW_TILE
    off = n_ctx_tiles
    mix = w_out.shape[0]
    return pl.pallas_call(
        _outproj_kernel,
        out_shape=jax.ShapeDtypeStruct((b, s, d), f32),
        grid=(b, s // tm),
        in_specs=[
            pl.BlockSpec((None, tm, MLA_HEADS * MLA_V), lambda bi, i: (bi, i, 0)),
            pl.BlockSpec((None, None, tm, ML_INNER), lambda bi, i: (0, bi, i + off, 0)),
            pl.BlockSpec((None, None, tm, ML_INNER), lambda bi, i: (1, bi, i + off, 0)),
            pl.BlockSpec((None, tm, ML_INNER), lambda bi, i: (bi, i + off, 0)),
            pl.BlockSpec((None, tm, d), lambda bi, i: (bi, i, 0)),
            pl.BlockSpec((None, N_MOD, d), lambda bi, i: (bi, 0, 0)),
            pl.BlockSpec((1, ML_INNER), lambda bi, i: (0, 0)),
            pl.BlockSpec((mix, d), lambda bi, i: (0, 0)),
            pl.BlockSpec((1, d), lambda bi, i: (0, 0)),
            pl.BlockSpec((1, d), lambda bi, i: (0, 0)),
        ],
        out_specs=pl.BlockSpec((None, tm, d), lambda bi, i: (bi, i, 0)),
        compiler_params=_cparams(("arbitrary", "arbitrary")),
        name="out_projection",
    )(a, hdir, hdir, o, x, modr, g_hn, w_out, ln_g, ln_b)


def _ffn_kernel(x_ref, xp_ref, xn_ref, mod_ref, wg_ref, wu_ref, cw_ref, cb_ref, wd_ref, lg_ref, lb_ref,
                out_ref, hs_ref, acc_ref, gs_ref, *, n_row_tiles):
    i = pl.program_id(1)
    j = pl.program_id(2)
    nf = pl.num_programs(2)
    tm = x_ref.shape[0]
    shift = mod_ref[3:4, :]
    scale = mod_ref[4:5, :]

    @pl.when(j == 0)
    def _():
        hs_ref[0:HALO, :] = (xp_ref[...] * (1.0 + scale) + shift).astype(bf16)
        hs_ref[HALO:HALO + tm, :] = (x_ref[...] * (1.0 + scale) + shift).astype(bf16)
        hs_ref[HALO + tm:2 * HALO + tm, :] = (xn_ref[...] * (1.0 + scale) + shift).astype(bf16)
        acc_ref[...] = jnp.zeros_like(acc_ref)

    gs_ref[...] = _dot(hs_ref[...], wg_ref[...])
    ridx = lax.broadcasted_iota(jnp.int32, (tm, 1), 0)
    g_prev = gs_ref[HALO - 1:HALO - 1 + tm, :]
    g_cur = gs_ref[HALO:HALO + tm, :]
    g_next = gs_ref[HALO + 1:HALO + 1 + tm, :]
    g_prev = jnp.where(jnp.logical_and(ridx == 0, i == 0), 0.0, g_prev)
    g_next = jnp.where(jnp.logical_and(ridx == tm - 1, i == n_row_tiles - 1), 0.0, g_next)
    gate = cw_ref[0:1, :] * g_prev + cw_ref[1:2, :] * g_cur + cw_ref[2:3, :] * g_next + cb_ref[...]
    up = _dot(hs_ref[HALO:HALO + tm, :], wu_ref[...])
    act = (_silu(gate) * up).astype(bf16)
    acc_ref[...] += _dot(act, wd_ref[...])

    @pl.when(j == nf - 1)
    def _():
        r = ALPHA * x_ref[...] + mod_ref[5:6, :] * acc_ref[...]
        out_ref[...] = _layer_norm_rows(r, lg_ref[...], lb_ref[...])


def _conv_ffn(x, modr, w_up, conv_w, conv_b, w_down, ln_g, ln_b):
    b, s, d = x.shape
    dff = w_down.shape[0]
    tm = min(FFN_TM, s)
    tf = min(FFN_TF, dff)
    assert s % tm == 0 and dff % tf == 0
    nr = s // tm
    nf = dff // tf
    hb = tm // HALO
    return pl.pallas_call(
        functools.partial(_ffn_kernel, n_row_tiles=nr),
        out_shape=jax.ShapeDtypeStruct((b, s, d), f32),
        grid=(b, nr, nf),
        in_specs=[
            pl.BlockSpec((None, tm, d), lambda bi, i, j: (bi, i, 0)),
            pl.BlockSpec((None, HALO, d), lambda bi, i, j: (bi, jnp.maximum(i * hb - 1, 0), 0)),
            pl.BlockSpec((None, HALO, d), lambda bi, i, j: (bi, jnp.minimum((i + 1) * hb, nr * hb - 1), 0)),
            pl.BlockSpec((None, N_MOD, d), lambda bi, i, j: (bi, 0, 0)),
            pl.BlockSpec((d, tf), lambda bi, i, j: (0, j)),
            pl.BlockSpec((d, tf), lambda bi, i, j: (0, nf + j)),
            pl.BlockSpec((3, tf), lambda bi, i, j: (0, j)),
            pl.BlockSpec((1, tf), lambda bi, i, j: (0, j)),
            pl.BlockSpec((tf, d), lambda bi, i, j: (j, 0)),
            pl.BlockSpec((1, d), lambda bi, i, j: (0, 0)),
            pl.BlockSpec((1, d), lambda bi, i, j: (0, 0)),
        ],
        out_specs=pl.BlockSpec((None, tm, d), lambda bi, i, j: (bi, i, 0)),
        scratch_shapes=[
            pltpu.VMEM((tm + 2 * HALO, d), bf16),
            pltpu.VMEM((tm, d), f32),
            pltpu.VMEM((tm + 2 * HALO, tf), f32),
        ],
        compiler_params=_cparams(("arbitrary", "arbitrary", "arbitrary")),
        name="conv_ffn",
    )(x, x, x, modr, w_up, w_up, conv_w, conv_b, w_down, ln_g, ln_b)


def _rope_perm():
    q = MLA_ROPE // 4
    return jnp.concatenate([jnp.arange(q, 2 * q), jnp.arange(0, q),
                            jnp.arange(3 * q, 4 * q), jnp.arange(2 * q, 3 * q)])


def _rope_tables(s):
    n_rows = s // GRID_W
    row = jnp.repeat(jnp.arange(n_rows), GRID_W).astype(f32)
    col = jnp.tile(jnp.arange(GRID_W), n_rows).astype(f32)
    n_freq = MLA_ROPE // 4
    inv = ROPE_THETA ** (-jnp.arange(n_freq, dtype=f32) / n_freq)
    ar = row[:, None] * inv
    ac = col[:, None] * inv
    cos = jnp.concatenate([jnp.cos(ar), jnp.cos(ar), jnp.cos(ac), jnp.cos(ac)], axis=-1)
    sin = jnp.concatenate([-jnp.sin(ar), jnp.sin(ar), -jnp.sin(ac), jnp.sin(ac)], axis=-1)
    return cos, sin


def _pack_w_in(w_in):
    d = w_in.shape[0]
    z = lambda n: jnp.zeros((d, n), w_in.dtype)
    kr = w_in[:, OFF_KR:OFF_U]
    gates = []
    for gi in range(4):
        gates += [w_in[:, OFF_G + gi * ML_HEADS:OFF_G + (gi + 1) * ML_HEADS], z(GATE_STRIDE - ML_HEADS)]
    cols = [w_in[:, 0:Q_LORA], w_in[:, OFF_CKV:OFF_KR],
            kr, z(LANES - MLA_ROPE), kr[:, _rope_perm()], z(LANES - MLA_ROPE),
            *gates, z(LANES - 4 * GATE_STRIDE),
            w_in[:, OFF_U:OFF_V], w_in[:, OFF_V:OFF_O], w_in[:, OFF_O:OFF_G]]
    return jnp.concatenate(cols, axis=1).astype(bf16)


def _pack_b_gate(b_gate):
    parts = []
    for gi in range(4):
        parts += [b_gate[gi * ML_HEADS:(gi + 1) * ML_HEADS], jnp.zeros((GATE_STRIDE - ML_HEADS,), f32)]
    parts.append(jnp.zeros((LANES - 4 * GATE_STRIDE,), f32))
    return jnp.concatenate(parts)[None, :]


def kernel(x, c, ctx, c_ctx, w_ada, b_ada, w_in, g_q, w_uq, g_kv, w_ukv, ml_conv_w, ml_conv_b, w_mq, w_mk, b_gate, g_hn, w_out, ln1_g, ln1_b, w_up, ffn_conv_w, ffn_conv_b, w_down, ln2_g, ln2_b):
    b, s, d = x.shape
    n_ctx = ctx.shape[1]
    assert n_ctx % ROW_TILE == 0 and s % ROW_TILE == 0 and n_ctx % CHUNK == 0 and s % CHUNK == 0
    n_ctx_tiles = n_ctx // ROW_TILE
    layer = 0

    rows = -(-(b + 1) // SUBLANES) * SUBLANES
    cc = jnp.concatenate([c, c_ctx[None, :], jnp.zeros((rows - b - 1, d), f32)], axis=0)
    modr = _modulation(cc, w_ada[layer], b_ada[layer][None, :]).reshape(rows, N_MOD, d)

    cos, sin = _rope_tables(s)
    pad = jnp.zeros((s, LANES - MLA_ROPE), f32)
    ck = jnp.concatenate([jnp.ones((n_ctx, LANES), f32), jnp.concatenate([cos, pad], axis=1)], axis=0)
    sk = jnp.concatenate([jnp.zeros((n_ctx, LANES), f32), jnp.concatenate([sin, pad], axis=1)], axis=0)
    ct = jnp.concatenate([cos, pad], axis=1).T
    st = jnp.concatenate([sin, pad], axis=1).T

    xcat = jnp.concatenate([ctx, x], axis=1)
    cq, ckv, kr, g, gt, u, v, o = _in_projection(
        xcat, modr, _pack_w_in(w_in[layer]), g_q[layer][None, :], g_kv[layer][None, :], ck, sk,
        _pack_b_gate(b_gate[layer]), n_ctx_tiles)

    dq = MLA_NOPE + MLA_ROPE
    wq = w_uq[layer].reshape(Q_LORA, MLA_HEADS, dq)
    wq_nope = jnp.transpose(wq[:, :, :MLA_NOPE], (1, 2, 0))
    wq_rope = jnp.transpose(wq[:, :, MLA_NOPE:], (1, 2, 0))
    zq = jnp.zeros((MLA_HEADS, QK_PAD - dq, Q_LORA), f32)
    wm = jnp.concatenate([wq_nope, wq_rope, zq], axis=1).astype(bf16)
    wp = jnp.concatenate([wq_rope[:, _rope_perm(), :],
                          jnp.zeros((MLA_HEADS, LANES - MLA_ROPE, Q_LORA), f32)], axis=1).astype(bf16)
    wkv = w_ukv[layer].reshape(KV_LORA, MLA_HEADS, MLA_NOPE + MLA_V)
    wk = wkv[:, :, :MLA_NOPE].reshape(KV_LORA, MLA_HEADS * MLA_NOPE).astype(bf16)
    wvt = jnp.transpose(wkv[:, :, MLA_NOPE:], (1, 2, 0)).astype(bf16)
    kk, vt = _kv_prep(ckv, kr, wk, wvt)
    qt = _q_prep(cq, wm, wp, ct, st, n_ctx_tiles)
    a = _attention(qt, kk, vt)

    w_qk = jnp.concatenate([w_mq[layer], w_mk[layer]], axis=-1).astype(bf16)
    qk = _mlstm_projection(u, ml_conv_w[layer], ml_conv_b[layer][None, :], w_qk, n_ctx_tiles)
    hdir = _mlstm_scan(qk, v, g, gt)

    x1 = _out_projection(a, hdir, o, x, modr, g_hn[layer].reshape(1, ML_INNER), w_out[layer].astype(bf16),
                         ln1_g[layer][None, :], ln1_b[layer][None, :], n_ctx_tiles)
    return _conv_ffn(x1, modr, w_up[layer].astype(bf16), ffn_conv_w[layer], ffn_conv_b[layer][None, :],
                     w_down[layer].astype(bf16), ln2_g[layer][None, :], ln2_b[layer][None, :])
```

```python
import functools

import jax
import jax.numpy as jnp
from jax import lax
from jax.experimental import pallas as pl
from jax.experimental.pallas import tpu as pltpu

f32 = jnp.float32
bf16 = jnp.bfloat16

GRID_W = 64
MLA_HEADS = 8
MLA_NOPE = 128
MLA_ROPE = 64
MLA_V = 128
Q_LORA = 512
KV_LORA = 256
ROPE_THETA = 10000.0
ML_HEADS = 4
ML_QK = 128
ML_V = 256
ML_INNER = ML_HEADS * ML_V
N_GATES = 4 * ML_HEADS
DEPTH = 1
OFF_CKV = Q_LORA
OFF_KR = Q_LORA + KV_LORA
OFF_U = OFF_KR + MLA_ROPE
OFF_V = OFF_U + ML_INNER
OFF_O = OFF_V + ML_INNER
OFF_G = OFF_O + ML_INNER
ALPHA = (2.0 * DEPTH) ** 0.25
EPS = 1e-6
N_MOD = 6

LANES = 128
SUBLANES = 8
MXU_DIM = 256
VMEM_LIMIT = 56 * 1024 * 1024

QK_PAD = 2 * LANES
GATE_STRIDE = SUBLANES
P_Q = 0
P_CKV = P_Q + Q_LORA
P_KR = P_CKV + KV_LORA
P_U = P_KR + LANES
P_V = P_U + ML_INNER
P_O = P_V + ML_INNER
P_G = P_O + ML_INNER
P_COLS = P_G + LANES
STAB_ROWS = 16
STAB_SLACK = 1.02
L_MIN = 2.0 ** -100

ROW_TILE = 256
CHUNK = 256
ATT_TQ = 1024
ATT_TK_MAX = 768
FFN_TM = 512
FFN_TF = 512
HALO = SUBLANES


def _cparams(sem):
    return pltpu.CompilerParams(dimension_semantics=sem, vmem_limit_bytes=VMEM_LIMIT)


def _silu(x):
    return x * jax.nn.sigmoid(x)


def _dot(a, b):
    return jnp.dot(a, b, preferred_element_type=f32)


def _dot_nt(a, b):
    return lax.dot_general(a, b, (((1,), (1,)), ((), ())), preferred_element_type=f32)


def _layer_norm_rows(r, g, b):
    mu = jnp.mean(r, axis=-1, keepdims=True)
    d = r - mu
    var = jnp.mean(d * d, axis=-1, keepdims=True)
    return d * lax.rsqrt(var + EPS) * g + b


def _mod_kernel(c_ref, w_ref, b_ref, o_ref):
    s = _silu(c_ref[...])
    o_ref[...] = _dot(s.astype(bf16), w_ref[...].astype(bf16)) + b_ref[...]


def _modulation(cc, w_ada, b_ada):
    rows, d = cc.shape
    n = w_ada.shape[1]
    tn = min(n, 1536)
    assert n % tn == 0
    return pl.pallas_call(
        _mod_kernel,
        out_shape=jax.ShapeDtypeStruct((rows, n), f32),
        grid=(n // tn,),
        in_specs=[
            pl.BlockSpec((rows, d), lambda j: (0, 0)),
            pl.BlockSpec((d, tn), lambda j: (0, j)),
            pl.BlockSpec((1, tn), lambda j: (0, j)),
        ],
        out_specs=pl.BlockSpec((rows, tn), lambda j: (0, j)),
        compiler_params=_cparams(("arbitrary",)),
        name="adaln_mod",
    )(cc, w_ada, b_ada)


def _inproj_kernel(ctx_ref, x_ref, mod_ref, w_ref, gq_ref, gkv_ref, ck_ref, sk_ref, bg_ref,
                   cq_ref, ckv_ref, kr_ref, g_ref, gt_ref, u_ref, v_ref, o_ref, h_ref, *, n_ctx_tiles):
    shift = mod_ref[0:1, :]
    scale = mod_ref[1:2, :]
    is_ctx = pl.program_id(1) < n_ctx_tiles

    @pl.when(is_ctx)
    def _():
        h_ref[...] = (ctx_ref[...] * (1.0 + scale) + shift).astype(bf16)

    @pl.when(jnp.logical_not(is_ctx))
    def _():
        h_ref[...] = (x_ref[...] * (1.0 + scale) + shift).astype(bf16)

    def sect(a, n):
        return _dot(h_ref[...], w_ref[:, a:a + n])

    zq = sect(P_Q, Q_LORA)
    cq = zq * lax.rsqrt(jnp.mean(zq * zq, axis=-1, keepdims=True) + EPS) * gq_ref[...]
    cq_ref[...] = cq.astype(bf16)
    zkv = sect(P_CKV, KV_LORA)
    ckv = zkv * lax.rsqrt(jnp.mean(zkv * zkv, axis=-1, keepdims=True) + EPS) * gkv_ref[...]
    ckv_ref[...] = ckv.astype(bf16)
    zkr = sect(P_KR, LANES)
    kr = zkr * ck_ref[...] + pltpu.roll(zkr, LANES - MLA_ROPE, 1) * sk_ref[...]
    kr_ref[...] = kr.astype(bf16)
    g = sect(P_G, LANES) + bg_ref[...]
    g_ref[...] = g
    gt_ref[...] = g.T[0:4 * GATE_STRIDE, :]
    u_ref[...] = sect(P_U, ML_INNER).astype(bf16)
    v_ref[...] = sect(P_V, ML_INNER).astype(bf16)
    o_ref[...] = sect(P_O, ML_INNER).astype(bf16)


def _in_projection(ctx, x, modr, w_in_p, g_q, g_kv, ck, sk, bg, n_ctx_tiles):
    b, s, d = x.shape
    tm = ROW_TILE
    t = s + n_ctx_tiles * tm
    nt = t // tm
    n_batch = b

    def mod_idx(bi, i):
        return (jnp.where(i < n_ctx_tiles, n_batch, bi), 0, 0)

    row = lambda w: pl.BlockSpec((None, tm, w), lambda bi, i: (bi, i, 0))
    const = lambda r, c: pl.BlockSpec((r, c), lambda bi, i: (0, 0))
    out_shapes = (
        jax.ShapeDtypeStruct((b, t, Q_LORA), bf16),
        jax.ShapeDtypeStruct((b, t, KV_LORA), bf16),
        jax.ShapeDtypeStruct((b, t, LANES), bf16),
        jax.ShapeDtypeStruct((b, t, LANES), f32),
        jax.ShapeDtypeStruct((b, 4 * GATE_STRIDE, t), f32),
        jax.ShapeDtypeStruct((b, t, ML_INNER), bf16),
        jax.ShapeDtypeStruct((b, t, ML_INNER), bf16),
        jax.ShapeDtypeStruct((b, t, ML_INNER), bf16),
    )
    out_specs = (
        row(Q_LORA), row(KV_LORA), row(LANES), row(LANES),
        pl.BlockSpec((None, 4 * GATE_STRIDE, tm), lambda bi, i: (bi, 0, i)),
        row(ML_INNER), row(ML_INNER), row(ML_INNER),
    )
    return pl.pallas_call(
        functools.partial(_inproj_kernel, n_ctx_tiles=n_ctx_tiles),
        out_shape=out_shapes,
        grid=(b, nt),
        in_specs=[
            pl.BlockSpec((None, tm, d), lambda bi, i: (bi, jnp.minimum(i, n_ctx_tiles - 1), 0)),
            pl.BlockSpec((None, tm, d), lambda bi, i: (bi, jnp.maximum(i - n_ctx_tiles, 0), 0)),
            pl.BlockSpec((None, N_MOD, d), mod_idx),
            const(d, P_COLS),
            const(1, Q_LORA),
            const(1, KV_LORA),
            pl.BlockSpec((tm, LANES), lambda bi, i: (i, 0)),
            pl.BlockSpec((tm, LANES), lambda bi, i: (i, 0)),
            const(1, LANES),
        ],
        out_specs=out_specs,
        scratch_shapes=[pltpu.VMEM((tm, d), bf16)],
        compiler_params=_cparams(("arbitrary", "arbitrary")),
        name="in_projection",
    )(ctx, x, modr, w_in_p, g_q, g_kv, ck, sk, bg)


def _kv_kernel(ckv_ref, kr_ref, wk_ref, wvt_ref, k_ref, vt_ref, ksq_ref):
    ckv = ckv_ref[...]
    kn = _dot(ckv, wk_ref[...]).astype(bf16)
    kr = kr_ref[...]
    knf = kn.astype(f32)
    krf = kr.astype(f32)
    hsel = lax.broadcasted_iota(jnp.int32, (MLA_HEADS, MLA_HEADS * MLA_NOPE), 0)
    csel = lax.broadcasted_iota(jnp.int32, (MLA_HEADS, MLA_HEADS * MLA_NOPE), 1) // MLA_NOPE
    sel = jnp.where(hsel == csel, 1.0, 0.0).astype(bf16)
    ksq_ref[...] = (_dot_nt(sel, (knf * knf).astype(bf16))
                    + _dot_nt(jnp.ones((MLA_HEADS, LANES), bf16), (krf * krf).astype(bf16)))
    lane = lax.broadcasted_iota(jnp.int32, kr.shape, 1)
    stab = jnp.logical_and(lane >= MLA_ROPE, lane < MLA_ROPE + STAB_ROWS)
    kr_aug = jnp.where(stab, 1.0, krf).astype(bf16)
    for h in range(MLA_HEADS):
        k_ref[h, :, 0:MLA_NOPE] = kn[:, h * MLA_NOPE:(h + 1) * MLA_NOPE]
        k_ref[h, :, MLA_NOPE:QK_PAD] = kr_aug
        vt_ref[h] = _dot_nt(wvt_ref[h], ckv).astype(bf16)


def _kv_prep(ckv, kr, wk, wvt):
    b, t, _ = ckv.shape
    tm = ROW_TILE
    return pl.pallas_call(
        _kv_kernel,
        out_shape=(
            jax.ShapeDtypeStruct((b, MLA_HEADS, t, QK_PAD), bf16),
            jax.ShapeDtypeStruct((b, MLA_HEADS, MLA_V, t), bf16),
            jax.ShapeDtypeStruct((b, MLA_HEADS, t), f32),
        ),
        grid=(b, t // tm),
        in_specs=[
            pl.BlockSpec((None, tm, KV_LORA), lambda bi, i: (bi, i, 0)),
            pl.BlockSpec((None, tm, LANES), lambda bi, i: (bi, i, 0)),
            pl.BlockSpec((KV_LORA, MLA_HEADS * MLA_NOPE), lambda bi, i: (0, 0)),
            pl.BlockSpec((MLA_HEADS, MLA_V, KV_LORA), lambda bi, i: (0, 0, 0)),
        ],
        out_specs=(
            pl.BlockSpec((None, MLA_HEADS, tm, QK_PAD), lambda bi, i: (bi, 0, i, 0)),
            pl.BlockSpec((None, MLA_HEADS, MLA_V, tm), lambda bi, i: (bi, 0, 0, i)),
            pl.BlockSpec((None, MLA_HEADS, tm), lambda bi, i: (bi, 0, i)),
        ),
        compiler_params=_cparams(("arbitrary", "arbitrary")),
        name="kv_prep",
    )(ckv, kr, wk, wvt)


def _q_kernel(cq_ref, ksq_ref, wm_ref, wp_ref, ct_ref, st_ref, qt_ref, *, scale):
    cq = cq_ref[...]
    ct = ct_ref[...]
    st = st_ref[...]
    kmax2 = jnp.max(ksq_ref[...], axis=1, keepdims=True)
    ridx = lax.broadcasted_iota(jnp.int32, (LANES, 1), 0)
    stab = jnp.logical_and(ridx >= MLA_ROPE, ridx < MLA_ROPE + STAB_ROWS)
    for h in range(MLA_HEADS):
        qm = _dot_nt(wm_ref[h], cq)
        qp = _dot_nt(wp_ref[h], cq)
        qn = (qm[0:MLA_NOPE] * scale).astype(bf16)
        qr = ((qm[MLA_NOPE:QK_PAD] * ct + qp * st) * scale).astype(bf16)
        qnf = qn.astype(f32)
        qrf = qr.astype(f32)
        qsq = jnp.sum(qnf * qnf, axis=0, keepdims=True) + jnp.sum(qrf * qrf, axis=0, keepdims=True)
        bound = jnp.sqrt(qsq * kmax2[h:h + 1, :]) * STAB_SLACK
        qt_ref[h, 0:MLA_NOPE, :] = qn
        qt_ref[h, MLA_NOPE:QK_PAD, :] = jnp.where(stab, bound * (-1.0 / STAB_ROWS), qrf).astype(bf16)


def _q_prep(cq, ksq, wm, wp, ct, st, n_ctx_tiles):
    b, t, _ = cq.shape
    tm = ROW_TILE
    s = t - n_ctx_tiles * tm
    scale = float((MLA_NOPE + MLA_ROPE) ** -0.5 * 1.4426950408889634)
    return pl.pallas_call(
        functools.partial(_q_kernel, scale=scale),
        out_shape=jax.ShapeDtypeStruct((b, MLA_HEADS, QK_PAD, s), bf16),
        grid=(b, s // tm),
        in_specs=[
            pl.BlockSpec((None, tm, Q_LORA), lambda bi, i: (bi, i + n_ctx_tiles, 0)),
            pl.BlockSpec((None, MLA_HEADS, t), lambda bi, i: (bi, 0, 0)),
            pl.BlockSpec((MLA_HEADS, QK_PAD, Q_LORA), lambda bi, i: (0, 0, 0)),
            pl.BlockSpec((MLA_HEADS, LANES, Q_LORA), lambda bi, i: (0, 0, 0)),
            pl.BlockSpec((LANES, tm), lambda bi, i: (0, i)),
            pl.BlockSpec((LANES, tm), lambda bi, i: (0, i)),
        ],
        out_specs=pl.BlockSpec((None, MLA_HEADS, QK_PAD, tm), lambda bi, i: (bi, 0, 0, i)),
        compiler_params=_cparams(("arbitrary", "arbitrary")),
        name="q_prep",
    )(cq, ksq, wm, wp, ct, st)


def _attn_bound_kernel(qt_ref, k_ref, vt_ref, o_ref, l_ref, *, tk, nk):
    qt = qt_ref[...]
    tq = qt.shape[1]
    l = jnp.zeros((1, tq), f32)
    acc = jnp.zeros((MLA_V, tq), f32)
    s = _dot(k_ref[0:tk, :], qt)
    for j in range(nk):
        s_next = _dot(k_ref[(j + 1) * tk:(j + 2) * tk, :], qt) if j + 1 < nk else None
        p = jnp.exp2(s)
        l = l + jnp.sum(p, axis=0, keepdims=True)
        acc = acc + _dot(vt_ref[:, j * tk:(j + 1) * tk], p.astype(bf16))
        s = s_next
    o_ref[...] = (acc / l).T.astype(o_ref.dtype)
    l_ref[...] = l


def _attn_online_kernel(qt_ref, k_ref, vt_ref, o_ref, *, tk, nk):
    qt = qt_ref[...]
    tq = qt.shape[1]
    m = jnp.full((1, tq), -jnp.inf, f32)
    l = jnp.zeros((1, tq), f32)
    acc = jnp.zeros((MLA_V, tq), f32)
    for j in range(nk):
        s = _dot(k_ref[j * tk:(j + 1) * tk, :], qt)
        m_new = jnp.maximum(m, jnp.max(s, axis=0, keepdims=True))
        alpha = jnp.exp2(m - m_new)
        p = jnp.exp2(s - m_new)
        l = alpha * l + jnp.sum(p, axis=0, keepdims=True)
        acc = alpha * acc + _dot(vt_ref[:, j * tk:(j + 1) * tk], p.astype(bf16))
        m = m_new
    o_ref[...] = (acc / l).T.astype(o_ref.dtype)


def _attention(qt, k, vt):
    b, hh, _, s = qt.shape
    t = k.shape[2]
    tq = min(ATT_TQ, s)
    tk = MXU_DIM
    for cand in range(MXU_DIM, ATT_TK_MAX + 1, MXU_DIM):
        if t % cand == 0:
            tk = cand
    assert s % tq == 0 and t % tk == 0
    in_specs = [
        pl.BlockSpec((None, None, QK_PAD, tq), lambda bi, h, i: (bi, h, 0, i)),
        pl.BlockSpec((None, None, t, QK_PAD), lambda bi, h, i: (bi, h, 0, 0)),
        pl.BlockSpec((None, None, MLA_V, t), lambda bi, h, i: (bi, h, 0, 0)),
    ]
    a_shape = jax.ShapeDtypeStruct((b, s, hh * MLA_V), bf16)
    a_spec = pl.BlockSpec((None, tq, MLA_V), lambda bi, h, i: (bi, i, h))
    sem = ("arbitrary", "arbitrary", "arbitrary")
    a, l = pl.pallas_call(
        functools.partial(_attn_bound_kernel, tk=tk, nk=t // tk),
        out_shape=(a_shape, jax.ShapeDtypeStruct((b, hh, 1, s), f32)),
        grid=(b, hh, s // tq),
        in_specs=in_specs,
        out_specs=(a_spec, pl.BlockSpec((None, None, 1, tq), lambda bi, h, i: (bi, h, 0, i))),
        compiler_params=_cparams(sem),
        name="attention_bound",
    )(qt, k, vt)

    def online(_):
        return pl.pallas_call(
            functools.partial(_attn_online_kernel, tk=tk, nk=t // tk),
            out_shape=a_shape,
            grid=(b, hh, s // tq),
            in_specs=in_specs,
            out_specs=a_spec,
            compiler_params=_cparams(sem),
            name="attention_online",
        )(qt, k, vt)

    ok = jnp.min(l) >= L_MIN
    return lax.cond(ok, lambda _: a, online, None)


def _mproj_kernel(u_ref, up_ref, un_ref, cw_ref, cb_ref, w_ref, qk_ref, *, n_ctx_tiles, n_tiles):
    i = pl.program_id(1)
    u = u_ref[...].astype(f32)
    tm = u.shape[0]
    first = jnp.logical_or(i == 0, i == n_ctx_tiles)
    last = jnp.logical_or(i == n_ctx_tiles - 1, i == n_tiles - 1)
    prev_row = jnp.where(first, 0.0, up_ref[HALO - 1:HALO, :].astype(f32))
    next_row = jnp.where(last, 0.0, un_ref[0:1, :].astype(f32))
    ridx = lax.broadcasted_iota(jnp.int32, (tm, 1), 0)
    u_prev = jnp.where(ridx == 0, prev_row, pltpu.roll(u, 1, 0))
    u_next = jnp.where(ridx == tm - 1, next_row, pltpu.roll(u, tm - 1, 0))
    y = cw_ref[0:1, :] * u_prev + cw_ref[1:2, :] * u + cw_ref[2:3, :] * u_next + cb_ref[...]
    act = _silu(y).astype(bf16)
    lane = lax.broadcasted_iota(jnp.int32, (1, 2 * ML_QK), 1)
    qscale = jnp.where(lane < ML_QK, float(ML_QK ** -0.5), 1.0)
    for h in range(ML_HEADS):
        qk = _dot(act[:, h * ML_V:(h + 1) * ML_V], w_ref[h]) * qscale
        qk_ref[:, h * 2 * ML_QK:(h + 1) * 2 * ML_QK] = qk.astype(bf16)


def _mlstm_projection(u, conv_w, conv_b, w_qk, n_ctx_tiles):
    b, t, _ = u.shape
    tm = ROW_TILE
    nt = t // tm
    hb = tm // HALO
    return pl.pallas_call(
        functools.partial(_mproj_kernel, n_ctx_tiles=n_ctx_tiles, n_tiles=nt),
        out_shape=jax.ShapeDtypeStruct((b, t, ML_HEADS * 2 * ML_QK), bf16),
        grid=(b, nt),
        in_specs=[
            pl.BlockSpec((None, tm, ML_INNER), lambda bi, i: (bi, i, 0)),
            pl.BlockSpec((None, HALO, ML_INNER), lambda bi, i: (bi, jnp.maximum(i * hb - 1, 0), 0)),
            pl.BlockSpec((None, HALO, ML_INNER),
                         lambda bi, i: (bi, jnp.minimum((i + 1) * hb, nt * hb - 1), 0)),
            pl.BlockSpec((3, ML_INNER), lambda bi, i: (0, 0)),
            pl.BlockSpec((1, ML_INNER), lambda bi, i: (0, 0)),
            pl.BlockSpec((ML_HEADS, ML_V, 2 * ML_QK), lambda bi, i: (0, 0, 0)),
        ],
        out_specs=pl.BlockSpec((None, tm, ML_HEADS * 2 * ML_QK), lambda bi, i: (bi, i, 0)),
        compiler_params=_cparams(("arbitrary", "arbitrary")),
        name="mlstm_projection",
    )(u, u, u, conv_w, conv_b, w_qk)


def _scan_kernel(qk_ref, v_ref, g_ref, gt_ref, h_ref, c_scr, n_scr, m_scr):
    d = pl.program_id(0)
    j = pl.program_id(2)
    is_fwd = d == 0
    L = qk_ref.shape[0]
    hi = lax.Precision.HIGHEST

    @pl.when(j == 0)
    def _():
        c_scr[...] = jnp.zeros_like(c_scr)
        n_scr[...] = jnp.zeros_like(n_scr)
        m_scr[...] = jnp.zeros_like(m_scr)

    rr = lax.broadcasted_iota(jnp.int32, (L, L), 0)
    cc = lax.broadcasted_iota(jnp.int32, (L, L), 1)
    order = (rr - cc) * jnp.where(is_fwd, 1, -1)
    causal = order >= 0
    causal_f = jnp.where(causal, 1.0, 0.0)
    anti_f = jnp.where(order <= 0, 1.0, 0.0)

    g = g_ref[...]
    gt = gt_ref[...]
    GS = GATE_STRIDE
    lf_col = jax.nn.log_sigmoid(g)
    lf_row = jax.nn.log_sigmoid(gt)
    bc_col = jnp.dot(causal_f, lf_col, precision=hi, preferred_element_type=f32)
    bc_row = jnp.dot(lf_row, anti_f, precision=hi, preferred_element_type=f32)

    for h in range(ML_HEADS):
        ig_c = jnp.where(is_fwd, g[:, h:h + 1], g[:, 2 * GS + h:2 * GS + h + 1])
        bc_c = jnp.where(is_fwd, bc_col[:, GS + h:GS + h + 1], bc_col[:, 3 * GS + h:3 * GS + h + 1])
        ig_r = jnp.where(is_fwd, gt[h:h + 1, :], gt[2 * GS + h:2 * GS + h + 1, :])
        bc_r = jnp.where(is_fwd, bc_row[GS + h:GS + h + 1, :], bc_row[3 * GS + h:3 * GS + h + 1, :])
        a = jnp.where(is_fwd, bc_r[:, L - 1:L], bc_r[:, 0:1])

        q = qk_ref[:, h * 2 * ML_QK:h * 2 * ML_QK + ML_QK]
        k = qk_ref[:, h * 2 * ML_QK + ML_QK:(h + 1) * 2 * ML_QK]
        v = v_ref[:, h * ML_V:(h + 1) * ML_V]
        c_st = c_scr[h]
        n_st = n_scr[h]
        m_st = m_scr[h]

        dmat = jnp.where(causal, bc_c - bc_r + ig_r, -jnp.inf)
        m_inter = bc_c + m_st
        m_t = jnp.maximum(m_inter, jnp.max(dmat, axis=1, keepdims=True))
        w_inter = jnp.exp(m_inter - m_t)
        p = jnp.exp(dmat - m_t) * _dot_nt(q, k)
        num = w_inter * _dot(q, c_st.astype(bf16)) + _dot(p.astype(bf16), v)
        den = (w_inter * jnp.sum(q.astype(f32) * n_st, axis=1, keepdims=True)
               + jnp.sum(p, axis=1, keepdims=True))
        h_ref[:, h * ML_V:(h + 1) * ML_V] = num / jnp.maximum(jnp.abs(den), jnp.exp(-m_t))

        g_log_r = a - bc_r + ig_r
        m_new = jnp.maximum(a + m_st, jnp.max(g_log_r, axis=1, keepdims=True))
        decay = jnp.exp(a + m_st - m_new)
        wk_c = jnp.exp(a - bc_c + ig_c - m_new)
        kw = k.astype(f32) * wk_c
        c_scr[h] = decay * c_st + _dot(kw.T.astype(bf16), v)
        n_scr[h] = decay * n_st + jnp.sum(kw, axis=0, keepdims=True)
        m_scr[h] = m_new


def _mlstm_scan(qk, v, g, gt):
    b, t, _ = qk.shape
    L = CHUNK
    nc = t // L

    def cidx(d, j):
        return jnp.where(d == 0, j, jnp.where(j == 0, 0, nc - j))

    return pl.pallas_call(
        _scan_kernel,
        out_shape=jax.ShapeDtypeStruct((2, b, t, ML_INNER), f32),
        grid=(2, b, nc),
        in_specs=[
            pl.BlockSpec((None, L, ML_HEADS * 2 * ML_QK), lambda d, bi, j: (bi, cidx(d, j), 0)),
            pl.BlockSpec((None, L, ML_INNER), lambda d, bi, j: (bi, cidx(d, j), 0)),
            pl.BlockSpec((None, L, LANES), lambda d, bi, j: (bi, cidx(d, j), 0)),
            pl.BlockSpec((None, 4 * GATE_STRIDE, L), lambda d, bi, j: (bi, 0, cidx(d, j))),
        ],
        out_specs=pl.BlockSpec((None, None, L, ML_INNER), lambda d, bi, j: (d, bi, cidx(d, j), 0)),
        scratch_shapes=[
            pltpu.VMEM((ML_HEADS, ML_QK, ML_V), f32),
            pltpu.VMEM((ML_HEADS, 1, ML_QK), f32),
            pltpu.VMEM((ML_HEADS, 1, 1), f32),
        ],
        compiler_params=_cparams(("arbitrary", "arbitrary", "arbitrary")),
        name="mlstm_scan",
    )(qk, v, g, gt)


def _outproj_kernel(a_ref, hf_ref, hb_ref, o_ref, x_ref, mod_ref, ghn_ref, w_ref, lg_ref, lb_ref,
                    out_ref):
    hm = hf_ref[...] + hb_ref[...]
    og = jax.nn.sigmoid(o_ref[...].astype(f32))
    ghn = ghn_ref[...]
    y = _dot(a_ref[...], w_ref[0:MLA_HEADS * MLA_V, :])
    for h in range(ML_HEADS):
        sl = slice(h * ML_V, (h + 1) * ML_V)
        seg = hm[:, sl]
        mu = jnp.mean(seg, axis=-1, keepdims=True)
        dd = seg - mu
        var = jnp.mean(dd * dd, axis=-1, keepdims=True)
        mo = og[:, sl] * (dd * lax.rsqrt(var + EPS) * ghn[:, sl])
        r0 = MLA_HEADS * MLA_V + h * ML_V
        y = y + _dot(mo.astype(bf16), w_ref[r0:r0 + ML_V, :])
    gate = mod_ref[2:3, :]
    r = ALPHA * x_ref[...] + gate * y
    out_ref[...] = _layer_norm_rows(r, lg_ref[...], lb_ref[...])


def _out_projection(a, hdir, o, x, modr, g_hn, w_out, ln_g, ln_b, n_ctx_tiles):
    b, s, d = x.shape
    tm = ROW_TILE
    off = n_ctx_tiles
    mix = w_out.shape[0]
    return pl.pallas_call(
        _outproj_kernel,
        out_shape=jax.ShapeDtypeStruct((b, s, d), f32),
        grid=(b, s // tm),
        in_specs=[
            pl.BlockSpec((None, tm, MLA_HEADS * MLA_V), lambda bi, i: (bi, i, 0)),
            pl.BlockSpec((None, None, tm, ML_INNER), lambda bi, i: (0, bi, i + off, 0)),
            pl.BlockSpec((None, None, tm, ML_INNER), lambda bi, i: (1, bi, i + off, 0)),
            pl.BlockSpec((None, tm, ML_INNER), lambda bi, i: (bi, i + off, 0)),
            pl.BlockSpec((None, tm, d), lambda bi, i: (bi, i, 0)),
            pl.BlockSpec((None, N_MOD, d), lambda bi, i: (bi, 0, 0)),
            pl.BlockSpec((1, ML_INNER), lambda bi, i: (0, 0)),
            pl.BlockSpec((mix, d), lambda bi, i: (0, 0)),
            pl.BlockSpec((1, d), lambda bi, i: (0, 0)),
            pl.BlockSpec((1, d), lambda bi, i: (0, 0)),
        ],
        out_specs=pl.BlockSpec((None, tm, d), lambda bi, i: (bi, i, 0)),
        compiler_params=_cparams(("arbitrary", "arbitrary")),
        name="out_projection",
    )(a, hdir, hdir, o, x, modr, g_hn, w_out, ln_g, ln_b)


def _ffn_kernel(x_ref, xp_ref, xn_ref, mod_ref, wg_ref, wu_ref, cw_ref, cb_ref, wd_ref, lg_ref, lb_ref,
                out_ref, hs_ref, acc_ref, gs_ref, *, n_row_tiles):
    i = pl.program_id(1)
    j = pl.program_id(2)
    nf = pl.num_programs(2)
    tm = x_ref.shape[0]
    shift = mod_ref[3:4, :]
    scale = mod_ref[4:5, :]

    @pl.when(j == 0)
    def _():
        hs_ref[0:HALO, :] = (xp_ref[...] * (1.0 + scale) + shift).astype(bf16)
        hs_ref[HALO:HALO + tm, :] = (x_ref[...] * (1.0 + scale) + shift).astype(bf16)
        hs_ref[HALO + tm:2 * HALO + tm, :] = (xn_ref[...] * (1.0 + scale) + shift).astype(bf16)
        acc_ref[...] = jnp.zeros_like(acc_ref)

    gs_ref[...] = _dot(hs_ref[...], wg_ref[...])
    ridx = lax.broadcasted_iota(jnp.int32, (tm, 1), 0)
    g_prev = gs_ref[HALO - 1:HALO - 1 + tm, :]
    g_cur = gs_ref[HALO:HALO + tm, :]
    g_next = gs_ref[HALO + 1:HALO + 1 + tm, :]
    g_prev = jnp.where(jnp.logical_and(ridx == 0, i == 0), 0.0, g_prev)
    g_next = jnp.where(jnp.logical_and(ridx == tm - 1, i == n_row_tiles - 1), 0.0, g_next)
    gate = cw_ref[0:1, :] * g_prev + cw_ref[1:2, :] * g_cur + cw_ref[2:3, :] * g_next + cb_ref[...]
    up = _dot(hs_ref[HALO:HALO + tm, :], wu_ref[...])
    act = (_silu(gate) * up).astype(bf16)
    acc_ref[...] += _dot(act, wd_ref[...])

    @pl.when(j == nf - 1)
    def _():
        r = ALPHA * x_ref[...] + mod_ref[5:6, :] * acc_ref[...]
        out_ref[...] = _layer_norm_rows(r, lg_ref[...], lb_ref[...])


def _conv_ffn(x, modr, w_up, conv_w, conv_b, w_down, ln_g, ln_b):
    b, s, d = x.shape
    dff = w_down.shape[0]
    tm = min(FFN_TM, s)
    tf = min(FFN_TF, dff)
    assert s % tm == 0 and dff % tf == 0
    nr = s // tm
    nf = dff // tf
    hb = tm // HALO
    return pl.pallas_call(
        functools.partial(_ffn_kernel, n_row_tiles=nr),
        out_shape=jax.ShapeDtypeStruct((b, s, d), f32),
        grid=(b, nr, nf),
        in_specs=[
            pl.BlockSpec((None, tm, d), lambda bi, i, j: (bi, i, 0)),
            pl.BlockSpec((None, HALO, d), lambda bi, i, j: (bi, jnp.maximum(i * hb - 1, 0), 0)),
            pl.BlockSpec((None, HALO, d), lambda bi, i, j: (bi, jnp.minimum((i + 1) * hb, nr * hb - 1), 0)),
            pl.BlockSpec((None, N_MOD, d), lambda bi, i, j: (bi, 0, 0)),
            pl.BlockSpec((d, tf), lambda bi, i, j: (0, j)),
            pl.BlockSpec((d, tf), lambda bi, i, j: (0, nf + j)),
            pl.BlockSpec((3, tf), lambda bi, i, j: (0, j)),
            pl.BlockSpec((1, tf), lambda bi, i, j: (0, j)),
            pl.BlockSpec((tf, d), lambda bi, i, j: (j, 0)),
            pl.BlockSpec((1, d), lambda bi, i, j: (0, 0)),
            pl.BlockSpec((1, d), lambda bi, i, j: (0, 0)),
        ],
        out_specs=pl.BlockSpec((None, tm, d), lambda bi, i, j: (bi, i, 0)),
        scratch_shapes=[
            pltpu.VMEM((tm + 2 * HALO, d), bf16),
            pltpu.VMEM((tm, d), f32),
            pltpu.VMEM((tm + 2 * HALO, tf), f32),
        ],
        compiler_params=_cparams(("arbitrary", "arbitrary", "arbitrary")),
        name="conv_ffn",
    )(x, x, x, modr, w_up, w_up, conv_w, conv_b, w_down, ln_g, ln_b)


def _rope_perm():
    q = MLA_ROPE // 4
    return jnp.concatenate([jnp.arange(q, 2 * q), jnp.arange(0, q),
                            jnp.arange(3 * q, 4 * q), jnp.arange(2 * q, 3 * q)])


def _rope_tables(s):
    n_rows = s // GRID_W
    row = jnp.repeat(jnp.arange(n_rows), GRID_W).astype(f32)
    col = jnp.tile(jnp.arange(GRID_W), n_rows).astype(f32)
    n_freq = MLA_ROPE // 4
    inv = ROPE_THETA ** (-jnp.arange(n_freq, dtype=f32) / n_freq)
    ar = row[:, None] * inv
    ac = col[:, None] * inv
    cos = jnp.concatenate([jnp.cos(ar), jnp.cos(ar), jnp.cos(ac), jnp.cos(ac)], axis=-1)
    sin = jnp.concatenate([-jnp.sin(ar), jnp.sin(ar), -jnp.sin(ac), jnp.sin(ac)], axis=-1)
    return cos, sin


def _pack_w_in(w_in):
    d = w_in.shape[0]
    z = lambda n: jnp.zeros((d, n), w_in.dtype)
    gates = []
    for gi in range(4):
        gates += [w_in[:, OFF_G + gi * ML_HEADS:OFF_G + (gi + 1) * ML_HEADS], z(GATE_STRIDE - ML_HEADS)]
    cols = [w_in[:, 0:OFF_U], w_in[:, OFF_KR:OFF_U][:, _rope_perm()], w_in[:, OFF_U:OFF_G],
            *gates, z(LANES - 4 * GATE_STRIDE)]
    return jnp.concatenate(cols, axis=1).astype(bf16)


def _pack_b_gate(b_gate):
    parts = []
    for gi in range(4):
        parts += [b_gate[gi * ML_HEADS:(gi + 1) * ML_HEADS], jnp.zeros((GATE_STRIDE - ML_HEADS,), f32)]
    parts.append(jnp.zeros((LANES - 4 * GATE_STRIDE,), f32))
    return jnp.concatenate(parts)[None, :]


def kernel(x, c, ctx, c_ctx, w_ada, b_ada, w_in, g_q, w_uq, g_kv, w_ukv, ml_conv_w, ml_conv_b, w_mq, w_mk, b_gate, g_hn, w_out, ln1_g, ln1_b, w_up, ffn_conv_w, ffn_conv_b, w_down, ln2_g, ln2_b):
    b, s, d = x.shape
    n_ctx = ctx.shape[1]
    assert n_ctx % ROW_TILE == 0 and s % ROW_TILE == 0 and n_ctx % CHUNK == 0 and s % CHUNK == 0
    n_ctx_tiles = n_ctx // ROW_TILE
    layer = 0

    rows = -(-(b + 1) // SUBLANES) * SUBLANES
    cc = jnp.concatenate([c, c_ctx[None, :], jnp.zeros((rows - b - 1, d), f32)], axis=0)
    modr = _modulation(cc, w_ada[layer], b_ada[layer][None, :]).reshape(rows, N_MOD, d)

    cos, sin = _rope_tables(s)
    pad = jnp.zeros((s, LANES - MLA_ROPE), f32)
    cos_p = jnp.concatenate([cos, pad], axis=1)
    sin_p = jnp.concatenate([sin, pad], axis=1)
    ctx_c = jnp.concatenate([jnp.ones((n_ctx, MLA_ROPE), f32), jnp.zeros((n_ctx, LANES - MLA_ROPE), f32)], axis=1)
    ck = jnp.concatenate([ctx_c, cos_p], axis=0)
    sk = jnp.concatenate([jnp.zeros((n_ctx, LANES), f32), sin_p], axis=0)
    ct = cos_p.T
    st = sin_p.T

    cq, ckv, kr, g, gt, u, v, o = _in_projection(
        ctx, x, modr, _pack_w_in(w_in[layer]), g_q[layer][None, :], g_kv[layer][None, :], ck, sk,
        _pack_b_gate(b_gate[layer]), n_ctx_tiles)

    dq = MLA_NOPE + MLA_ROPE
    wq = w_uq[layer].reshape(Q_LORA, MLA_HEADS, dq)
    wq_nope = jnp.transpose(wq[:, :, :MLA_NOPE], (1, 2, 0))
    wq_rope = jnp.transpose(wq[:, :, MLA_NOPE:], (1, 2, 0))
    zq = jnp.zeros((MLA_HEADS, QK_PAD - dq, Q_LORA), f32)
    wm = jnp.concatenate([wq_nope, wq_rope, zq], axis=1).astype(bf16)
    wp = jnp.concatenate([wq_rope[:, _rope_perm(), :],
                          jnp.zeros((MLA_HEADS, LANES - MLA_ROPE, Q_LORA), f32)], axis=1).astype(bf16)
    wkv = w_ukv[layer].reshape(KV_LORA, MLA_HEADS, MLA_NOPE + MLA_V)
    wk = wkv[:, :, :MLA_NOPE].reshape(KV_LORA, MLA_HEADS * MLA_NOPE).astype(bf16)
    wvt = jnp.transpose(wkv[:, :, MLA_NOPE:], (1, 2, 0)).astype(bf16)
    kk, vt, ksq = _kv_prep(ckv, kr, wk, wvt)
    qt = _q_prep(cq, ksq, wm, wp, ct, st, n_ctx_tiles)
    a = _attention(qt, kk, vt)

    w_qk = jnp.concatenate([w_mq[layer], w_mk[layer]], axis=-1).astype(bf16)
    qk = _mlstm_projection(u, ml_conv_w[layer], ml_conv_b[layer][None, :], w_qk, n_ctx_tiles)
    hdir = _mlstm_scan(qk, v, g, gt)

    x1 = _out_projection(a, hdir, o, x, modr, g_hn[layer].reshape(1, ML_INNER), w_out[layer].astype(bf16),
                         ln1_g[layer][None, :], ln1_b[layer][None, :], n_ctx_tiles)
    return _conv_ffn(x1, modr, w_up[layer].astype(bf16), ffn_conv_w[layer], ffn_conv_b[layer][None, :],
                     w_down[layer].astype(bf16), ln2_g[layer][None, :], ln2_b[layer][None, :])
```

```python
import functools

import jax
import jax.numpy as jnp
from jax import lax
from jax.experimental import pallas as pl
from jax.experimental.pallas import tpu as pltpu

f32 = jnp.float32
bf16 = jnp.bfloat16

GRID_W = 64
MLA_HEADS = 8
MLA_NOPE = 128
MLA_ROPE = 64
MLA_V = 128
Q_LORA = 512
KV_LORA = 256
ROPE_THETA = 10000.0
ML_HEADS = 4
ML_QK = 128
ML_V = 256
ML_INNER = ML_HEADS * ML_V
N_GATES = 4 * ML_HEADS
DEPTH = 1
OFF_CKV = Q_LORA
OFF_KR = Q_LORA + KV_LORA
OFF_U = OFF_KR + MLA_ROPE
OFF_V = OFF_U + ML_INNER
OFF_O = OFF_V + ML_INNER
OFF_G = OFF_O + ML_INNER
ALPHA = (2.0 * DEPTH) ** 0.25
EPS = 1e-6
N_MOD = 6

LANES = 128
SUBLANES = 8
MXU_DIM = 256
VMEM_LIMIT = 56 * 1024 * 1024

QK_PAD = 2 * LANES
GATE_STRIDE = SUBLANES
P_Q = 0
P_CKV = P_Q + Q_LORA
P_KR = P_CKV + KV_LORA
P_U = P_KR + LANES
P_V = P_U + ML_INNER
P_O = P_V + ML_INNER
P_G = P_O + ML_INNER
P_COLS = P_G + LANES
STAB_ROWS = 16
STAB_SLACK = 1.02
L_MIN = 2.0 ** -100

ROW_TILE = 256
CHUNK = 256
ATT_TQ = 1024
ATT_TK_MAX = 768
FFN_TM = 512
FFN_TF = 512
HALO = SUBLANES


def _cparams(sem):
    return pltpu.CompilerParams(dimension_semantics=sem, vmem_limit_bytes=VMEM_LIMIT)


def _silu(x):
    return x * jax.nn.sigmoid(x)


def _dot(a, b):
    return jnp.dot(a, b, preferred_element_type=f32)


def _dot_nt(a, b):
    return lax.dot_general(a, b, (((1,), (1,)), ((), ())), preferred_element_type=f32)


def _layer_norm_rows(r, g, b):
    mu = jnp.mean(r, axis=-1, keepdims=True)
    d = r - mu
    var = jnp.mean(d * d, axis=-1, keepdims=True)
    return d * lax.rsqrt(var + EPS) * g + b


def _mod_kernel(c_ref, w_ref, b_ref, o_ref):
    s = _silu(c_ref[...])
    o_ref[...] = _dot(s.astype(bf16), w_ref[...].astype(bf16)) + b_ref[...]


def _modulation(cc, w_ada, b_ada):
    rows, d = cc.shape
    n = w_ada.shape[1]
    tn = min(n, 1536)
    assert n % tn == 0
    return pl.pallas_call(
        _mod_kernel,
        out_shape=jax.ShapeDtypeStruct((rows, n), f32),
        grid=(n // tn,),
        in_specs=[
            pl.BlockSpec((rows, d), lambda j: (0, 0)),
            pl.BlockSpec((d, tn), lambda j: (0, j)),
            pl.BlockSpec((1, tn), lambda j: (0, j)),
        ],
        out_specs=pl.BlockSpec((rows, tn), lambda j: (0, j)),
        compiler_params=_cparams(("arbitrary",)),
        name="adaln_mod",
    )(cc, w_ada, b_ada)


def _inproj_kernel(ctx_ref, x_ref, mod_ref, w_ref, gq_ref, gkv_ref, ck_ref, sk_ref, bg_ref,
                   cq_ref, ckv_ref, kr_ref, g_ref, gt_ref, u_ref, v_ref, o_ref, h_ref, *, n_ctx_tiles):
    shift = mod_ref[0:1, :]
    scale = mod_ref[1:2, :]
    is_ctx = pl.program_id(1) < n_ctx_tiles

    @pl.when(is_ctx)
    def _():
        h_ref[...] = (ctx_ref[...] * (1.0 + scale) + shift).astype(bf16)

    @pl.when(jnp.logical_not(is_ctx))
    def _():
        h_ref[...] = (x_ref[...] * (1.0 + scale) + shift).astype(bf16)

    def sect(a, n):
        return _dot(h_ref[...], w_ref[:, a:a + n])

    zq = sect(P_Q, Q_LORA)
    cq = zq * lax.rsqrt(jnp.mean(zq * zq, axis=-1, keepdims=True) + EPS) * gq_ref[...]
    cq_ref[...] = cq.astype(bf16)
    zkv = sect(P_CKV, KV_LORA)
    ckv = zkv * lax.rsqrt(jnp.mean(zkv * zkv, axis=-1, keepdims=True) + EPS) * gkv_ref[...]
    ckv_ref[...] = ckv.astype(bf16)
    zkr = sect(P_KR, LANES)
    kr = zkr * ck_ref[...] + pltpu.roll(zkr, LANES - MLA_ROPE, 1) * sk_ref[...]
    kr_ref[...] = kr.astype(bf16)
    g = sect(P_G, LANES) + bg_ref[...]
    tm = g.shape[0]
    lf = jax.nn.log_sigmoid(g)
    tri = jnp.where(lax.broadcasted_iota(jnp.int32, (tm, tm), 0)
                    >= lax.broadcasted_iota(jnp.int32, (tm, tm), 1), 1.0, 0.0).astype(bf16)
    hi = lf.astype(bf16)
    r1 = lf - hi.astype(f32)
    mid = r1.astype(bf16)
    lo = (r1 - mid.astype(f32)).astype(bf16)
    pre = _dot(tri, hi) + _dot(tri, mid) + _dot(tri, lo)
    suf = pre[tm - 1:tm, :] - pre + lf
    grp = lax.broadcasted_iota(jnp.int32, (1, LANES), 1) // GATE_STRIDE
    g2 = jnp.where(grp == 1, pre, jnp.where(grp == 3, suf, g))
    g_ref[...] = g2
    gt_ref[...] = g2.T[0:4 * GATE_STRIDE, :]
    u_ref[...] = sect(P_U, ML_INNER).astype(bf16)
    v_ref[...] = sect(P_V, ML_INNER).astype(bf16)
    o_ref[...] = sect(P_O, ML_INNER).astype(bf16)


def _in_projection(ctx, x, modr, w_in_p, g_q, g_kv, ck, sk, bg, n_ctx_tiles):
    b, s, d = x.shape
    tm = ROW_TILE
    t = s + n_ctx_tiles * tm
    nt = t // tm
    n_batch = b

    def mod_idx(bi, i):
        return (jnp.where(i < n_ctx_tiles, n_batch, bi), 0, 0)

    row = lambda w: pl.BlockSpec((None, tm, w), lambda bi, i: (bi, i, 0))
    const = lambda r, c: pl.BlockSpec((r, c), lambda bi, i: (0, 0))
    out_shapes = (
        jax.ShapeDtypeStruct((b, t, Q_LORA), bf16),
        jax.ShapeDtypeStruct((b, t, KV_LORA), bf16),
        jax.ShapeDtypeStruct((b, t, LANES), bf16),
        jax.ShapeDtypeStruct((b, t, LANES), f32),
        jax.ShapeDtypeStruct((b, 4 * GATE_STRIDE, t), f32),
        jax.ShapeDtypeStruct((b, t, ML_INNER), bf16),
        jax.ShapeDtypeStruct((b, t, ML_INNER), bf16),
        jax.ShapeDtypeStruct((b, t, ML_INNER), bf16),
    )
    out_specs = (
        row(Q_LORA), row(KV_LORA), row(LANES), row(LANES),
        pl.BlockSpec((None, 4 * GATE_STRIDE, tm), lambda bi, i: (bi, 0, i)),
        row(ML_INNER), row(ML_INNER), row(ML_INNER),
    )
    return pl.pallas_call(
        functools.partial(_inproj_kernel, n_ctx_tiles=n_ctx_tiles),
        out_shape=out_shapes,
        grid=(b, nt),
        in_specs=[
            pl.BlockSpec((None, tm, d), lambda bi, i: (bi, jnp.minimum(i, n_ctx_tiles - 1), 0)),
            pl.BlockSpec((None, tm, d), lambda bi, i: (bi, jnp.maximum(i - n_ctx_tiles, 0), 0)),
            pl.BlockSpec((None, N_MOD, d), mod_idx),
            const(d, P_COLS),
            const(1, Q_LORA),
            const(1, KV_LORA),
            pl.BlockSpec((tm, LANES), lambda bi, i: (i, 0)),
            pl.BlockSpec((tm, LANES), lambda bi, i: (i, 0)),
            const(1, LANES),
        ],
        out_specs=out_specs,
        scratch_shapes=[pltpu.VMEM((tm, d), bf16)],
        compiler_params=_cparams(("arbitrary", "arbitrary")),
        name="in_projection",
    )(ctx, x, modr, w_in_p, g_q, g_kv, ck, sk, bg)


def _kv_kernel(ckv_ref, kr_ref, wk_ref, wvt_ref, k_ref, vt_ref, ksq_ref):
    ckv = ckv_ref[...]
    kn = _dot(ckv, wk_ref[...]).astype(bf16)
    kr = kr_ref[...]
    knf = kn.astype(f32)
    krf = kr.astype(f32)
    hsel = lax.broadcasted_iota(jnp.int32, (MLA_HEADS, MLA_HEADS * MLA_NOPE), 0)
    csel = lax.broadcasted_iota(jnp.int32, (MLA_HEADS, MLA_HEADS * MLA_NOPE), 1) // MLA_NOPE
    sel = jnp.where(hsel == csel, 1.0, 0.0).astype(bf16)
    ksq_ref[...] = (_dot_nt(sel, (knf * knf).astype(bf16))
                    + _dot_nt(jnp.ones((MLA_HEADS, LANES), bf16), (krf * krf).astype(bf16)))
    lane = lax.broadcasted_iota(jnp.int32, kr.shape, 1)
    stab = jnp.logical_and(lane >= MLA_ROPE, lane < MLA_ROPE + STAB_ROWS)
    kr_aug = jnp.where(stab, 1.0, krf).astype(bf16)
    for h in range(MLA_HEADS):
        k_ref[h, :, 0:MLA_NOPE] = kn[:, h * MLA_NOPE:(h + 1) * MLA_NOPE]
        k_ref[h, :, MLA_NOPE:QK_PAD] = kr_aug
        vt_ref[h] = _dot_nt(wvt_ref[h], ckv).astype(bf16)


def _kv_prep(ckv, kr, wk, wvt):
    b, t, _ = ckv.shape
    tm = ROW_TILE
    return pl.pallas_call(
        _kv_kernel,
        out_shape=(
            jax.ShapeDtypeStruct((b, MLA_HEADS, t, QK_PAD), bf16),
            jax.ShapeDtypeStruct((b, MLA_HEADS, MLA_V, t), bf16),
            jax.ShapeDtypeStruct((b, MLA_HEADS, t), f32),
        ),
        grid=(b, t // tm),
        in_specs=[
            pl.BlockSpec((None, tm, KV_LORA), lambda bi, i: (bi, i, 0)),
            pl.BlockSpec((None, tm, LANES), lambda bi, i: (bi, i, 0)),
            pl.BlockSpec((KV_LORA, MLA_HEADS * MLA_NOPE), lambda bi, i: (0, 0)),
            pl.BlockSpec((MLA_HEADS, MLA_V, KV_LORA), lambda bi, i: (0, 0, 0)),
        ],
        out_specs=(
            pl.BlockSpec((None, MLA_HEADS, tm, QK_PAD), lambda bi, i: (bi, 0, i, 0)),
            pl.BlockSpec((None, MLA_HEADS, MLA_V, tm), lambda bi, i: (bi, 0, 0, i)),
            pl.BlockSpec((None, MLA_HEADS, tm), lambda bi, i: (bi, 0, i)),
        ),
        compiler_params=_cparams(("arbitrary", "arbitrary")),
        name="kv_prep",
    )(ckv, kr, wk, wvt)


def _q_kernel(cq_ref, ksq_ref, wm_ref, wp_ref, ct_ref, st_ref, qt_ref, *, scale):
    cq = cq_ref[...]
    ct = ct_ref[...]
    st = st_ref[...]
    kmax2 = jnp.max(ksq_ref[...], axis=1, keepdims=True)
    ridx = lax.broadcasted_iota(jnp.int32, (LANES, 1), 0)
    stab = jnp.logical_and(ridx >= MLA_ROPE, ridx < MLA_ROPE + STAB_ROWS)
    for h in range(MLA_HEADS):
        qm = _dot_nt(wm_ref[h], cq)
        qp = _dot_nt(wp_ref[h], cq)
        qn = (qm[0:MLA_NOPE] * scale).astype(bf16)
        qr = ((qm[MLA_NOPE:QK_PAD] * ct + qp * st) * scale).astype(bf16)
        qnf = qn.astype(f32)
        qrf = qr.astype(f32)
        qsq = jnp.sum(qnf * qnf, axis=0, keepdims=True) + jnp.sum(qrf * qrf, axis=0, keepdims=True)
        bound = jnp.sqrt(qsq * kmax2[h:h + 1, :]) * STAB_SLACK
        qt_ref[h, 0:MLA_NOPE, :] = qn
        qt_ref[h, MLA_NOPE:QK_PAD, :] = jnp.where(stab, bound * (-1.0 / STAB_ROWS), qrf).astype(bf16)


def _q_prep(cq, ksq, wm, wp, ct, st, n_ctx_tiles):
    b, t, _ = cq.shape
    tm = ROW_TILE
    s = t - n_ctx_tiles * tm
    scale = float((MLA_NOPE + MLA_ROPE) ** -0.5 * 1.4426950408889634)
    return pl.pallas_call(
        functools.partial(_q_kernel, scale=scale),
        out_shape=jax.ShapeDtypeStruct((b, MLA_HEADS, QK_PAD, s), bf16),
        grid=(b, s // tm),
        in_specs=[
            pl.BlockSpec((None, tm, Q_LORA), lambda bi, i: (bi, i + n_ctx_tiles, 0)),
            pl.BlockSpec((None, MLA_HEADS, t), lambda bi, i: (bi, 0, 0)),
            pl.BlockSpec((MLA_HEADS, QK_PAD, Q_LORA), lambda bi, i: (0, 0, 0)),
            pl.BlockSpec((MLA_HEADS, LANES, Q_LORA), lambda bi, i: (0, 0, 0)),
            pl.BlockSpec((LANES, tm), lambda bi, i: (0, i)),
            pl.BlockSpec((LANES, tm), lambda bi, i: (0, i)),
        ],
        out_specs=pl.BlockSpec((None, MLA_HEADS, QK_PAD, tm), lambda bi, i: (bi, 0, 0, i)),
        compiler_params=_cparams(("arbitrary", "arbitrary")),
        name="q_prep",
    )(cq, ksq, wm, wp, ct, st)


def _attn_bound_kernel(qt_ref, k_ref, vt_ref, o_ref, l_ref, *, tk, nk):
    qt = qt_ref[...]
    tq = qt.shape[1]
    l = jnp.zeros((1, tq), f32)
    acc = jnp.zeros((MLA_V, tq), f32)
    s = _dot(k_ref[0:tk, :], qt)
    for j in range(nk):
        s_next = _dot(k_ref[(j + 1) * tk:(j + 2) * tk, :], qt) if j + 1 < nk else None
        p = jnp.exp2(s)
        l = l + jnp.sum(p, axis=0, keepdims=True)
        acc = acc + _dot(vt_ref[:, j * tk:(j + 1) * tk], p.astype(bf16))
        s = s_next
    o_ref[...] = (acc / l).T.astype(o_ref.dtype)
    l_ref[...] = l


def _attn_online_kernel(qt_ref, k_ref, vt_ref, o_ref, *, tk, nk):
    qt = qt_ref[...]
    tq = qt.shape[1]
    m = jnp.full((1, tq), -jnp.inf, f32)
    l = jnp.zeros((1, tq), f32)
    acc = jnp.zeros((MLA_V, tq), f32)
    for j in range(nk):
        s = _dot(k_ref[j * tk:(j + 1) * tk, :], qt)
        m_new = jnp.maximum(m, jnp.max(s, axis=0, keepdims=True))
        alpha = jnp.exp2(m - m_new)
        p = jnp.exp2(s - m_new)
        l = alpha * l + jnp.sum(p, axis=0, keepdims=True)
        acc = alpha * acc + _dot(vt_ref[:, j * tk:(j + 1) * tk], p.astype(bf16))
        m = m_new
    o_ref[...] = (acc / l).T.astype(o_ref.dtype)


def _attention(qt, k, vt):
    b, hh, _, s = qt.shape
    t = k.shape[2]
    tq = min(ATT_TQ, s)
    tk = MXU_DIM
    for cand in range(MXU_DIM, ATT_TK_MAX + 1, MXU_DIM):
        if t % cand == 0:
            tk = cand
    assert s % tq == 0 and t % tk == 0
    in_specs = [
        pl.BlockSpec((None, None, QK_PAD, tq), lambda bi, h, i: (bi, h, 0, i)),
        pl.BlockSpec((None, None, t, QK_PAD), lambda bi, h, i: (bi, h, 0, 0)),
        pl.BlockSpec((None, None, MLA_V, t), lambda bi, h, i: (bi, h, 0, 0)),
    ]
    a_shape = jax.ShapeDtypeStruct((b, s, hh * MLA_V), bf16)
    a_spec = pl.BlockSpec((None, tq, MLA_V), lambda bi, h, i: (bi, i, h))
    sem = ("arbitrary", "arbitrary", "arbitrary")
    a, l = pl.pallas_call(
        functools.partial(_attn_bound_kernel, tk=tk, nk=t // tk),
        out_shape=(a_shape, jax.ShapeDtypeStruct((b, hh, 1, s), f32)),
        grid=(b, hh, s // tq),
        in_specs=in_specs,
        out_specs=(a_spec, pl.BlockSpec((None, None, 1, tq), lambda bi, h, i: (bi, h, 0, i))),
        compiler_params=_cparams(sem),
        name="attention_bound",
    )(qt, k, vt)

    def online(_):
        return pl.pallas_call(
            functools.partial(_attn_online_kernel, tk=tk, nk=t // tk),
            out_shape=a_shape,
            grid=(b, hh, s // tq),
            in_specs=in_specs,
            out_specs=a_spec,
            compiler_params=_cparams(sem),
            name="attention_online",
        )(qt, k, vt)

    ok = jnp.min(l) >= L_MIN
    return lax.cond(ok, lambda _: a, online, None)


def _mproj_kernel(u_ref, up_ref, un_ref, cw_ref, cb_ref, wq_ref, wkt_ref, q_ref, kt_ref, *,
                  n_ctx_tiles, n_tiles):
    i = pl.program_id(1)
    u = u_ref[...].astype(f32)
    tm = u.shape[0]
    first = jnp.logical_or(i == 0, i == n_ctx_tiles)
    last = jnp.logical_or(i == n_ctx_tiles - 1, i == n_tiles - 1)
    prev_row = jnp.where(first, 0.0, up_ref[HALO - 1:HALO, :].astype(f32))
    next_row = jnp.where(last, 0.0, un_ref[0:1, :].astype(f32))
    ridx = lax.broadcasted_iota(jnp.int32, (tm, 1), 0)
    u_prev = jnp.where(ridx == 0, prev_row, pltpu.roll(u, 1, 0))
    u_next = jnp.where(ridx == tm - 1, next_row, pltpu.roll(u, tm - 1, 0))
    y = cw_ref[0:1, :] * u_prev + cw_ref[1:2, :] * u + cw_ref[2:3, :] * u_next + cb_ref[...]
    act = _silu(y).astype(bf16)
    for h in range(ML_HEADS):
        a_h = act[:, h * ML_V:(h + 1) * ML_V]
        q_ref[:, h * ML_QK:(h + 1) * ML_QK] = (_dot(a_h, wq_ref[h]) * float(ML_QK ** -0.5)).astype(bf16)
        kt_ref[h * ML_QK:(h + 1) * ML_QK, :] = _dot_nt(wkt_ref[h], a_h).astype(bf16)


def _mlstm_projection(u, conv_w, conv_b, w_q, w_kt, n_ctx_tiles):
    b, t, _ = u.shape
    tm = ROW_TILE
    nt = t // tm
    hb = tm // HALO
    return pl.pallas_call(
        functools.partial(_mproj_kernel, n_ctx_tiles=n_ctx_tiles, n_tiles=nt),
        out_shape=(jax.ShapeDtypeStruct((b, t, ML_HEADS * ML_QK), bf16),
                   jax.ShapeDtypeStruct((b, ML_HEADS * ML_QK, t), bf16)),
        grid=(b, nt),
        in_specs=[
            pl.BlockSpec((None, tm, ML_INNER), lambda bi, i: (bi, i, 0)),
            pl.BlockSpec((None, HALO, ML_INNER), lambda bi, i: (bi, jnp.maximum(i * hb - 1, 0), 0)),
            pl.BlockSpec((None, HALO, ML_INNER),
                         lambda bi, i: (bi, jnp.minimum((i + 1) * hb, nt * hb - 1), 0)),
            pl.BlockSpec((3, ML_INNER), lambda bi, i: (0, 0)),
            pl.BlockSpec((1, ML_INNER), lambda bi, i: (0, 0)),
            pl.BlockSpec((ML_HEADS, ML_V, ML_QK), lambda bi, i: (0, 0, 0)),
            pl.BlockSpec((ML_HEADS, ML_QK, ML_V), lambda bi, i: (0, 0, 0)),
        ],
        out_specs=(pl.BlockSpec((None, tm, ML_HEADS * ML_QK), lambda bi, i: (bi, i, 0)),
                   pl.BlockSpec((None, ML_HEADS * ML_QK, tm), lambda bi, i: (bi, 0, i))),
        compiler_params=_cparams(("arbitrary", "arbitrary")),
        name="mlstm_projection",
    )(u, u, u, conv_w, conv_b, w_q, w_kt)


def _scan_kernel(qf_ref, ktf_ref, vf_ref, gf_ref, gtf_ref, qb_ref, ktb_ref, vb_ref, gb_ref, gtb_ref,
                 hf_ref, hb_ref, c_scr, m_scr):
    j = pl.program_id(1)
    L = qf_ref.shape[0]
    GS = GATE_STRIDE

    rr = lax.broadcasted_iota(jnp.int32, (L, L), 0)
    cc = lax.broadcasted_iota(jnp.int32, (L, L), 1)
    visible = (rr >= cc, rr <= cc)
    dirs = ((qf_ref, ktf_ref, vf_ref, gf_ref, gtf_ref, hf_ref), (qb_ref, ktb_ref, vb_ref, gb_ref, gtb_ref, hb_ref))
    ones_col = jnp.where(lax.broadcasted_iota(jnp.int32, (L, LANES), 1) == 0, 1.0, 0.0).astype(bf16)

    chains = []
    for d, (q_ref, kt_ref, v_ref, g_ref, gt_ref, h_ref) in enumerate(dirs):
        g = g_ref[...]
        gt = gt_ref[...]
        for h in range(ML_HEADS):
            ci = 2 * d * GS + h
            cf = ci + GS
            bc_r = gt[cf:cf + 1, :]
            chains.append(dict(
                d=d, h=h, st=d * ML_HEADS + h, h_ref=h_ref,
                q=q_ref[:, h * ML_QK:(h + 1) * ML_QK],
                kt=kt_ref[h * ML_QK:(h + 1) * ML_QK, :],
                v=jnp.concatenate([v_ref[:, h * ML_V:(h + 1) * ML_V], ones_col], axis=1),
                bc_c=g[:, cf:cf + 1],
                ig_r=gt[ci:ci + 1, :], bc_r=bc_r,
                a=bc_r[:, L - 1:L] if d == 0 else bc_r[:, 0:1],
            ))

    def step(first):
        for c in chains:
            c["s"] = _dot(c["q"], c["kt"])
        for c in chains:
            if not first:
                c["c_st"] = c_scr[c["st"]]
                c["m_st"] = m_scr[c["st"]]
                c["qc"] = _dot(c["q"], c["c_st"].astype(bf16))
            else:
                c["m_st"] = 0.0

        for c in chains:
            drow = jnp.where(visible[c["d"]], c["ig_r"] - c["bc_r"], -jnp.inf)
            mm = jnp.maximum(c["m_st"], jnp.max(drow, axis=1, keepdims=True))
            c["w_inter"] = jnp.exp(c["m_st"] - mm)
            c["floor"] = jnp.exp(-(c["bc_c"] + mm))
            c["p"] = jnp.exp(drow - mm) * c["s"]

        for c in chains:
            nd = _dot(c["p"].astype(bf16), c["v"])
            if not first:
                nd = nd + c["w_inter"] * c["qc"]
            den = nd[:, ML_V:ML_V + 1]
            h = c["h"]
            c["h_ref"][:, h * ML_V:(h + 1) * ML_V] = nd[:, 0:ML_V] / jnp.maximum(jnp.abs(den), c["floor"])

        for c in chains:
            a, m_st = c["a"], c["m_st"]
            g_log_r = a - c["bc_r"] + c["ig_r"]
            m_new = jnp.maximum(a + m_st, jnp.max(g_log_r, axis=1, keepdims=True))
            kwt = c["kt"].astype(f32) * jnp.exp(g_log_r - m_new)
            c_new = _dot(kwt.astype(bf16), c["v"])
            if not first:
                c_new = c_new + jnp.exp(a + m_st - m_new) * c["c_st"]
            c_scr[c["st"]] = c_new
            m_scr[c["st"]] = m_new

    @pl.when(j == 0)
    def _():
        step(True)

    @pl.when(j != 0)
    def _():
        step(False)


def _mlstm_scan(q, kt, v, g, gt):
    b, t, _ = q.shape
    L = CHUNK
    nc = t // L

    fwd = lambda j: j
    bwd = lambda j: jnp.where(j == 0, 0, nc - j)

    def specs(cidx):
        return [
            pl.BlockSpec((None, L, ML_HEADS * ML_QK), lambda bi, j: (bi, cidx(j), 0)),
            pl.BlockSpec((None, ML_HEADS * ML_QK, L), lambda bi, j: (bi, 0, cidx(j))),
            pl.BlockSpec((None, L, ML_INNER), lambda bi, j: (bi, cidx(j), 0)),
            pl.BlockSpec((None, L, LANES), lambda bi, j: (bi, cidx(j), 0)),
            pl.BlockSpec((None, 4 * GATE_STRIDE, L), lambda bi, j: (bi, 0, cidx(j))),
        ]

    n_chain = 2 * ML_HEADS
    h_shape = jax.ShapeDtypeStruct((b, t, ML_INNER), f32)
    return pl.pallas_call(
        _scan_kernel,
        out_shape=(h_shape, h_shape),
        grid=(b, nc),
        in_specs=specs(fwd) + specs(bwd),
        out_specs=(pl.BlockSpec((None, L, ML_INNER), lambda bi, j: (bi, fwd(j), 0)),
                   pl.BlockSpec((None, L, ML_INNER), lambda bi, j: (bi, bwd(j), 0))),
        scratch_shapes=[
            pltpu.VMEM((n_chain, ML_QK, ML_V + LANES), f32),
            pltpu.VMEM((n_chain, 1, 1), f32),
        ],
        compiler_params=_cparams(("arbitrary", "arbitrary")),
        name="mlstm_scan",
    )(q, kt, v, g, gt, q, kt, v, g, gt)


def _outproj_kernel(a_ref, hf_ref, hb_ref, o_ref, x_ref, mod_ref, ghn_ref, w_ref, lg_ref, lb_ref,
                    out_ref):
    hm = hf_ref[...] + hb_ref[...]
    og = jax.nn.sigmoid(o_ref[...].astype(f32))
    ghn = ghn_ref[...]
    y = _dot(a_ref[...], w_ref[0:MLA_HEADS * MLA_V, :])
    for h in range(ML_HEADS):
        sl = slice(h * ML_V, (h + 1) * ML_V)
        seg = hm[:, sl]
        mu = jnp.mean(seg, axis=-1, keepdims=True)
        dd = seg - mu
        var = jnp.mean(dd * dd, axis=-1, keepdims=True)
        mo = og[:, sl] * (dd * lax.rsqrt(var + EPS) * ghn[:, sl])
        r0 = MLA_HEADS * MLA_V + h * ML_V
        y = y + _dot(mo.astype(bf16), w_ref[r0:r0 + ML_V, :])
    gate = mod_ref[2:3, :]
    r = ALPHA * x_ref[...] + gate * y
    out_ref[...] = _layer_norm_rows(r, lg_ref[...], lb_ref[...])


def _out_projection(a, hf, hb, o, x, modr, g_hn, w_out, ln_g, ln_b, n_ctx_tiles):
    b, s, d = x.shape
    tm = ROW_TILE
    off = n_ctx_tiles
    mix = w_out.shape[0]
    return pl.pallas_call(
        _outproj_kernel,
        out_shape=jax.ShapeDtypeStruct((b, s, d), f32),
        grid=(b, s // tm),
        in_specs=[
            pl.BlockSpec((None, tm, MLA_HEADS * MLA_V), lambda bi, i: (bi, i, 0)),
            pl.BlockSpec((None, tm, ML_INNER), lambda bi, i: (bi, i + off, 0)),
            pl.BlockSpec((None, tm, ML_INNER), lambda bi, i: (bi, i + off, 0)),
            pl.BlockSpec((None, tm, ML_INNER), lambda bi, i: (bi, i + off, 0)),
            pl.BlockSpec((None, tm, d), lambda bi, i: (bi, i, 0)),
            pl.BlockSpec((None, N_MOD, d), lambda bi, i: (bi, 0, 0)),
            pl.BlockSpec((1, ML_INNER), lambda bi, i: (0, 0)),
            pl.BlockSpec((mix, d), lambda bi, i: (0, 0)),
            pl.BlockSpec((1, d), lambda bi, i: (0, 0)),
            pl.BlockSpec((1, d), lambda bi, i: (0, 0)),
        ],
        out_specs=pl.BlockSpec((None, tm, d), lambda bi, i: (bi, i, 0)),
        compiler_params=_cparams(("arbitrary", "arbitrary")),
        name="out_projection",
    )(a, hf, hb, o, x, modr, g_hn, w_out, ln_g, ln_b)


def _ffn_kernel(x_ref, xp_ref, xn_ref, mod_ref, wg_ref, wu_ref, cw_ref, cb_ref, wd_ref, lg_ref, lb_ref,
                out_ref, hs_ref, acc_ref, gs_ref, *, n_row_tiles):
    i = pl.program_id(1)
    j = pl.program_id(2)
    nf = pl.num_programs(2)
    tm = x_ref.shape[0]
    shift = mod_ref[3:4, :]
    scale = mod_ref[4:5, :]

    @pl.when(j == 0)
    def _():
        hs_ref[0:HALO, :] = (xp_ref[...] * (1.0 + scale) + shift).astype(bf16)
        hs_ref[HALO:HALO + tm, :] = (x_ref[...] * (1.0 + scale) + shift).astype(bf16)
        hs_ref[HALO + tm:2 * HALO + tm, :] = (xn_ref[...] * (1.0 + scale) + shift).astype(bf16)
        acc_ref[...] = jnp.zeros_like(acc_ref)

    gs_ref[...] = _dot(hs_ref[...], wg_ref[...])
    ridx = lax.broadcasted_iota(jnp.int32, (tm, 1), 0)
    g_prev = gs_ref[HALO - 1:HALO - 1 + tm, :]
    g_cur = gs_ref[HALO:HALO + tm, :]
    g_next = gs_ref[HALO + 1:HALO + 1 + tm, :]
    g_prev = jnp.where(jnp.logical_and(ridx == 0, i == 0), 0.0, g_prev)
    g_next = jnp.where(jnp.logical_and(ridx == tm - 1, i == n_row_tiles - 1), 0.0, g_next)
    gate = cw_ref[0:1, :] * g_prev + cw_ref[1:2, :] * g_cur + cw_ref[2:3, :] * g_next + cb_ref[...]
    up = _dot(hs_ref[HALO:HALO + tm, :], wu_ref[...])
    act = (_silu(gate) * up).astype(bf16)
    acc_ref[...] += _dot(act, wd_ref[...])

    @pl.when(j == nf - 1)
    def _():
        r = ALPHA * x_ref[...] + mod_ref[5:6, :] * acc_ref[...]
        out_ref[...] = _layer_norm_rows(r, lg_ref[...], lb_ref[...])


def _conv_ffn(x, modr, w_up, conv_w, conv_b, w_down, ln_g, ln_b):
    b, s, d = x.shape
    dff = w_down.shape[0]
    tm = min(FFN_TM, s)
    tf = min(FFN_TF, dff)
    assert s % tm == 0 and dff % tf == 0
    nr = s // tm
    nf = dff // tf
    hb = tm // HALO
    return pl.pallas_call(
        functools.partial(_ffn_kernel, n_row_tiles=nr),
        out_shape=jax.ShapeDtypeStruct((b, s, d), f32),
        grid=(b, nr, nf),
        in_specs=[
            pl.BlockSpec((None, tm, d), lambda bi, i, j: (bi, i, 0)),
            pl.BlockSpec((None, HALO, d), lambda bi, i, j: (bi, jnp.maximum(i * hb - 1, 0), 0)),
            pl.BlockSpec((None, HALO, d), lambda bi, i, j: (bi, jnp.minimum((i + 1) * hb, nr * hb - 1), 0)),
            pl.BlockSpec((None, N_MOD, d), lambda bi, i, j: (bi, 0, 0)),
            pl.BlockSpec((d, tf), lambda bi, i, j: (0, j)),
            pl.BlockSpec((d, tf), lambda bi, i, j: (0, nf + j)),
            pl.BlockSpec((3, tf), lambda bi, i, j: (0, j)),
            pl.BlockSpec((1, tf), lambda bi, i, j: (0, j)),
            pl.BlockSpec((tf, d), lambda bi, i, j: (j, 0)),
            pl.BlockSpec((1, d), lambda bi, i, j: (0, 0)),
            pl.BlockSpec((1, d), lambda bi, i, j: (0, 0)),
        ],
        out_specs=pl.BlockSpec((None, tm, d), lambda bi, i, j: (bi, i, 0)),
        scratch_shapes=[
            pltpu.VMEM((tm + 2 * HALO, d), bf16),
            pltpu.VMEM((tm, d), f32),
            pltpu.VMEM((tm + 2 * HALO, tf), f32),
        ],
        compiler_params=_cparams(("arbitrary", "arbitrary", "arbitrary")),
        name="conv_ffn",
    )(x, x, x, modr, w_up, w_up, conv_w, conv_b, w_down, ln_g, ln_b)


def _rope_perm():
    q = MLA_ROPE // 4
    return jnp.concatenate([jnp.arange(q, 2 * q), jnp.arange(0, q),
                            jnp.arange(3 * q, 4 * q), jnp.arange(2 * q, 3 * q)])


def _rope_tables(s):
    n_rows = s // GRID_W
    row = jnp.repeat(jnp.arange(n_rows), GRID_W).astype(f32)
    col = jnp.tile(jnp.arange(GRID_W), n_rows).astype(f32)
    n_freq = MLA_ROPE // 4
    inv = ROPE_THETA ** (-jnp.arange(n_freq, dtype=f32) / n_freq)
    ar = row[:, None] * inv
    ac = col[:, None] * inv
    cos = jnp.concatenate([jnp.cos(ar), jnp.cos(ar), jnp.cos(ac), jnp.cos(ac)], axis=-1)
    sin = jnp.concatenate([-jnp.sin(ar), jnp.sin(ar), -jnp.sin(ac), jnp.sin(ac)], axis=-1)
    return cos, sin


def _pack_w_in(w_in):
    d = w_in.shape[0]
    z = lambda n: jnp.zeros((d, n), w_in.dtype)
    gates = []
    for gi in range(4):
        gates += [w_in[:, OFF_G + gi * ML_HEADS:OFF_G + (gi + 1) * ML_HEADS], z(GATE_STRIDE - ML_HEADS)]
    cols = [w_in[:, 0:OFF_U], w_in[:, OFF_KR:OFF_U][:, _rope_perm()], w_in[:, OFF_U:OFF_G],
            *gates, z(LANES - 4 * GATE_STRIDE)]
    return jnp.concatenate(cols, axis=1).astype(bf16)


def _pack_b_gate(b_gate):
    parts = []
    for gi in range(4):
        parts += [b_gate[gi * ML_HEADS:(gi + 1) * ML_HEADS], jnp.zeros((GATE_STRIDE - ML_HEADS,), f32)]
    parts.append(jnp.zeros((LANES - 4 * GATE_STRIDE,), f32))
    return jnp.concatenate(parts)[None, :]


def kernel(x, c, ctx, c_ctx, w_ada, b_ada, w_in, g_q, w_uq, g_kv, w_ukv, ml_conv_w, ml_conv_b, w_mq, w_mk, b_gate, g_hn, w_out, ln1_g, ln1_b, w_up, ffn_conv_w, ffn_conv_b, w_down, ln2_g, ln2_b):
    b, s, d = x.shape
    n_ctx = ctx.shape[1]
    assert n_ctx % ROW_TILE == 0 and s % ROW_TILE == 0 and n_ctx % CHUNK == 0 and s % CHUNK == 0
    n_ctx_tiles = n_ctx // ROW_TILE
    layer = 0

    rows = -(-(b + 1) // SUBLANES) * SUBLANES
    cc = jnp.concatenate([c, c_ctx[None, :], jnp.zeros((rows - b - 1, d), f32)], axis=0)
    modr = _modulation(cc, w_ada[layer], b_ada[layer][None, :]).reshape(rows, N_MOD, d)

    cos, sin = _rope_tables(s)
    pad = jnp.zeros((s, LANES - MLA_ROPE), f32)
    cos_p = jnp.concatenate([cos, pad], axis=1)
    sin_p = jnp.concatenate([sin, pad], axis=1)
    ctx_c = jnp.concatenate([jnp.ones((n_ctx, MLA_ROPE), f32), jnp.zeros((n_ctx, LANES - MLA_ROPE), f32)], axis=1)
    ck = jnp.concatenate([ctx_c, cos_p], axis=0)
    sk = jnp.concatenate([jnp.zeros((n_ctx, LANES), f32), sin_p], axis=0)
    ct = cos_p.T
    st = sin_p.T

    cq, ckv, kr, g, gt, u, v, o = _in_projection(
        ctx, x, modr, _pack_w_in(w_in[layer]), g_q[layer][None, :], g_kv[layer][None, :], ck, sk,
        _pack_b_gate(b_gate[layer]), n_ctx_tiles)

    dq = MLA_NOPE + MLA_ROPE
    wq = w_uq[layer].reshape(Q_LORA, MLA_HEADS, dq)
    wq_nope = jnp.transpose(wq[:, :, :MLA_NOPE], (1, 2, 0))
    wq_rope = jnp.transpose(wq[:, :, MLA_NOPE:], (1, 2, 0))
    zq = jnp.zeros((MLA_HEADS, QK_PAD - dq, Q_LORA), f32)
    wm = jnp.concatenate([wq_nope, wq_rope, zq], axis=1).astype(bf16)
    wp = jnp.concatenate([wq_rope[:, _rope_perm(), :],
                          jnp.zeros((MLA_HEADS, LANES - MLA_ROPE, Q_LORA), f32)], axis=1).astype(bf16)
    wkv = w_ukv[layer].reshape(KV_LORA, MLA_HEADS, MLA_NOPE + MLA_V)
    wk = wkv[:, :, :MLA_NOPE].reshape(KV_LORA, MLA_HEADS * MLA_NOPE).astype(bf16)
    wvt = jnp.transpose(wkv[:, :, MLA_NOPE:], (1, 2, 0)).astype(bf16)
    kk, vt, ksq = _kv_prep(ckv, kr, wk, wvt)
    qt = _q_prep(cq, ksq, wm, wp, ct, st, n_ctx_tiles)
    a = _attention(qt, kk, vt)

    w_kt = jnp.transpose(w_mk[layer], (0, 2, 1)).astype(bf16)
    mq, mkt = _mlstm_projection(u, ml_conv_w[layer], ml_conv_b[layer][None, :], w_mq[layer].astype(bf16),
                                w_kt, n_ctx_tiles)
    hf, hb = _mlstm_scan(mq, mkt, v, g, gt)

    x1 = _out_projection(a, hf, hb, o, x, modr, g_hn[layer].reshape(1, ML_INNER), w_out[layer].astype(bf16),
                         ln1_g[layer][None, :], ln1_b[layer][None, :], n_ctx_tiles)
    return _conv_ffn(x1, modr, w_up[layer].astype(bf16), ffn_conv_w[layer], ffn_conv_b[layer][None, :],
                     w_down[layer].astype(bf16), ln2_g[layer][None, :], ln2_b[layer][None, :])
```

```python
import functools

import jax
import jax.numpy as jnp
from jax import lax
from jax.experimental import pallas as pl
from jax.experimental.pallas import tpu as pltpu

f32 = jnp.float32
bf16 = jnp.bfloat16

GRID_W = 64
MLA_HEADS = 8
MLA_NOPE = 128
MLA_ROPE = 64
MLA_V = 128
Q_LORA = 512
KV_LORA = 256
ROPE_THETA = 10000.0
ML_HEADS = 4
ML_QK = 128
ML_V = 256
ML_INNER = ML_HEADS * ML_V
N_GATES = 4 * ML_HEADS
DEPTH = 1
OFF_CKV = Q_LORA
OFF_KR = Q_LORA + KV_LORA
OFF_U = OFF_KR + MLA_ROPE
OFF_V = OFF_U + ML_INNER
OFF_O = OFF_V + ML_INNER
OFF_G = OFF_O + ML_INNER
ALPHA = (2.0 * DEPTH) ** 0.25
EPS = 1e-6
N_MOD = 6

LANES = 128
SUBLANES = 8
MXU_DIM = 256
VMEM_LIMIT = 56 * 1024 * 1024

QK_PAD = 2 * LANES
GATE_STRIDE = SUBLANES
P_Q = 0
P_CKV = P_Q + Q_LORA
P_KR = P_CKV + KV_LORA
P_U = P_KR + LANES
P_V = P_U + ML_INNER
P_O = P_V + ML_INNER
P_G = P_O + ML_INNER
P_COLS = P_G + LANES
STAB_ROWS = 16
STAB_SLACK = 1.02
L_MIN = 2.0 ** -100

ROW_TILE = 256
CHUNK = ROW_TILE
ATT_TQ = 1024
ATT_TK_MAX = 768
FFN_TM = 512
FFN_TF = 512
HALO = SUBLANES


def _cparams(sem):
    return pltpu.CompilerParams(dimension_semantics=sem, vmem_limit_bytes=VMEM_LIMIT)


def _silu(x):
    return x * jax.nn.sigmoid(x)


def _dot(a, b):
    return jnp.dot(a, b, preferred_element_type=f32)


def _dot_nt(a, b):
    return lax.dot_general(a, b, (((1,), (1,)), ((), ())), preferred_element_type=f32)


def _layer_norm_rows(r, g, b):
    mu = jnp.mean(r, axis=-1, keepdims=True)
    d = r - mu
    var = jnp.mean(d * d, axis=-1, keepdims=True)
    return d * lax.rsqrt(var + EPS) * g + b


def _mod_kernel(c_ref, w_ref, b_ref, o_ref):
    s = _silu(c_ref[...])
    o_ref[...] = _dot(s.astype(bf16), w_ref[...].astype(bf16)) + b_ref[...]


def _modulation(cc, w_ada, b_ada):
    rows, d = cc.shape
    n = w_ada.shape[1]
    tn = min(n, 1536)
    assert n % tn == 0
    return pl.pallas_call(
        _mod_kernel,
        out_shape=jax.ShapeDtypeStruct((rows, n), f32),
        grid=(n // tn,),
        in_specs=[
            pl.BlockSpec((rows, d), lambda j: (0, 0)),
            pl.BlockSpec((d, tn), lambda j: (0, j)),
            pl.BlockSpec((1, tn), lambda j: (0, j)),
        ],
        out_specs=pl.BlockSpec((rows, tn), lambda j: (0, j)),
        compiler_params=_cparams(("arbitrary",)),
        name="adaln_mod",
    )(cc, w_ada, b_ada)


def _inproj_kernel(ctx_ref, x_ref, mod_ref, w_ref, gq_ref, gkv_ref, ck_ref, sk_ref, bg_ref,
                   cq_ref, ckv_ref, kr_ref, g_ref, gt_ref, u_ref, v_ref, o_ref, h_ref, *, n_ctx_tiles):
    shift = mod_ref[0:1, :]
    scale = mod_ref[1:2, :]
    is_ctx = pl.program_id(1) < n_ctx_tiles

    @pl.when(is_ctx)
    def _():
        h_ref[...] = (ctx_ref[...] * (1.0 + scale) + shift).astype(bf16)

    @pl.when(jnp.logical_not(is_ctx))
    def _():
        h_ref[...] = (x_ref[...] * (1.0 + scale) + shift).astype(bf16)

    def sect(a, n):
        return _dot(h_ref[...], w_ref[:, a:a + n])

    zq = sect(P_Q, Q_LORA)
    cq = zq * lax.rsqrt(jnp.mean(zq * zq, axis=-1, keepdims=True) + EPS) * gq_ref[...]
    cq_ref[...] = cq.astype(bf16)
    zkv = sect(P_CKV, KV_LORA)
    ckv = zkv * lax.rsqrt(jnp.mean(zkv * zkv, axis=-1, keepdims=True) + EPS) * gkv_ref[...]
    ckv_ref[...] = ckv.astype(bf16)
    zkr = sect(P_KR, LANES)
    kr = zkr * ck_ref[...] + pltpu.roll(zkr, LANES - MLA_ROPE, 1) * sk_ref[...]
    kr_ref[...] = kr.astype(bf16)
    g = sect(P_G, LANES) + bg_ref[...]
    tm = g.shape[0]
    lf = jax.nn.log_sigmoid(g)
    tri = jnp.where(lax.broadcasted_iota(jnp.int32, (tm, tm), 0)
                    >= lax.broadcasted_iota(jnp.int32, (tm, tm), 1), 1.0, 0.0).astype(bf16)
    hi = lf.astype(bf16)
    r1 = lf - hi.astype(f32)
    mid = r1.astype(bf16)
    lo = (r1 - mid.astype(f32)).astype(bf16)
    pre = _dot(tri, hi) + _dot(tri, mid) + _dot(tri, lo)
    suf = pre[tm - 1:tm, :] - pre + lf
    grp = lax.broadcasted_iota(jnp.int32, (1, LANES), 1) // GATE_STRIDE
    g2 = jnp.where(grp == 1, pre, jnp.where(grp == 3, suf, g))
    g_ref[...] = g2
    gt_ref[...] = g2.T[0:4 * GATE_STRIDE, :]
    u_ref[...] = sect(P_U, ML_INNER).astype(bf16)
    v_ref[...] = sect(P_V, ML_INNER).astype(bf16)
    o_ref[...] = sect(P_O, ML_INNER).astype(bf16)


def _in_projection(ctx, x, modr, w_in_p, g_q, g_kv, ck, sk, bg, n_ctx_tiles):
    b, s, d = x.shape
    tm = ROW_TILE
    t = s + n_ctx_tiles * tm
    nt = t // tm
    n_batch = b

    def mod_idx(bi, i):
        return (jnp.where(i < n_ctx_tiles, n_batch, bi), 0, 0)

    row = lambda w: pl.BlockSpec((None, tm, w), lambda bi, i: (bi, i, 0))
    const = lambda r, c: pl.BlockSpec((r, c), lambda bi, i: (0, 0))
    out_shapes = (
        jax.ShapeDtypeStruct((b, t, Q_LORA), bf16),
        jax.ShapeDtypeStruct((b, t, KV_LORA), bf16),
        jax.ShapeDtypeStruct((b, t, LANES), bf16),
        jax.ShapeDtypeStruct((b, t, LANES), f32),
        jax.ShapeDtypeStruct((b, nt, 4 * GATE_STRIDE, tm), f32),
        jax.ShapeDtypeStruct((b, t, ML_INNER), bf16),
        jax.ShapeDtypeStruct((b, t, ML_INNER), bf16),
        jax.ShapeDtypeStruct((b, t, ML_INNER), bf16),
    )
    out_specs = (
        row(Q_LORA), row(KV_LORA), row(LANES), row(LANES),
        pl.BlockSpec((None, None, 4 * GATE_STRIDE, tm), lambda bi, i: (bi, i, 0, 0)),
        row(ML_INNER), row(ML_INNER), row(ML_INNER),
    )
    return pl.pallas_call(
        functools.partial(_inproj_kernel, n_ctx_tiles=n_ctx_tiles),
        out_shape=out_shapes,
        grid=(b, nt),
        in_specs=[
            pl.BlockSpec((None, tm, d), lambda bi, i: (bi, jnp.minimum(i, n_ctx_tiles - 1), 0)),
            pl.BlockSpec((None, tm, d), lambda bi, i: (bi, jnp.maximum(i - n_ctx_tiles, 0), 0)),
            pl.BlockSpec((None, N_MOD, d), mod_idx),
            const(d, P_COLS),
            const(1, Q_LORA),
            const(1, KV_LORA),
            pl.BlockSpec((tm, LANES), lambda bi, i: (i, 0)),
            pl.BlockSpec((tm, LANES), lambda bi, i: (i, 0)),
            const(1, LANES),
        ],
        out_specs=out_specs,
        scratch_shapes=[pltpu.VMEM((tm, d), bf16)],
        compiler_params=_cparams(("arbitrary", "arbitrary")),
        name="in_projection",
    )(ctx, x, modr, w_in_p, g_q, g_kv, ck, sk, bg)


def _kv_kernel(ckv_ref, kr_ref, wk_ref, wvt_ref, k_ref, vt_ref, ksq_ref):
    ckv = ckv_ref[...]
    kn = _dot(ckv, wk_ref[...]).astype(bf16)
    kr = kr_ref[...]
    knf = kn.astype(f32)
    krf = kr.astype(f32)
    hsel = lax.broadcasted_iota(jnp.int32, (MLA_HEADS, MLA_HEADS * MLA_NOPE), 0)
    csel = lax.broadcasted_iota(jnp.int32, (MLA_HEADS, MLA_HEADS * MLA_NOPE), 1) // MLA_NOPE
    sel = jnp.where(hsel == csel, 1.0, 0.0).astype(bf16)
    ksq_ref[...] = (_dot_nt(sel, (knf * knf).astype(bf16))
                    + _dot_nt(jnp.ones((MLA_HEADS, LANES), bf16), (krf * krf).astype(bf16)))
    lane = lax.broadcasted_iota(jnp.int32, kr.shape, 1)
    stab = jnp.logical_and(lane >= MLA_ROPE, lane < MLA_ROPE + STAB_ROWS)
    kr_aug = jnp.where(stab, 1.0, krf).astype(bf16)
    for h in range(MLA_HEADS):
        k_ref[h, :, 0:MLA_NOPE] = kn[:, h * MLA_NOPE:(h + 1) * MLA_NOPE]
        k_ref[h, :, MLA_NOPE:QK_PAD] = kr_aug
        vt_ref[h] = _dot_nt(wvt_ref[h], ckv).astype(bf16)


def _kv_prep(ckv, kr, wk, wvt):
    b, t, _ = ckv.shape
    tm = ROW_TILE
    return pl.pallas_call(
        _kv_kernel,
        out_shape=(
            jax.ShapeDtypeStruct((b, MLA_HEADS, t, QK_PAD), bf16),
            jax.ShapeDtypeStruct((b, MLA_HEADS, MLA_V, t), bf16),
            jax.ShapeDtypeStruct((b, MLA_HEADS, t), f32),
        ),
        grid=(b, t // tm),
        in_specs=[
            pl.BlockSpec((None, tm, KV_LORA), lambda bi, i: (bi, i, 0)),
            pl.BlockSpec((None, tm, LANES), lambda bi, i: (bi, i, 0)),
            pl.BlockSpec((KV_LORA, MLA_HEADS * MLA_NOPE), lambda bi, i: (0, 0)),
            pl.BlockSpec((MLA_HEADS, MLA_V, KV_LORA), lambda bi, i: (0, 0, 0)),
        ],
        out_specs=(
            pl.BlockSpec((None, MLA_HEADS, tm, QK_PAD), lambda bi, i: (bi, 0, i, 0)),
            pl.BlockSpec((None, MLA_HEADS, MLA_V, tm), lambda bi, i: (bi, 0, 0, i)),
            pl.BlockSpec((None, MLA_HEADS, tm), lambda bi, i: (bi, 0, i)),
        ),
        compiler_params=_cparams(("arbitrary", "arbitrary")),
        name="kv_prep",
    )(ckv, kr, wk, wvt)


def _q_kernel(cq_ref, ksq_ref, wm_ref, wp_ref, ct_ref, st_ref, qt_ref, *, scale):
    cq = cq_ref[...]
    ct = ct_ref[...]
    st = st_ref[...]
    kmax2 = jnp.max(ksq_ref[...], axis=1, keepdims=True)
    ridx = lax.broadcasted_iota(jnp.int32, (LANES, 1), 0)
    stab = jnp.logical_and(ridx >= MLA_ROPE, ridx < MLA_ROPE + STAB_ROWS)
    for h in range(MLA_HEADS):
        qm = _dot_nt(wm_ref[h], cq)
        qp = _dot_nt(wp_ref[h], cq)
        qn = (qm[0:MLA_NOPE] * scale).astype(bf16)
        qr = ((qm[MLA_NOPE:QK_PAD] * ct + qp * st) * scale).astype(bf16)
        qnf = qn.astype(f32)
        qrf = qr.astype(f32)
        qsq = jnp.sum(qnf * qnf, axis=0, keepdims=True) + jnp.sum(qrf * qrf, axis=0, keepdims=True)
        bound = jnp.sqrt(qsq * kmax2[h:h + 1, :]) * STAB_SLACK
        qt_ref[h, 0:MLA_NOPE, :] = qn
        qt_ref[h, MLA_NOPE:QK_PAD, :] = jnp.where(stab, bound * (-1.0 / STAB_ROWS), qrf).astype(bf16)


def _q_prep(cq, ksq, wm, wp, ct, st, n_ctx_tiles):
    b, t, _ = cq.shape
    tm = ROW_TILE
    s = t - n_ctx_tiles * tm
    scale = float((MLA_NOPE + MLA_ROPE) ** -0.5 * 1.4426950408889634)
    return pl.pallas_call(
        functools.partial(_q_kernel, scale=scale),
        out_shape=jax.ShapeDtypeStruct((b, MLA_HEADS, QK_PAD, s), bf16),
        grid=(b, s // tm),
        in_specs=[
            pl.BlockSpec((None, tm, Q_LORA), lambda bi, i: (bi, i + n_ctx_tiles, 0)),
            pl.BlockSpec((None, MLA_HEADS, t), lambda bi, i: (bi, 0, 0)),
            pl.BlockSpec((MLA_HEADS, QK_PAD, Q_LORA), lambda bi, i: (0, 0, 0)),
            pl.BlockSpec((MLA_HEADS, LANES, Q_LORA), lambda bi, i: (0, 0, 0)),
            pl.BlockSpec((LANES, tm), lambda bi, i: (0, i)),
            pl.BlockSpec((LANES, tm), lambda bi, i: (0, i)),
        ],
        out_specs=pl.BlockSpec((None, MLA_HEADS, QK_PAD, tm), lambda bi, i: (bi, 0, 0, i)),
        compiler_params=_cparams(("arbitrary", "arbitrary")),
        name="q_prep",
    )(cq, ksq, wm, wp, ct, st)


def _attn_bound_kernel(qt_ref, k_ref, vt_ref, o_ref, l_ref, *, tk, nk):
    qt = qt_ref[...]
    tq = qt.shape[1]
    l = jnp.zeros((1, tq), f32)
    acc = jnp.zeros((MLA_V, tq), f32)
    s = _dot(k_ref[0:tk, :], qt)
    for j in range(nk):
        s_next = _dot(k_ref[(j + 1) * tk:(j + 2) * tk, :], qt) if j + 1 < nk else None
        p = jnp.exp2(s)
        l = l + jnp.sum(p, axis=0, keepdims=True)
        acc = acc + _dot(vt_ref[:, j * tk:(j + 1) * tk], p.astype(bf16))
        s = s_next
    o_ref[...] = (acc / l).T.astype(o_ref.dtype)
    l_ref[...] = l


def _attn_online_kernel(qt_ref, k_ref, vt_ref, o_ref, *, tk, nk):
    qt = qt_ref[...]
    tq = qt.shape[1]
    m = jnp.full((1, tq), -jnp.inf, f32)
    l = jnp.zeros((1, tq), f32)
    acc = jnp.zeros((MLA_V, tq), f32)
    for j in range(nk):
        s = _dot(k_ref[j * tk:(j + 1) * tk, :], qt)
        m_new = jnp.maximum(m, jnp.max(s, axis=0, keepdims=True))
        alpha = jnp.exp2(m - m_new)
        p = jnp.exp2(s - m_new)
        l = alpha * l + jnp.sum(p, axis=0, keepdims=True)
        acc = alpha * acc + _dot(vt_ref[:, j * tk:(j + 1) * tk], p.astype(bf16))
        m = m_new
    o_ref[...] = (acc / l).T.astype(o_ref.dtype)


def _attention(qt, k, vt):
    b, hh, _, s = qt.shape
    t = k.shape[2]
    tq = min(ATT_TQ, s)
    tk = MXU_DIM
    for cand in range(MXU_DIM, ATT_TK_MAX + 1, MXU_DIM):
        if t % cand == 0:
            tk = cand
    assert s % tq == 0 and t % tk == 0
    in_specs = [
        pl.BlockSpec((None, None, QK_PAD, tq), lambda bi, h, i: (bi, h, 0, i)),
        pl.BlockSpec((None, None, t, QK_PAD), lambda bi, h, i: (bi, h, 0, 0)),
        pl.BlockSpec((None, None, MLA_V, t), lambda bi, h, i: (bi, h, 0, 0)),
    ]
    a_shape = jax.ShapeDtypeStruct((b, s, hh * MLA_V), bf16)
    a_spec = pl.BlockSpec((None, tq, MLA_V), lambda bi, h, i: (bi, i, h))
    sem = ("arbitrary", "arbitrary", "arbitrary")
    a, l = pl.pallas_call(
        functools.partial(_attn_bound_kernel, tk=tk, nk=t // tk),
        out_shape=(a_shape, jax.ShapeDtypeStruct((b, hh, 1, s), f32)),
        grid=(b, hh, s // tq),
        in_specs=in_specs,
        out_specs=(a_spec, pl.BlockSpec((None, None, 1, tq), lambda bi, h, i: (bi, h, 0, i))),
        compiler_params=_cparams(sem),
        name="attention_bound",
    )(qt, k, vt)

    def online(_):
        return pl.pallas_call(
            functools.partial(_attn_online_kernel, tk=tk, nk=t // tk),
            out_shape=a_shape,
            grid=(b, hh, s // tq),
            in_specs=in_specs,
            out_specs=a_spec,
            compiler_params=_cparams(sem),
            name="attention_online",
        )(qt, k, vt)

    ok = jnp.min(l) >= L_MIN
    return lax.cond(ok, lambda _: a, online, None)


def _mproj_kernel(u_ref, up_ref, un_ref, cw_ref, cb_ref, wq_ref, wkt_ref, q_ref, kt_ref, *,
                  n_ctx_tiles, n_tiles):
    i = pl.program_id(1)
    u = u_ref[...].astype(f32)
    tm = u.shape[0]
    first = jnp.logical_or(i == 0, i == n_ctx_tiles)
    last = jnp.logical_or(i == n_ctx_tiles - 1, i == n_tiles - 1)
    prev_row = jnp.where(first, 0.0, up_ref[HALO - 1:HALO, :].astype(f32))
    next_row = jnp.where(last, 0.0, un_ref[0:1, :].astype(f32))
    ridx = lax.broadcasted_iota(jnp.int32, (tm, 1), 0)
    u_prev = jnp.where(ridx == 0, prev_row, pltpu.roll(u, 1, 0))
    u_next = jnp.where(ridx == tm - 1, next_row, pltpu.roll(u, tm - 1, 0))
    y = cw_ref[0:1, :] * u_prev + cw_ref[1:2, :] * u + cw_ref[2:3, :] * u_next + cb_ref[...]
    act = _silu(y).astype(bf16)
    for h in range(ML_HEADS):
        a_h = act[:, h * ML_V:(h + 1) * ML_V]
        q_ref[:, h * ML_QK:(h + 1) * ML_QK] = (_dot(a_h, wq_ref[h]) * float(ML_QK ** -0.5)).astype(bf16)
        kt_ref[h * ML_QK:(h + 1) * ML_QK, :] = _dot_nt(wkt_ref[h], a_h).astype(bf16)


def _mlstm_projection(u, conv_w, conv_b, w_q, w_kt, n_ctx_tiles):
    b, t, _ = u.shape
    tm = ROW_TILE
    nt = t // tm
    hb = tm // HALO
    return pl.pallas_call(
        functools.partial(_mproj_kernel, n_ctx_tiles=n_ctx_tiles, n_tiles=nt),
        out_shape=(jax.ShapeDtypeStruct((b, t, ML_HEADS * ML_QK), bf16),
                   jax.ShapeDtypeStruct((b, nt, ML_HEADS * ML_QK, tm), bf16)),
        grid=(b, nt),
        in_specs=[
            pl.BlockSpec((None, tm, ML_INNER), lambda bi, i: (bi, i, 0)),
            pl.BlockSpec((None, HALO, ML_INNER), lambda bi, i: (bi, jnp.maximum(i * hb - 1, 0), 0)),
            pl.BlockSpec((None, HALO, ML_INNER),
                         lambda bi, i: (bi, jnp.minimum((i + 1) * hb, nt * hb - 1), 0)),
            pl.BlockSpec((3, ML_INNER), lambda bi, i: (0, 0)),
            pl.BlockSpec((1, ML_INNER), lambda bi, i: (0, 0)),
            pl.BlockSpec((ML_HEADS, ML_V, ML_QK), lambda bi, i: (0, 0, 0)),
            pl.BlockSpec((ML_HEADS, ML_QK, ML_V), lambda bi, i: (0, 0, 0)),
        ],
        out_specs=(pl.BlockSpec((None, tm, ML_HEADS * ML_QK), lambda bi, i: (bi, i, 0)),
                   pl.BlockSpec((None, None, ML_HEADS * ML_QK, tm), lambda bi, i: (bi, i, 0, 0))),
        compiler_params=_cparams(("arbitrary", "arbitrary")),
        name="mlstm_projection",
    )(u, u, u, conv_w, conv_b, w_q, w_kt)


def _scan_kernel(qf_ref, ktf_ref, vf_ref, gf_ref, gtf_ref, qb_ref, ktb_ref, vb_ref, gb_ref, gtb_ref,
                 hf_ref, hb_ref, c_scr, m_scr):
    j = pl.program_id(1)
    L = qf_ref.shape[0]
    GS = GATE_STRIDE

    rr = lax.broadcasted_iota(jnp.int32, (L, L), 0)
    cc = lax.broadcasted_iota(jnp.int32, (L, L), 1)
    visible = (rr >= cc, rr <= cc)
    dirs = ((qf_ref, ktf_ref, vf_ref, gf_ref, gtf_ref, hf_ref), (qb_ref, ktb_ref, vb_ref, gb_ref, gtb_ref, hb_ref))
    ones_col = jnp.where(lax.broadcasted_iota(jnp.int32, (L, LANES), 1) == 0, 1.0, 0.0).astype(bf16)

    chains = []
    for d, (q_ref, kt_ref, v_ref, g_ref, gt_ref, h_ref) in enumerate(dirs):
        g = g_ref[...]
        gt = gt_ref[...]
        for h in range(ML_HEADS):
            ci = 2 * d * GS + h
            cf = ci + GS
            bc_r = gt[cf:cf + 1, :]
            chains.append(dict(
                d=d, h=h, st=d * ML_HEADS + h, h_ref=h_ref,
                q=q_ref[:, h * ML_QK:(h + 1) * ML_QK],
                kt=kt_ref[h * ML_QK:(h + 1) * ML_QK, :],
                v=jnp.concatenate([v_ref[:, h * ML_V:(h + 1) * ML_V], ones_col], axis=1),
                bc_c=g[:, cf:cf + 1],
                ig_r=gt[ci:ci + 1, :], bc_r=bc_r,
                a=bc_r[:, L - 1:L] if d == 0 else bc_r[:, 0:1],
            ))

    def step(first):
        for c in chains:
            c["s"] = _dot(c["q"], c["kt"])
        for c in chains:
            if not first:
                c["c_st"] = c_scr[c["st"]]
                c["m_st"] = m_scr[c["st"]]
                c["qc"] = _dot(c["q"], c["c_st"].astype(bf16))
            else:
                c["m_st"] = 0.0

        for c in chains:
            drow = jnp.where(visible[c["d"]], c["ig_r"] - c["bc_r"], -jnp.inf)
            mm = jnp.maximum(c["m_st"], jnp.max(drow, axis=1, keepdims=True))
            c["w_inter"] = jnp.exp(c["m_st"] - mm)
            c["floor"] = jnp.exp(-(c["bc_c"] + mm))
            c["p"] = jnp.exp(drow - mm) * c["s"]

        for c in chains:
            nd = _dot(c["p"].astype(bf16), c["v"])
            if not first:
                nd = nd + c["w_inter"] * c["qc"]
            den = nd[:, ML_V:ML_V + 1]
            h = c["h"]
            c["h_ref"][:, h * ML_V:(h + 1) * ML_V] = nd[:, 0:ML_V] / jnp.maximum(jnp.abs(den), c["floor"])

        for c in chains:
            a, m_st = c["a"], c["m_st"]
            g_log_r = a - c["bc_r"] + c["ig_r"]
            m_new = jnp.maximum(a + m_st, jnp.max(g_log_r, axis=1, keepdims=True))
            kwt = c["kt"].astype(f32) * jnp.exp(g_log_r - m_new)
            c_new = _dot(kwt.astype(bf16), c["v"])
            if not first:
                c_new = c_new + jnp.exp(a + m_st - m_new) * c["c_st"]
            c_scr[c["st"]] = c_new
            m_scr[c["st"]] = m_new

    @pl.when(j == 0)
    def _():
        step(True)

    @pl.when(j != 0)
    def _():
        step(False)


def _mlstm_scan(q, kt, v, g, gt):
    b, t, _ = q.shape
    L = CHUNK
    nc = t // L

    fwd = lambda j: j
    bwd = lambda j: jnp.where(j == 0, 0, nc - j)

    def specs(cidx):
        return [
            pl.BlockSpec((None, L, ML_HEADS * ML_QK), lambda bi, j: (bi, cidx(j), 0)),
            pl.BlockSpec((None, None, ML_HEADS * ML_QK, L), lambda bi, j: (bi, cidx(j), 0, 0)),
            pl.BlockSpec((None, L, ML_INNER), lambda bi, j: (bi, cidx(j), 0)),
            pl.BlockSpec((None, L, LANES), lambda bi, j: (bi, cidx(j), 0)),
            pl.BlockSpec((None, None, 4 * GATE_STRIDE, L), lambda bi, j: (bi, cidx(j), 0, 0)),
        ]

    n_chain = 2 * ML_HEADS
    h_shape = jax.ShapeDtypeStruct((b, t, ML_INNER), f32)
    return pl.pallas_call(
        _scan_kernel,
        out_shape=(h_shape, h_shape),
        grid=(b, nc),
        in_specs=specs(fwd) + specs(bwd),
        out_specs=(pl.BlockSpec((None, L, ML_INNER), lambda bi, j: (bi, fwd(j), 0)),
                   pl.BlockSpec((None, L, ML_INNER), lambda bi, j: (bi, bwd(j), 0))),
        scratch_shapes=[
            pltpu.VMEM((n_chain, ML_QK, ML_V + LANES), f32),
            pltpu.VMEM((n_chain, 1, 1), f32),
        ],
        compiler_params=_cparams(("arbitrary", "arbitrary")),
        name="mlstm_scan",
    )(q, kt, v, g, gt, q, kt, v, g, gt)


def _outproj_kernel(a_ref, hf_ref, hb_ref, o_ref, x_ref, mod_ref, ghn_ref, w_ref, lg_ref, lb_ref,
                    out_ref):
    hm = hf_ref[...] + hb_ref[...]
    og = jax.nn.sigmoid(o_ref[...].astype(f32))
    ghn = ghn_ref[...]
    y = _dot(a_ref[...], w_ref[0:MLA_HEADS * MLA_V, :])
    for h in range(ML_HEADS):
        sl = slice(h * ML_V, (h + 1) * ML_V)
        seg = hm[:, sl]
        mu = jnp.mean(seg, axis=-1, keepdims=True)
        dd = seg - mu
        var = jnp.mean(dd * dd, axis=-1, keepdims=True)
        mo = og[:, sl] * (dd * lax.rsqrt(var + EPS) * ghn[:, sl])
        r0 = MLA_HEADS * MLA_V + h * ML_V
        y = y + _dot(mo.astype(bf16), w_ref[r0:r0 + ML_V, :])
    gate = mod_ref[2:3, :]
    r = ALPHA * x_ref[...] + gate * y
    out_ref[...] = _layer_norm_rows(r, lg_ref[...], lb_ref[...])


def _out_projection(a, hf, hb, o, x, modr, g_hn, w_out, ln_g, ln_b, n_ctx_tiles):
    b, s, d = x.shape
    tm = ROW_TILE
    off = n_ctx_tiles
    mix = w_out.shape[0]
    return pl.pallas_call(
        _outproj_kernel,
        out_shape=jax.ShapeDtypeStruct((b, s, d), f32),
        grid=(b, s // tm),
        in_specs=[
            pl.BlockSpec((None, tm, MLA_HEADS * MLA_V), lambda bi, i: (bi, i, 0)),
            pl.BlockSpec((None, tm, ML_INNER), lambda bi, i: (bi, i + off, 0)),
            pl.BlockSpec((None, tm, ML_INNER), lambda bi, i: (bi, i + off, 0)),
            pl.BlockSpec((None, tm, ML_INNER), lambda bi, i: (bi, i + off, 0)),
            pl.BlockSpec((None, tm, d), lambda bi, i: (bi, i, 0)),
            pl.BlockSpec((None, N_MOD, d), lambda bi, i: (bi, 0, 0)),
            pl.BlockSpec((1, ML_INNER), lambda bi, i: (0, 0)),
            pl.BlockSpec((mix, d), lambda bi, i: (0, 0)),
            pl.BlockSpec((1, d), lambda bi, i: (0, 0)),
            pl.BlockSpec((1, d), lambda bi, i: (0, 0)),
        ],
        out_specs=pl.BlockSpec((None, tm, d), lambda bi, i: (bi, i, 0)),
        compiler_params=_cparams(("arbitrary", "arbitrary")),
        name="out_projection",
    )(a, hf, hb, o, x, modr, g_hn, w_out, ln_g, ln_b)


def _ffn_kernel(x_ref, xp_ref, xn_ref, mod_ref, wup_hbm, cw_ref, cb_ref, wd_hbm, lg_ref, lb_ref,
                out_ref, hs_ref, gs_ref, wg_buf, wu_buf, wd_buf, sem, *, n_row_tiles, nf, tf, nbuf):
    i = pl.program_id(1)
    step = pl.program_id(0) * n_row_tiles + i
    n_steps = pl.num_programs(0) * n_row_tiles
    tm = x_ref.shape[0]
    dff = nf * tf

    def weight_copies(j):
        slot = j % nbuf
        return (
            pltpu.make_async_copy(wup_hbm.at[:, pl.ds(j * tf, tf)], wg_buf.at[slot], sem.at[0, slot]),
            pltpu.make_async_copy(wup_hbm.at[:, pl.ds(dff + j * tf, tf)], wu_buf.at[slot], sem.at[1, slot]),
            pltpu.make_async_copy(wd_hbm.at[pl.ds(j * tf, tf), :], wd_buf.at[slot], sem.at[2, slot]),
        )

    @pl.when(step == 0)
    def _():
        for c in weight_copies(0):
            c.start()

    shift = mod_ref[3:4, :]
    scale = mod_ref[4:5, :]
    hs_ref[0:HALO, :] = (xp_ref[...] * (1.0 + scale) + shift).astype(bf16)
    hs_ref[HALO:HALO + tm, :] = (x_ref[...] * (1.0 + scale) + shift).astype(bf16)
    hs_ref[HALO + tm:2 * HALO + tm, :] = (xn_ref[...] * (1.0 + scale) + shift).astype(bf16)
    ridx = lax.broadcasted_iota(jnp.int32, (tm, 1), 0)
    first_row = jnp.logical_and(ridx == 0, i == 0)
    last_row = jnp.logical_and(ridx == tm - 1, i == n_row_tiles - 1)

    for j in range(nf):
        for c in weight_copies((j + 1) % nf):
            c.start()
        for c in weight_copies(j):
            c.wait()
        slot = j % nbuf
        gs = gs_ref.at[j % 2]
        cols = slice(j * tf, (j + 1) * tf)
        gs[...] = _dot(hs_ref[...], wg_buf[slot])
        g_prev = jnp.where(first_row, 0.0, gs[HALO - 1:HALO - 1 + tm, :])
        g_next = jnp.where(last_row, 0.0, gs[HALO + 1:HALO + 1 + tm, :])
        gate = (cw_ref[0:1, cols] * g_prev + cw_ref[1:2, cols] * gs[HALO:HALO + tm, :]
                + cw_ref[2:3, cols] * g_next + cb_ref[:, cols])
        up = _dot(hs_ref[HALO:HALO + tm, :], wu_buf[slot])
        act = (_silu(gate) * up).astype(bf16)
        contrib = _dot(act, wd_buf[slot])
        if j == 0:
            out_ref[...] = contrib
        else:
            out_ref[...] += contrib

    r = ALPHA * x_ref[...] + mod_ref[5:6, :] * out_ref[...]
    out_ref[...] = _layer_norm_rows(r, lg_ref[...], lb_ref[...])

    @pl.when(step == n_steps - 1)
    def _():
        for c in weight_copies(0):
            c.wait()


def _conv_ffn(x, modr, w_up, conv_w, conv_b, w_down, ln_g, ln_b):
    b, s, d = x.shape
    dff = w_down.shape[0]
    tm = min(FFN_TM, s)
    tf = min(FFN_TF, dff)
    assert s % tm == 0 and dff % tf == 0
    nr = s // tm
    nf = dff // tf
    assert nf >= 2
    nbuf = next(n for n in range(2, nf + 1) if (nf - 1) % n != 0)
    hb = tm // HALO
    return pl.pallas_call(
        functools.partial(_ffn_kernel, n_row_tiles=nr, nf=nf, tf=tf, nbuf=nbuf),
        out_shape=jax.ShapeDtypeStruct((b, s, d), f32),
        grid=(b, nr),
        in_specs=[
            pl.BlockSpec((None, tm, d), lambda bi, i: (bi, i, 0)),
            pl.BlockSpec((None, HALO, d), lambda bi, i: (bi, jnp.maximum(i * hb - 1, 0), 0)),
            pl.BlockSpec((None, HALO, d), lambda bi, i: (bi, jnp.minimum((i + 1) * hb, nr * hb - 1), 0)),
            pl.BlockSpec((None, N_MOD, d), lambda bi, i: (bi, 0, 0)),
            pl.BlockSpec(memory_space=pl.ANY),
            pl.BlockSpec((3, dff), lambda bi, i: (0, 0)),
            pl.BlockSpec((1, dff), lambda bi, i: (0, 0)),
            pl.BlockSpec(memory_space=pl.ANY),
            pl.BlockSpec((1, d), lambda bi, i: (0, 0)),
            pl.BlockSpec((1, d), lambda bi, i: (0, 0)),
        ],
        out_specs=pl.BlockSpec((None, tm, d), lambda bi, i: (bi, i, 0)),
        scratch_shapes=[
            pltpu.VMEM((tm + 2 * HALO, d), bf16),
            pltpu.VMEM((2, tm + 2 * HALO, tf), f32),
            pltpu.VMEM((nbuf, d, tf), bf16),
            pltpu.VMEM((nbuf, d, tf), bf16),
            pltpu.VMEM((nbuf, tf, d), bf16),
            pltpu.SemaphoreType.DMA((3, nbuf)),
        ],
        compiler_params=_cparams(("arbitrary", "arbitrary")),
        name="conv_ffn",
    )(x, x, x, modr, w_up, conv_w, conv_b, w_down, ln_g, ln_b)


def _rope_perm():
    q = MLA_ROPE // 4
    return jnp.concatenate([jnp.arange(q, 2 * q), jnp.arange(0, q),
                            jnp.arange(3 * q, 4 * q), jnp.arange(2 * q, 3 * q)])


def _rope_tables(s):
    n_rows = s // GRID_W
    row = jnp.repeat(jnp.arange(n_rows), GRID_W).astype(f32)
    col = jnp.tile(jnp.arange(GRID_W), n_rows).astype(f32)
    n_freq = MLA_ROPE // 4
    inv = ROPE_THETA ** (-jnp.arange(n_freq, dtype=f32) / n_freq)
    ar = row[:, None] * inv
    ac = col[:, None] * inv
    cos = jnp.concatenate([jnp.cos(ar), jnp.cos(ar), jnp.cos(ac), jnp.cos(ac)], axis=-1)
    sin = jnp.concatenate([-jnp.sin(ar), jnp.sin(ar), -jnp.sin(ac), jnp.sin(ac)], axis=-1)
    return cos, sin


def _pack_w_in(w_in):
    d = w_in.shape[0]
    z = lambda n: jnp.zeros((d, n), w_in.dtype)
    gates = []
    for gi in range(4):
        gates += [w_in[:, OFF_G + gi * ML_HEADS:OFF_G + (gi + 1) * ML_HEADS], z(GATE_STRIDE - ML_HEADS)]
    cols = [w_in[:, 0:OFF_U], w_in[:, OFF_KR:OFF_U][:, _rope_perm()], w_in[:, OFF_U:OFF_G],
            *gates, z(LANES - 4 * GATE_STRIDE)]
    return jnp.concatenate(cols, axis=1).astype(bf16)


def _pack_b_gate(b_gate):
    parts = []
    for gi in range(4):
        parts += [b_gate[gi * ML_HEADS:(gi + 1) * ML_HEADS], jnp.zeros((GATE_STRIDE - ML_HEADS,), f32)]
    parts.append(jnp.zeros((LANES - 4 * GATE_STRIDE,), f32))
    return jnp.concatenate(parts)[None, :]


def kernel(x, c, ctx, c_ctx, w_ada, b_ada, w_in, g_q, w_uq, g_kv, w_ukv, ml_conv_w, ml_conv_b, w_mq, w_mk, b_gate, g_hn, w_out, ln1_g, ln1_b, w_up, ffn_conv_w, ffn_conv_b, w_down, ln2_g, ln2_b):
    b, s, d = x.shape
    n_ctx = ctx.shape[1]
    assert n_ctx % ROW_TILE == 0 and s % ROW_TILE == 0 and n_ctx % CHUNK == 0 and s % CHUNK == 0
    n_ctx_tiles = n_ctx // ROW_TILE
    layer = 0

    rows = -(-(b + 1) // SUBLANES) * SUBLANES
    cc = jnp.concatenate([c, c_ctx[None, :], jnp.zeros((rows - b - 1, d), f32)], axis=0)
    modr = _modulation(cc, w_ada[layer], b_ada[layer][None, :]).reshape(rows, N_MOD, d)

    cos, sin = _rope_tables(s)
    pad = jnp.zeros((s, LANES - MLA_ROPE), f32)
    cos_p = jnp.concatenate([cos, pad], axis=1)
    sin_p = jnp.concatenate([sin, pad], axis=1)
    ctx_c = jnp.concatenate([jnp.ones((n_ctx, MLA_ROPE), f32), jnp.zeros((n_ctx, LANES - MLA_ROPE), f32)], axis=1)
    ck = jnp.concatenate([ctx_c, cos_p], axis=0)
    sk = jnp.concatenate([jnp.zeros((n_ctx, LANES), f32), sin_p], axis=0)
    ct = cos_p.T
    st = sin_p.T

    cq, ckv, kr, g, gt, u, v, o = _in_projection(
        ctx, x, modr, _pack_w_in(w_in[layer]), g_q[layer][None, :], g_kv[layer][None, :], ck, sk,
        _pack_b_gate(b_gate[layer]), n_ctx_tiles)

    dq = MLA_NOPE + MLA_ROPE
    wq = w_uq[layer].reshape(Q_LORA, MLA_HEADS, dq)
    wq_nope = jnp.transpose(wq[:, :, :MLA_NOPE], (1, 2, 0))
    wq_rope = jnp.transpose(wq[:, :, MLA_NOPE:], (1, 2, 0))
    zq = jnp.zeros((MLA_HEADS, QK_PAD - dq, Q_LORA), f32)
    wm = jnp.concatenate([wq_nope, wq_rope, zq], axis=1).astype(bf16)
    wp = jnp.concatenate([wq_rope[:, _rope_perm(), :],
                          jnp.zeros((MLA_HEADS, LANES - MLA_ROPE, Q_LORA), f32)], axis=1).astype(bf16)
    wkv = w_ukv[layer].reshape(KV_LORA, MLA_HEADS, MLA_NOPE + MLA_V)
    wk = wkv[:, :, :MLA_NOPE].reshape(KV_LORA, MLA_HEADS * MLA_NOPE).astype(bf16)
    wvt = jnp.transpose(wkv[:, :, MLA_NOPE:], (1, 2, 0)).astype(bf16)
    kk, vt, ksq = _kv_prep(ckv, kr, wk, wvt)
    qt = _q_prep(cq, ksq, wm, wp, ct, st, n_ctx_tiles)
    a = _attention(qt, kk, vt)

    w_kt = jnp.transpose(w_mk[layer], (0, 2, 1)).astype(bf16)
    mq, mkt = _mlstm_projection(u, ml_conv_w[layer], ml_conv_b[layer][None, :], w_mq[layer].astype(bf16),
                                w_kt, n_ctx_tiles)
    hf, hb = _mlstm_scan(mq, mkt, v, g, gt)

    x1 = _out_projection(a, hf, hb, o, x, modr, g_hn[layer].reshape(1, ML_INNER), w_out[layer].astype(bf16),
                         ln1_g[layer][None, :], ln1_b[layer][None, :], n_ctx_tiles)
    return _conv_ffn(x1, modr, w_up[layer].astype(bf16), ffn_conv_w[layer], ffn_conv_b[layer][None, :],
                     w_down[layer].astype(bf16), ln2_g[layer][None, :], ln2_b[layer][None, :])
```

```python
import functools

import jax
import jax.numpy as jnp
from jax import lax
from jax.experimental import pallas as pl
from jax.experimental.pallas import tpu as pltpu

f32 = jnp.float32
bf16 = jnp.bfloat16

GRID_W = 64
MLA_HEADS = 8
MLA_NOPE = 128
MLA_ROPE = 64
MLA_V = 128
Q_LORA = 512
KV_LORA = 256
ROPE_THETA = 10000.0
ML_HEADS = 4
ML_QK = 128
ML_V = 256
ML_INNER = ML_HEADS * ML_V
N_GATES = 4 * ML_HEADS
DEPTH = 1
OFF_CKV = Q_LORA
OFF_KR = Q_LORA + KV_LORA
OFF_U = OFF_KR + MLA_ROPE
OFF_V = OFF_U + ML_INNER
OFF_O = OFF_V + ML_INNER
OFF_G = OFF_O + ML_INNER
ALPHA = (2.0 * DEPTH) ** 0.25
EPS = 1e-6
N_MOD = 6

LANES = 128
SUBLANES = 8
MXU_DIM = 256
VMEM_LIMIT = 56 * 1024 * 1024

QK_PAD = 2 * LANES
GATE_STRIDE = SUBLANES
P_Q = 0
P_CKV = P_Q + Q_LORA
P_KR = P_CKV + KV_LORA
P_U = P_KR + LANES
P_V = P_U + ML_INNER
P_O = P_V + ML_INNER
P_G = P_O + ML_INNER
P_COLS = P_G + LANES
STAB_ROWS = 16
STAB_SLACK = 1.02
L_MIN = 2.0 ** -100

ROW_TILE = 256
CHUNK = ROW_TILE
ATT_TQ = 1024
ATT_TK_MAX = 768
OUT_TM = 512
FFN_TM = 512
FFN_TF = 512
HALO = SUBLANES


def _cparams(sem):
    return pltpu.CompilerParams(dimension_semantics=sem, vmem_limit_bytes=VMEM_LIMIT)


def _silu(x):
    return x * jax.nn.sigmoid(x)


def _dot(a, b):
    return jnp.dot(a, b, preferred_element_type=f32)


def _dot_nt(a, b):
    return lax.dot_general(a, b, (((1,), (1,)), ((), ())), preferred_element_type=f32)


def _layer_norm_rows(r, g, b):
    mu = jnp.mean(r, axis=-1, keepdims=True)
    d = r - mu
    var = jnp.mean(d * d, axis=-1, keepdims=True)
    return d * lax.rsqrt(var + EPS) * g + b


def _mod_kernel(c_ref, w_ref, b_ref, o_ref):
    s = _silu(c_ref[...])
    o_ref[...] = _dot(s.astype(bf16), w_ref[...].astype(bf16)) + b_ref[...]


def _modulation(cc, w_ada, b_ada):
    rows, d = cc.shape
    n = w_ada.shape[1]
    tn = min(n, 1536)
    assert n % tn == 0
    return pl.pallas_call(
        _mod_kernel,
        out_shape=jax.ShapeDtypeStruct((rows, n), f32),
        grid=(n // tn,),
        in_specs=[
            pl.BlockSpec((rows, d), lambda j: (0, 0)),
            pl.BlockSpec((d, tn), lambda j: (0, j)),
            pl.BlockSpec((1, tn), lambda j: (0, j)),
        ],
        out_specs=pl.BlockSpec((rows, tn), lambda j: (0, j)),
        compiler_params=_cparams(("arbitrary",)),
        name="adaln_mod",
    )(cc, w_ada, b_ada)


def _inproj_kernel(ctx_ref, x_ref, mod_ref, w_ref, gq_ref, gkv_ref, ck_ref, sk_ref, bg_ref,
                   cq_ref, ckv_ref, kr_ref, g_ref, gt_ref, u_ref, v_ref, o_ref, h_ref, *, n_ctx_tiles):
    shift = mod_ref[0:1, :]
    scale = mod_ref[1:2, :]
    is_ctx = pl.program_id(1) < n_ctx_tiles
    rows = jnp.where(is_ctx, ctx_ref[...], x_ref[...])
    h_ref[...] = (rows * (1.0 + scale) + shift).astype(bf16)

    def sect(a, n):
        return _dot(h_ref[...], w_ref[:, a:a + n])

    zq = sect(P_Q, Q_LORA)
    cq = zq * lax.rsqrt(jnp.mean(zq * zq, axis=-1, keepdims=True) + EPS) * gq_ref[...]
    cq_ref[...] = cq.astype(bf16)
    zkv = sect(P_CKV, KV_LORA)
    ckv = zkv * lax.rsqrt(jnp.mean(zkv * zkv, axis=-1, keepdims=True) + EPS) * gkv_ref[...]
    ckv_ref[...] = ckv.astype(bf16)
    zkr = sect(P_KR, LANES)
    kr = zkr * ck_ref[...] + pltpu.roll(zkr, LANES - MLA_ROPE, 1) * sk_ref[...]
    kr_ref[...] = kr.astype(bf16)
    g = sect(P_G, LANES) + bg_ref[...]
    tm = g.shape[0]
    lf = jax.nn.log_sigmoid(g)
    tri = jnp.where(lax.broadcasted_iota(jnp.int32, (tm, tm), 0)
                    >= lax.broadcasted_iota(jnp.int32, (tm, tm), 1), 1.0, 0.0).astype(bf16)
    hi = lf.astype(bf16)
    r1 = lf - hi.astype(f32)
    mid = r1.astype(bf16)
    lo = (r1 - mid.astype(f32)).astype(bf16)
    pre = _dot(tri, hi) + _dot(tri, mid) + _dot(tri, lo)
    suf = pre[tm - 1:tm, :] - pre + lf
    grp = lax.broadcasted_iota(jnp.int32, (1, LANES), 1) // GATE_STRIDE
    g2 = jnp.where(grp == 1, pre, jnp.where(grp == 3, suf, g))
    g_ref[...] = g2
    gt_ref[...] = g2.T[0:4 * GATE_STRIDE, :]
    u_ref[...] = sect(P_U, ML_INNER).astype(bf16)
    v_ref[...] = sect(P_V, ML_INNER).astype(bf16)
    o_ref[...] = sect(P_O, ML_INNER).astype(bf16)


def _in_projection(ctx, x, modr, w_in_p, g_q, g_kv, ck, sk, bg, n_ctx_tiles):
    b, s, d = x.shape
    tm = ROW_TILE
    t = s + n_ctx_tiles * tm
    nt = t // tm
    n_batch = b

    def mod_idx(bi, i):
        return (jnp.where(i < n_ctx_tiles, n_batch, bi), 0, 0)

    row = lambda w: pl.BlockSpec((None, tm, w), lambda bi, i: (bi, i, 0))
    const = lambda r, c: pl.BlockSpec((r, c), lambda bi, i: (0, 0))
    out_shapes = (
        jax.ShapeDtypeStruct((b, t, Q_LORA), bf16),
        jax.ShapeDtypeStruct((b, t, KV_LORA), bf16),
        jax.ShapeDtypeStruct((b, t, LANES), bf16),
        jax.ShapeDtypeStruct((b, t, LANES), f32),
        jax.ShapeDtypeStruct((b, nt, 4 * GATE_STRIDE, tm), f32),
        jax.ShapeDtypeStruct((b, t, ML_INNER), bf16),
        jax.ShapeDtypeStruct((b, t, ML_INNER), bf16),
        jax.ShapeDtypeStruct((b, s, ML_INNER), bf16),
    )
    out_specs = (
        row(Q_LORA), row(KV_LORA), row(LANES), row(LANES),
        pl.BlockSpec((None, None, 4 * GATE_STRIDE, tm), lambda bi, i: (bi, i, 0, 0)),
        row(ML_INNER), row(ML_INNER),
        pl.BlockSpec((None, tm, ML_INNER), lambda bi, i: (bi, jnp.maximum(i - n_ctx_tiles, 0), 0)),
    )
    return pl.pallas_call(
        functools.partial(_inproj_kernel, n_ctx_tiles=n_ctx_tiles),
        out_shape=out_shapes,
        grid=(b, nt),
        in_specs=[
            pl.BlockSpec((None, tm, d), lambda bi, i: (bi, jnp.minimum(i, n_ctx_tiles - 1), 0)),
            pl.BlockSpec((None, tm, d), lambda bi, i: (bi, jnp.maximum(i - n_ctx_tiles, 0), 0)),
            pl.BlockSpec((None, N_MOD, d), mod_idx),
            const(d, P_COLS),
            const(1, Q_LORA),
            const(1, KV_LORA),
            pl.BlockSpec((tm, LANES), lambda bi, i: (i, 0)),
            pl.BlockSpec((tm, LANES), lambda bi, i: (i, 0)),
            const(1, LANES),
        ],
        out_specs=out_specs,
        scratch_shapes=[pltpu.VMEM((tm, d), bf16)],
        compiler_params=_cparams(("arbitrary", "arbitrary")),
        name="in_projection",
    )(ctx, x, modr, w_in_p, g_q, g_kv, ck, sk, bg)


def _kv_kernel(ckv_ref, kr_ref, wk_ref, wvt_ref, k_ref, vt_ref, ksq_ref):
    ckv = ckv_ref[...]
    kn = _dot(ckv, wk_ref[...]).astype(bf16)
    kr = kr_ref[...]
    knf = kn.astype(f32)
    krf = kr.astype(f32)
    hsel = lax.broadcasted_iota(jnp.int32, (MLA_HEADS, MLA_HEADS * MLA_NOPE), 0)
    csel = lax.broadcasted_iota(jnp.int32, (MLA_HEADS, MLA_HEADS * MLA_NOPE), 1) // MLA_NOPE
    sel = jnp.where(hsel == csel, 1.0, 0.0).astype(bf16)
    ksq_ref[...] = (_dot_nt(sel, (knf * knf).astype(bf16))
                    + _dot_nt(jnp.ones((MLA_HEADS, LANES), bf16), (krf * krf).astype(bf16)))
    lane = lax.broadcasted_iota(jnp.int32, kr.shape, 1)
    stab = jnp.logical_and(lane >= MLA_ROPE, lane < MLA_ROPE + STAB_ROWS)
    kr_aug = jnp.where(stab, 1.0, krf).astype(bf16)
    for h in range(MLA_HEADS):
        k_ref[h, :, 0:MLA_NOPE] = kn[:, h * MLA_NOPE:(h + 1) * MLA_NOPE]
        k_ref[h, :, MLA_NOPE:QK_PAD] = kr_aug
        vt_ref[h] = _dot_nt(wvt_ref[h], ckv).astype(bf16)


def _kv_prep(ckv, kr, wk, wvt):
    b, t, _ = ckv.shape
    tm = ROW_TILE
    return pl.pallas_call(
        _kv_kernel,
        out_shape=(
            jax.ShapeDtypeStruct((b, MLA_HEADS, t, QK_PAD), bf16),
            jax.ShapeDtypeStruct((b, MLA_HEADS, MLA_V, t), bf16),
            jax.ShapeDtypeStruct((b, MLA_HEADS, t), f32),
        ),
        grid=(b, t // tm),
        in_specs=[
            pl.BlockSpec((None, tm, KV_LORA), lambda bi, i: (bi, i, 0)),
            pl.BlockSpec((None, tm, LANES), lambda bi, i: (bi, i, 0)),
            pl.BlockSpec((KV_LORA, MLA_HEADS * MLA_NOPE), lambda bi, i: (0, 0)),
            pl.BlockSpec((MLA_HEADS, MLA_V, KV_LORA), lambda bi, i: (0, 0, 0)),
        ],
        out_specs=(
            pl.BlockSpec((None, MLA_HEADS, tm, QK_PAD), lambda bi, i: (bi, 0, i, 0)),
            pl.BlockSpec((None, MLA_HEADS, MLA_V, tm), lambda bi, i: (bi, 0, 0, i)),
            pl.BlockSpec((None, MLA_HEADS, tm), lambda bi, i: (bi, 0, i)),
        ),
        compiler_params=_cparams(("arbitrary", "arbitrary")),
        name="kv_prep",
    )(ckv, kr, wk, wvt)


def _q_kernel(cq_ref, ksq_ref, wm_ref, ct_ref, st_ref, qt_ref, *, scale):
    cq = cq_ref[...]
    ct = ct_ref[...]
    st = st_ref[...]
    tm = cq.shape[0]
    kmax2 = jnp.max(ksq_ref[...], axis=1, keepdims=True)
    r0, r1 = MLA_NOPE, MLA_NOPE + MLA_ROPE
    for h in range(MLA_HEADS):
        qm = _dot_nt(wm_ref[h], cq)
        qn = (qm[0:r0] * scale).astype(bf16)
        qr = ((qm[r0:r1] * ct + qm[r1:QK_PAD] * st) * scale).astype(bf16)
        qnf = qn.astype(f32)
        qrf = qr.astype(f32)
        qsq = jnp.sum(qnf * qnf, axis=0, keepdims=True) + jnp.sum(qrf * qrf, axis=0, keepdims=True)
        bound = jnp.sqrt(qsq * kmax2[h:h + 1, :]) * STAB_SLACK
        qt_ref[h, 0:r0, :] = qn
        qt_ref[h, r0:r1, :] = qr
        qt_ref[h, r1:r1 + STAB_ROWS, :] = jnp.broadcast_to(bound * (-1.0 / STAB_ROWS), (STAB_ROWS, tm)).astype(bf16)
        qt_ref[h, r1 + STAB_ROWS:QK_PAD, :] = jnp.zeros((QK_PAD - r1 - STAB_ROWS, tm), bf16)


def _q_prep(cq, ksq, wm, ct, st, n_ctx_tiles):
    b, t, _ = cq.shape
    tm = ROW_TILE
    s = t - n_ctx_tiles * tm
    scale = float((MLA_NOPE + MLA_ROPE) ** -0.5 * 1.4426950408889634)
    return pl.pallas_call(
        functools.partial(_q_kernel, scale=scale),
        out_shape=jax.ShapeDtypeStruct((b, MLA_HEADS, QK_PAD, s), bf16),
        grid=(b, s // tm),
        in_specs=[
            pl.BlockSpec((None, tm, Q_LORA), lambda bi, i: (bi, i + n_ctx_tiles, 0)),
            pl.BlockSpec((None, MLA_HEADS, t), lambda bi, i: (bi, 0, 0)),
            pl.BlockSpec((MLA_HEADS, QK_PAD, Q_LORA), lambda bi, i: (0, 0, 0)),
            pl.BlockSpec((MLA_ROPE, tm), lambda bi, i: (0, i)),
            pl.BlockSpec((MLA_ROPE, tm), lambda bi, i: (0, i)),
        ],
        out_specs=pl.BlockSpec((None, MLA_HEADS, QK_PAD, tm), lambda bi, i: (bi, 0, 0, i)),
        compiler_params=_cparams(("arbitrary", "arbitrary")),
        name="q_prep",
    )(cq, ksq, wm, ct, st)


def _attn_bound_kernel(qt_ref, k_ref, vt_ref, o_ref, l_ref, *, tk, nk):
    qt = qt_ref[...]
    tq = qt.shape[1]
    l = jnp.zeros((1, tq), f32)
    acc = jnp.zeros((MLA_V, tq), f32)
    s = _dot(k_ref[0:tk, :], qt)
    for j in range(nk):
        s_next = _dot(k_ref[(j + 1) * tk:(j + 2) * tk, :], qt) if j + 1 < nk else None
        p = jnp.exp2(s)
        l = l + jnp.sum(p, axis=0, keepdims=True)
        acc = acc + _dot(vt_ref[:, j * tk:(j + 1) * tk], p.astype(bf16))
        s = s_next
    o_ref[...] = (acc / l).T.astype(o_ref.dtype)
    l_ref[...] = l


def _attn_online_kernel(qt_ref, k_ref, vt_ref, o_ref, *, tk, nk):
    qt = qt_ref[...]
    tq = qt.shape[1]
    m = jnp.full((1, tq), -jnp.inf, f32)
    l = jnp.zeros((1, tq), f32)
    acc = jnp.zeros((MLA_V, tq), f32)
    for j in range(nk):
        s = _dot(k_ref[j * tk:(j + 1) * tk, :], qt)
        m_new = jnp.maximum(m, jnp.max(s, axis=0, keepdims=True))
        alpha = jnp.exp2(m - m_new)
        p = jnp.exp2(s - m_new)
        l = alpha * l + jnp.sum(p, axis=0, keepdims=True)
        acc = alpha * acc + _dot(vt_ref[:, j * tk:(j + 1) * tk], p.astype(bf16))
        m = m_new
    o_ref[...] = (acc / l).T.astype(o_ref.dtype)


def _attention(qt, k, vt):
    b, hh, _, s = qt.shape
    t = k.shape[2]
    tq = min(ATT_TQ, s)
    tk = MXU_DIM
    for cand in range(MXU_DIM, ATT_TK_MAX + 1, MXU_DIM):
        if t % cand == 0:
            tk = cand
    assert s % tq == 0 and t % tk == 0
    in_specs = [
        pl.BlockSpec((None, None, QK_PAD, tq), lambda bi, h, i: (bi, h, 0, i)),
        pl.BlockSpec((None, None, t, QK_PAD), lambda bi, h, i: (bi, h, 0, 0)),
        pl.BlockSpec((None, None, MLA_V, t), lambda bi, h, i: (bi, h, 0, 0)),
    ]
    a_shape = jax.ShapeDtypeStruct((b, s, hh * MLA_V), bf16)
    a_spec = pl.BlockSpec((None, tq, MLA_V), lambda bi, h, i: (bi, i, h))
    sem = ("arbitrary", "arbitrary", "arbitrary")
    a, l = pl.pallas_call(
        functools.partial(_attn_bound_kernel, tk=tk, nk=t // tk),
        out_shape=(a_shape, jax.ShapeDtypeStruct((b, hh, 1, s), f32)),
        grid=(b, hh, s // tq),
        in_specs=in_specs,
        out_specs=(a_spec, pl.BlockSpec((None, None, 1, tq), lambda bi, h, i: (bi, h, 0, i))),
        compiler_params=_cparams(sem),
        name="attention_bound",
    )(qt, k, vt)

    def online(_):
        return pl.pallas_call(
            functools.partial(_attn_online_kernel, tk=tk, nk=t // tk),
            out_shape=a_shape,
            grid=(b, hh, s // tq),
            in_specs=in_specs,
            out_specs=a_spec,
            compiler_params=_cparams(sem),
            name="attention_online",
        )(qt, k, vt)

    ok = jnp.min(l) >= L_MIN
    return lax.cond(ok, lambda _: a, online, None)


def _mproj_kernel(u_ref, up_ref, un_ref, cw_ref, cb_ref, wq_ref, wkt_ref, q_ref, kt_ref, *,
                  n_ctx_tiles, n_tiles):
    i = pl.program_id(1)
    u = u_ref[...].astype(f32)
    tm = u.shape[0]
    first = jnp.logical_or(i == 0, i == n_ctx_tiles)
    last = jnp.logical_or(i == n_ctx_tiles - 1, i == n_tiles - 1)
    prev_row = jnp.where(first, 0.0, up_ref[HALO - 1:HALO, :].astype(f32))
    next_row = jnp.where(last, 0.0, un_ref[0:1, :].astype(f32))
    ridx = lax.broadcasted_iota(jnp.int32, (tm, 1), 0)
    u_prev = jnp.where(ridx == 0, prev_row, pltpu.roll(u, 1, 0))
    u_next = jnp.where(ridx == tm - 1, next_row, pltpu.roll(u, tm - 1, 0))
    y = cw_ref[0:1, :] * u_prev + cw_ref[1:2, :] * u + cw_ref[2:3, :] * u_next + cb_ref[...]
    act = _silu(y).astype(bf16)
    for h in range(ML_HEADS):
        a_h = act[:, h * ML_V:(h + 1) * ML_V]
        q_ref[:, h * ML_QK:(h + 1) * ML_QK] = (_dot(a_h, wq_ref[h]) * float(ML_QK ** -0.5)).astype(bf16)
        kt_ref[h * ML_QK:(h + 1) * ML_QK, :] = _dot_nt(wkt_ref[h], a_h).astype(bf16)


def _mlstm_projection(u, conv_w, conv_b, w_q, w_kt, n_ctx_tiles):
    b, t, _ = u.shape
    tm = ROW_TILE
    nt = t // tm
    hb = tm // HALO
    return pl.pallas_call(
        functools.partial(_mproj_kernel, n_ctx_tiles=n_ctx_tiles, n_tiles=nt),
        out_shape=(jax.ShapeDtypeStruct((b, t, ML_HEADS * ML_QK), bf16),
                   jax.ShapeDtypeStruct((b, nt, ML_HEADS * ML_QK, tm), bf16)),
        grid=(b, nt),
        in_specs=[
            pl.BlockSpec((None, tm, ML_INNER), lambda bi, i: (bi, i, 0)),
            pl.BlockSpec((None, HALO, ML_INNER), lambda bi, i: (bi, jnp.maximum(i * hb - 1, 0), 0)),
            pl.BlockSpec((None, HALO, ML_INNER),
                         lambda bi, i: (bi, jnp.minimum((i + 1) * hb, nt * hb - 1), 0)),
            pl.BlockSpec((3, ML_INNER), lambda bi, i: (0, 0)),
            pl.BlockSpec((1, ML_INNER), lambda bi, i: (0, 0)),
            pl.BlockSpec((ML_HEADS, ML_V, ML_QK), lambda bi, i: (0, 0, 0)),
            pl.BlockSpec((ML_HEADS, ML_QK, ML_V), lambda bi, i: (0, 0, 0)),
        ],
        out_specs=(pl.BlockSpec((None, tm, ML_HEADS * ML_QK), lambda bi, i: (bi, i, 0)),
                   pl.BlockSpec((None, None, ML_HEADS * ML_QK, tm), lambda bi, i: (bi, i, 0, 0))),
        compiler_params=_cparams(("arbitrary", "arbitrary")),
        name="mlstm_projection",
    )(u, u, u, conv_w, conv_b, w_q, w_kt)


def _scan_kernel(qf_ref, ktf_ref, vf_ref, gf_ref, gtf_ref, qb_ref, ktb_ref, vb_ref, gb_ref, gtb_ref,
                 hf_ref, hb_ref, c_scr, m_scr):
    j = pl.program_id(1)
    L = qf_ref.shape[0]
    GS = GATE_STRIDE

    rr = lax.broadcasted_iota(jnp.int32, (L, L), 0)
    cc = lax.broadcasted_iota(jnp.int32, (L, L), 1)
    visible = (rr >= cc, rr <= cc)
    dirs = ((qf_ref, ktf_ref, vf_ref, gf_ref, gtf_ref, hf_ref), (qb_ref, ktb_ref, vb_ref, gb_ref, gtb_ref, hb_ref))
    ones_col = jnp.where(lax.broadcasted_iota(jnp.int32, (L, LANES), 1) == 0, 1.0, 0.0).astype(bf16)

    chains = []
    for d, (q_ref, kt_ref, v_ref, g_ref, gt_ref, h_ref) in enumerate(dirs):
        g = g_ref[...]
        gt = gt_ref[...]
        for h in range(ML_HEADS):
            ci = 2 * d * GS + h
            cf = ci + GS
            bc_r = gt[cf:cf + 1, :]
            chains.append(dict(
                d=d, h=h, st=d * ML_HEADS + h, h_ref=h_ref,
                q=q_ref[:, h * ML_QK:(h + 1) * ML_QK],
                kt=kt_ref[h * ML_QK:(h + 1) * ML_QK, :],
                v=jnp.concatenate([v_ref[:, h * ML_V:(h + 1) * ML_V], ones_col], axis=1),
                bc_c=g[:, cf:cf + 1],
                ig_r=gt[ci:ci + 1, :], bc_r=bc_r,
                a=bc_r[:, L - 1:L] if d == 0 else bc_r[:, 0:1],
            ))

    def step(first):
        for c in chains:
            c["s"] = _dot(c["q"], c["kt"])
        for c in chains:
            if not first:
                c["c_st"] = c_scr[c["st"]]
                c["m_st"] = m_scr[c["st"]]
                c["qc"] = _dot(c["q"], c["c_st"].astype(bf16))
            else:
                c["m_st"] = 0.0

        for c in chains:
            drow = jnp.where(visible[c["d"]], c["ig_r"] - c["bc_r"], -jnp.inf)
            mm = jnp.maximum(c["m_st"], jnp.max(drow, axis=1, keepdims=True))
            c["w_inter"] = jnp.exp(c["m_st"] - mm)
            c["floor"] = jnp.exp(-(c["bc_c"] + mm))
            c["p"] = jnp.exp(drow - mm) * c["s"]

        for c in chains:
            nd = _dot(c["p"].astype(bf16), c["v"])
            if not first:
                nd = nd + c["w_inter"] * c["qc"]
            den = nd[:, ML_V:ML_V + 1]
            h = c["h"]
            hv = nd[:, 0:ML_V] / jnp.maximum(jnp.abs(den), c["floor"])
            c["h_ref"][:, h * ML_V:(h + 1) * ML_V] = hv.astype(c["h_ref"].dtype)

        for c in chains:
            a, m_st = c["a"], c["m_st"]
            g_log_r = a - c["bc_r"] + c["ig_r"]
            m_new = jnp.maximum(a + m_st, jnp.max(g_log_r, axis=1, keepdims=True))
            kwt = c["kt"].astype(f32) * jnp.exp(g_log_r - m_new)
            c_new = _dot(kwt.astype(bf16), c["v"])
            if not first:
                c_new = c_new + jnp.exp(a + m_st - m_new) * c["c_st"]
            c_scr[c["st"]] = c_new
            m_scr[c["st"]] = m_new

    @pl.when(j == 0)
    def _():
        step(True)

    @pl.when(j != 0)
    def _():
        step(False)


def _mlstm_scan(q, kt, v, g, gt):
    b, t, _ = q.shape
    L = CHUNK
    nc = t // L

    fwd = lambda j: j
    bwd = lambda j: jnp.where(j == 0, 0, nc - j)

    def specs(cidx):
        return [
            pl.BlockSpec((None, L, ML_HEADS * ML_QK), lambda bi, j: (bi, cidx(j), 0)),
            pl.BlockSpec((None, None, ML_HEADS * ML_QK, L), lambda bi, j: (bi, cidx(j), 0, 0)),
            pl.BlockSpec((None, L, ML_INNER), lambda bi, j: (bi, cidx(j), 0)),
            pl.BlockSpec((None, L, LANES), lambda bi, j: (bi, cidx(j), 0)),
            pl.BlockSpec((None, None, 4 * GATE_STRIDE, L), lambda bi, j: (bi, cidx(j), 0, 0)),
        ]

    n_chain = 2 * ML_HEADS
    h_shape = jax.ShapeDtypeStruct((b, t - L, ML_INNER), bf16)
    return pl.pallas_call(
        _scan_kernel,
        out_shape=(h_shape, h_shape),
        grid=(b, nc),
        in_specs=specs(fwd) + specs(bwd),
        out_specs=(pl.BlockSpec((None, L, ML_INNER), lambda bi, j: (bi, jnp.maximum(j - 1, 0), 0)),
                   pl.BlockSpec((None, L, ML_INNER), lambda bi, j: (bi, nc - 1 - jnp.maximum(j, 1), 0))),
        scratch_shapes=[
            pltpu.VMEM((n_chain, ML_QK, ML_V + LANES), f32),
            pltpu.VMEM((n_chain, 1, 1), f32),
        ],
        compiler_params=_cparams(("arbitrary", "arbitrary")),
        name="mlstm_scan",
    )(q, kt, v, g, gt, q, kt, v, g, gt)


def _outproj_kernel(a_ref, hf_ref, hb_ref, o_ref, x_ref, mod_ref, ghn_ref, w_ref, lg_ref, lb_ref,
                    out_ref):
    ghn = ghn_ref[...]
    gate = mod_ref[2:3, :]
    tm = x_ref.shape[0]
    halves = [slice(r, r + tm // 2) for r in (0, tm // 2)]
    lhs = []
    for rows in halves:
        hm = hf_ref[rows, :].astype(f32) + hb_ref[rows, :].astype(f32)
        og = jax.nn.sigmoid(o_ref[rows, :].astype(f32))
        parts = [a_ref[rows, :]]
        for h in range(ML_HEADS):
            sl = slice(h * ML_V, (h + 1) * ML_V)
            seg = hm[:, sl]
            mu = jnp.mean(seg, axis=-1, keepdims=True)
            dd = seg - mu
            var = jnp.mean(dd * dd, axis=-1, keepdims=True)
            parts.append((og[:, sl] * (dd * lax.rsqrt(var + EPS) * ghn[:, sl])).astype(bf16))
        lhs.append(jnp.concatenate(parts, axis=1))
    ys = [_dot(l, w_ref[...]) for l in lhs]
    for rows, y in zip(halves, ys):
        r = ALPHA * x_ref[rows, :] + gate * y
        out_ref[rows, :] = _layer_norm_rows(r, lg_ref[...], lb_ref[...])


def _out_projection(a, hf, hb, o, x, modr, g_hn, w_out, ln_g, ln_b, n_ctx_tiles):
    b, s, d = x.shape
    tm = min(OUT_TM, s)
    assert s % tm == 0
    mix = w_out.shape[0]
    return pl.pallas_call(
        _outproj_kernel,
        out_shape=jax.ShapeDtypeStruct((b, s, d), f32),
        grid=(b, s // tm),
        in_specs=[
            pl.BlockSpec((None, tm, MLA_HEADS * MLA_V), lambda bi, i: (bi, i, 0)),
            pl.BlockSpec((None, tm, ML_INNER), lambda bi, i: (bi, i, 0)),
            pl.BlockSpec((None, tm, ML_INNER), lambda bi, i: (bi, i, 0)),
            pl.BlockSpec((None, tm, ML_INNER), lambda bi, i: (bi, i, 0)),
            pl.BlockSpec((None, tm, d), lambda bi, i: (bi, i, 0)),
            pl.BlockSpec((None, N_MOD, d), lambda bi, i: (bi, 0, 0)),
            pl.BlockSpec((1, ML_INNER), lambda bi, i: (0, 0)),
            pl.BlockSpec((mix, d), lambda bi, i: (0, 0)),
            pl.BlockSpec((1, d), lambda bi, i: (0, 0)),
            pl.BlockSpec((1, d), lambda bi, i: (0, 0)),
        ],
        out_specs=pl.BlockSpec((None, tm, d), lambda bi, i: (bi, i, 0)),
        compiler_params=_cparams(("arbitrary", "arbitrary")),
        name="out_projection",
    )(a, hf, hb, o, x, modr, g_hn, w_out, ln_g, ln_b)


def _ffn_kernel(x_ref, xp_ref, xn_ref, mod_ref, wup_hbm, cw_ref, cb_ref, wd_hbm, lg_ref, lb_ref,
                out_ref, hs_ref, gs_ref, wg_buf, wu_buf, wd_buf, sem, *, n_row_tiles, nf, tf, nbuf):
    i = pl.program_id(1)
    step = pl.program_id(0) * n_row_tiles + i
    n_steps = pl.num_programs(0) * n_row_tiles
    tm = x_ref.shape[0]
    dff = nf * tf

    def weight_copies(j):
        slot = j % nbuf
        return (
            pltpu.make_async_copy(wup_hbm.at[:, pl.ds(j * tf, tf)], wg_buf.at[slot], sem.at[0, slot]),
            pltpu.make_async_copy(wup_hbm.at[:, pl.ds(dff + j * tf, tf)], wu_buf.at[slot], sem.at[1, slot]),
            pltpu.make_async_copy(wd_hbm.at[pl.ds(j * tf, tf), :], wd_buf.at[slot], sem.at[2, slot]),
        )

    @pl.when(step == 0)
    def _():
        for c in weight_copies(0):
            c.start()

    shift = mod_ref[3:4, :]
    scale = mod_ref[4:5, :]
    hs_ref[0:HALO, :] = (xp_ref[...] * (1.0 + scale) + shift).astype(bf16)
    hs_ref[HALO:HALO + tm, :] = (x_ref[...] * (1.0 + scale) + shift).astype(bf16)
    hs_ref[HALO + tm:2 * HALO + tm, :] = (xn_ref[...] * (1.0 + scale) + shift).astype(bf16)
    ridx = lax.broadcasted_iota(jnp.int32, (tm, 1), 0)
    first_row = jnp.logical_and(ridx == 0, i == 0)
    last_row = jnp.logical_and(ridx == tm - 1, i == n_row_tiles - 1)

    for j in range(nf):
        for c in weight_copies((j + 1) % nf):
            c.start()
        for c in weight_copies(j):
            c.wait()
        slot = j % nbuf
        gs = gs_ref.at[j % 2]
        cols = slice(j * tf, (j + 1) * tf)
        gs[...] = _dot(hs_ref[...], wg_buf[slot])
        g_prev = jnp.where(first_row, 0.0, gs[HALO - 1:HALO - 1 + tm, :])
        g_next = jnp.where(last_row, 0.0, gs[HALO + 1:HALO + 1 + tm, :])
        gate = (cw_ref[0:1, cols] * g_prev + cw_ref[1:2, cols] * gs[HALO:HALO + tm, :]
                + cw_ref[2:3, cols] * g_next + cb_ref[:, cols])
        up = _dot(hs_ref[HALO:HALO + tm, :], wu_buf[slot])
        act = (_silu(gate) * up).astype(bf16)
        contrib = _dot(act, wd_buf[slot])
        if j == 0:
            out_ref[...] = contrib
        else:
            out_ref[...] += contrib

    r = ALPHA * x_ref[...] + mod_ref[5:6, :] * out_ref[...]
    out_ref[...] = _layer_norm_rows(r, lg_ref[...], lb_ref[...])

    @pl.when(step == n_steps - 1)
    def _():
        for c in weight_copies(0):
            c.wait()


def _conv_ffn(x, modr, w_up, conv_w, conv_b, w_down, ln_g, ln_b):
    b, s, d = x.shape
    dff = w_down.shape[0]
    tm = min(FFN_TM, s)
    tf = min(FFN_TF, dff)
    assert s % tm == 0 and dff % tf == 0
    nr = s // tm
    nf = dff // tf
    assert nf >= 2
    nbuf = next(n for n in range(2, nf + 1) if (nf - 1) % n != 0)
    hb = tm // HALO
    return pl.pallas_call(
        functools.partial(_ffn_kernel, n_row_tiles=nr, nf=nf, tf=tf, nbuf=nbuf),
        out_shape=jax.ShapeDtypeStruct((b, s, d), f32),
        grid=(b, nr),
        in_specs=[
            pl.BlockSpec((None, tm, d), lambda bi, i: (bi, i, 0)),
            pl.BlockSpec((None, HALO, d), lambda bi, i: (bi, jnp.maximum(i * hb - 1, 0), 0)),
            pl.BlockSpec((None, HALO, d), lambda bi, i: (bi, jnp.minimum((i + 1) * hb, nr * hb - 1), 0)),
            pl.BlockSpec((None, N_MOD, d), lambda bi, i: (bi, 0, 0)),
            pl.BlockSpec(memory_space=pl.ANY),
            pl.BlockSpec((3, dff), lambda bi, i: (0, 0)),
            pl.BlockSpec((1, dff), lambda bi, i: (0, 0)),
            pl.BlockSpec(memory_space=pl.ANY),
            pl.BlockSpec((1, d), lambda bi, i: (0, 0)),
            pl.BlockSpec((1, d), lambda bi, i: (0, 0)),
        ],
        out_specs=pl.BlockSpec((None, tm, d), lambda bi, i: (bi, i, 0)),
        scratch_shapes=[
            pltpu.VMEM((tm + 2 * HALO, d), bf16),
            pltpu.VMEM((2, tm + 2 * HALO, tf), f32),
            pltpu.VMEM((nbuf, d, tf), bf16),
            pltpu.VMEM((nbuf, d, tf), bf16),
            pltpu.VMEM((nbuf, tf, d), bf16),
            pltpu.SemaphoreType.DMA((3, nbuf)),
        ],
        compiler_params=_cparams(("arbitrary", "arbitrary")),
        name="conv_ffn",
    )(x, x, x, modr, w_up, conv_w, conv_b, w_down, ln_g, ln_b)


def _rope_perm():
    q = MLA_ROPE // 4
    return jnp.concatenate([jnp.arange(q, 2 * q), jnp.arange(0, q),
                            jnp.arange(3 * q, 4 * q), jnp.arange(2 * q, 3 * q)])


def _rope_tables(s):
    n_rows = s // GRID_W
    row = jnp.repeat(jnp.arange(n_rows), GRID_W).astype(f32)
    col = jnp.tile(jnp.arange(GRID_W), n_rows).astype(f32)
    n_freq = MLA_ROPE // 4
    inv = ROPE_THETA ** (-jnp.arange(n_freq, dtype=f32) / n_freq)
    ar = row[:, None] * inv
    ac = col[:, None] * inv
    cos = jnp.concatenate([jnp.cos(ar), jnp.cos(ar), jnp.cos(ac), jnp.cos(ac)], axis=-1)
    sin = jnp.concatenate([-jnp.sin(ar), jnp.sin(ar), -jnp.sin(ac), jnp.sin(ac)], axis=-1)
    return cos, sin


def _pack_w_in(w_in):
    d = w_in.shape[0]
    z = lambda n: jnp.zeros((d, n), w_in.dtype)
    gates = []
    for gi in range(4):
        gates += [w_in[:, OFF_G + gi * ML_HEADS:OFF_G + (gi + 1) * ML_HEADS], z(GATE_STRIDE - ML_HEADS)]
    cols = [w_in[:, 0:OFF_U], w_in[:, OFF_KR:OFF_U][:, _rope_perm()], w_in[:, OFF_U:OFF_G],
            *gates, z(LANES - 4 * GATE_STRIDE)]
    return jnp.concatenate(cols, axis=1).astype(bf16)


def _pack_b_gate(b_gate):
    parts = []
    for gi in range(4):
        parts += [b_gate[gi * ML_HEADS:(gi + 1) * ML_HEADS], jnp.zeros((GATE_STRIDE - ML_HEADS,), f32)]
    parts.append(jnp.zeros((LANES - 4 * GATE_STRIDE,), f32))
    return jnp.concatenate(parts)[None, :]


def kernel(x, c, ctx, c_ctx, w_ada, b_ada, w_in, g_q, w_uq, g_kv, w_ukv, ml_conv_w, ml_conv_b, w_mq, w_mk, b_gate, g_hn, w_out, ln1_g, ln1_b, w_up, ffn_conv_w, ffn_conv_b, w_down, ln2_g, ln2_b):
    b, s, d = x.shape
    n_ctx = ctx.shape[1]
    assert n_ctx == CHUNK and s % ROW_TILE == 0
    n_ctx_tiles = n_ctx // ROW_TILE
    layer = 0

    rows = -(-(b + 1) // SUBLANES) * SUBLANES
    cc = jnp.concatenate([c, c_ctx[None, :], jnp.zeros((rows - b - 1, d), f32)], axis=0)
    modr = _modulation(cc, w_ada[layer], b_ada[layer][None, :]).reshape(rows, N_MOD, d)

    cos, sin = _rope_tables(s)
    pad = jnp.zeros((s, LANES - MLA_ROPE), f32)
    cos_p = jnp.concatenate([cos, pad], axis=1)
    sin_p = jnp.concatenate([sin, pad], axis=1)
    ctx_c = jnp.concatenate([jnp.ones((n_ctx, MLA_ROPE), f32), jnp.zeros((n_ctx, LANES - MLA_ROPE), f32)], axis=1)
    ck = jnp.concatenate([ctx_c, cos_p], axis=0)
    sk = jnp.concatenate([jnp.zeros((n_ctx, LANES), f32), sin_p], axis=0)
    ct = cos.T
    st = sin.T

    cq, ckv, kr, g, gt, u, v, o = _in_projection(
        ctx, x, modr, _pack_w_in(w_in[layer]), g_q[layer][None, :], g_kv[layer][None, :], ck, sk,
        _pack_b_gate(b_gate[layer]), n_ctx_tiles)

    dq = MLA_NOPE + MLA_ROPE
    wq = w_uq[layer].reshape(Q_LORA, MLA_HEADS, dq)
    wq_nope = jnp.transpose(wq[:, :, :MLA_NOPE], (1, 2, 0))
    wq_rope = jnp.transpose(wq[:, :, MLA_NOPE:], (1, 2, 0))
    wm = jnp.concatenate([wq_nope, wq_rope, wq_rope[:, _rope_perm(), :]], axis=1).astype(bf16)
    wkv = w_ukv[layer].reshape(KV_LORA, MLA_HEADS, MLA_NOPE + MLA_V)
    wk = wkv[:, :, :MLA_NOPE].reshape(KV_LORA, MLA_HEADS * MLA_NOPE).astype(bf16)
    wvt = jnp.transpose(wkv[:, :, MLA_NOPE:], (1, 2, 0)).astype(bf16)
    kk, vt, ksq = _kv_prep(ckv, kr, wk, wvt)
    qt = _q_prep(cq, ksq, wm, ct, st, n_ctx_tiles)
    a = _attention(qt, kk, vt)

    w_kt = jnp.transpose(w_mk[layer], (0, 2, 1)).astype(bf16)
    mq, mkt = _mlstm_projection(u, ml_conv_w[layer], ml_conv_b[layer][None, :], w_mq[layer].astype(bf16),
                                w_kt, n_ctx_tiles)
    hf, hb = _mlstm_scan(mq, mkt, v, g, gt)

    x1 = _out_projection(a, hf, hb, o, x, modr, g_hn[layer].reshape(1, ML_INNER), w_out[layer].astype(bf16),
                         ln1_g[layer][None, :], ln1_b[layer][None, :], n_ctx_tiles)
    return _conv_ffn(x1, modr, w_up[layer].astype(bf16), ffn_conv_w[layer], ffn_conv_b[layer][None, :],
                     w_down[layer].astype(bf16), ln2_g[layer][None, :], ln2_b[layer][None, :])
```

```python
import functools

import jax
import jax.numpy as jnp
from jax import lax
from jax.experimental import pallas as pl
from jax.experimental.pallas import tpu as pltpu

f32 = jnp.float32
bf16 = jnp.bfloat16

GRID_W = 64
MLA_HEADS = 8
MLA_NOPE = 128
MLA_ROPE = 64
MLA_V = 128
Q_LORA = 512
KV_LORA = 256
ROPE_THETA = 10000.0
ML_HEADS = 4
ML_QK = 128
ML_V = 256
ML_INNER = ML_HEADS * ML_V
N_GATES = 4 * ML_HEADS
DEPTH = 1
OFF_CKV = Q_LORA
OFF_KR = Q_LORA + KV_LORA
OFF_U = OFF_KR + MLA_ROPE
OFF_V = OFF_U + ML_INNER
OFF_O = OFF_V + ML_INNER
OFF_G = OFF_O + ML_INNER
ALPHA = (2.0 * DEPTH) ** 0.25
EPS = 1e-6
N_MOD = 6

LANES = 128
SUBLANES = 8
MXU_DIM = 256
VMEM_LIMIT = 56 * 1024 * 1024

QK_PAD = 2 * LANES
GATE_STRIDE = SUBLANES
P_Q = 0
P_CKV = P_Q + Q_LORA
P_KR = P_CKV + KV_LORA
P_U = P_KR + LANES
P_V = P_U + ML_INNER
P_O = P_V + ML_INNER
P_G = P_O + ML_INNER
P_COLS = P_G + LANES
STAB_ROWS = 16
STAB_SLACK = 1.02
L_MIN = 2.0 ** -100

ROW_TILE = 256
CHUNK = ROW_TILE
ATT_TQ = 1024
ATT_TK_MAX = 768
OUT_TM = 512
FFN_TM = 512
FFN_TF = 512
HALO = SUBLANES


def _cparams(sem):
    return pltpu.CompilerParams(dimension_semantics=sem, vmem_limit_bytes=VMEM_LIMIT)


def _silu(x):
    return x * jax.nn.sigmoid(x)


def _dot(a, b):
    return jnp.dot(a, b, preferred_element_type=f32)


def _dot_nt(a, b):
    return lax.dot_general(a, b, (((1,), (1,)), ((), ())), preferred_element_type=f32)


def _layer_norm_rows(r, g, b):
    mu = jnp.mean(r, axis=-1, keepdims=True)
    d = r - mu
    var = jnp.mean(d * d, axis=-1, keepdims=True)
    return d * lax.rsqrt(var + EPS) * g + b


def _mod_kernel(c_ref, w_ref, b_ref, o_ref):
    s = _silu(c_ref[...])
    o_ref[...] = _dot(s.astype(bf16), w_ref[...].astype(bf16)) + b_ref[...]


def _modulation(cc, w_ada, b_ada):
    rows, d = cc.shape
    n = w_ada.shape[1]
    tn = min(n, 1536)
    assert n % tn == 0
    return pl.pallas_call(
        _mod_kernel,
        out_shape=jax.ShapeDtypeStruct((rows, n), f32),
        grid=(n // tn,),
        in_specs=[
            pl.BlockSpec((rows, d), lambda j: (0, 0)),
            pl.BlockSpec((d, tn), lambda j: (0, j)),
            pl.BlockSpec((1, tn), lambda j: (0, j)),
        ],
        out_specs=pl.BlockSpec((rows, tn), lambda j: (0, j)),
        compiler_params=_cparams(("arbitrary",)),
        name="adaln_mod",
    )(cc, w_ada, b_ada)


def _inproj_kernel(ctx_ref, x_ref, mod_ref, w_ref, gq_ref, gkv_ref, ck_ref, sk_ref, bg_ref,
                   cq_ref, ckv_ref, kr_ref, g_ref, gt_ref, u_ref, v_ref, o_ref, h_ref, *, n_ctx_tiles):
    shift = mod_ref[0:1, :]
    scale = mod_ref[1:2, :]
    is_ctx = pl.program_id(1) < n_ctx_tiles
    rows = jnp.where(is_ctx, ctx_ref[...], x_ref[...])
    h_ref[...] = (rows * (1.0 + scale) + shift).astype(bf16)

    def sect(a, n):
        return _dot(h_ref[...], w_ref[:, a:a + n])

    zq = sect(P_Q, Q_LORA)
    cq = zq * lax.rsqrt(jnp.mean(zq * zq, axis=-1, keepdims=True) + EPS) * gq_ref[...]
    cq_ref[...] = cq.astype(bf16)
    zkv = sect(P_CKV, KV_LORA)
    ckv = zkv * lax.rsqrt(jnp.mean(zkv * zkv, axis=-1, keepdims=True) + EPS) * gkv_ref[...]
    ckv_ref[...] = ckv.astype(bf16)
    zkr = sect(P_KR, LANES)
    kr = zkr * ck_ref[...] + pltpu.roll(zkr, LANES - MLA_ROPE, 1) * sk_ref[...]
    kr_ref[...] = kr.astype(bf16)
    g = sect(P_G, LANES) + bg_ref[...]
    tm = g.shape[0]
    lf = jax.nn.log_sigmoid(g)
    tri = jnp.where(lax.broadcasted_iota(jnp.int32, (tm, tm), 0)
                    >= lax.broadcasted_iota(jnp.int32, (tm, tm), 1), 1.0, 0.0).astype(bf16)
    hi = lf.astype(bf16)
    r1 = lf - hi.astype(f32)
    mid = r1.astype(bf16)
    lo = (r1 - mid.astype(f32)).astype(bf16)
    pre = _dot(tri, hi) + _dot(tri, mid) + _dot(tri, lo)
    suf = pre[tm - 1:tm, :] - pre + lf
    grp = lax.broadcasted_iota(jnp.int32, (1, LANES), 1) // GATE_STRIDE
    g2 = jnp.where(grp == 1, pre, jnp.where(grp == 3, suf, g))
    g_ref[...] = g2
    gt_ref[...] = g2.T[0:4 * GATE_STRIDE, :]
    u_ref[...] = sect(P_U, ML_INNER).astype(bf16)
    v_ref[...] = sect(P_V, ML_INNER).astype(bf16)
    o_ref[...] = sect(P_O, ML_INNER).astype(bf16)


def _in_projection(ctx, x, modr, w_in_p, g_q, g_kv, ck, sk, bg, n_ctx_tiles):
    b, s, d = x.shape
    tm = ROW_TILE
    t = s + n_ctx_tiles * tm
    nt = t // tm
    n_batch = b

    def mod_idx(bi, i):
        return (jnp.where(i < n_ctx_tiles, n_batch, bi), 0, 0)

    row = lambda w: pl.BlockSpec((None, tm, w), lambda bi, i: (bi, i, 0))
    const = lambda r, c: pl.BlockSpec((r, c), lambda bi, i: (0, 0))
    out_shapes = (
        jax.ShapeDtypeStruct((b, t, Q_LORA), bf16),
        jax.ShapeDtypeStruct((b, t, KV_LORA), bf16),
        jax.ShapeDtypeStruct((b, t, LANES), bf16),
        jax.ShapeDtypeStruct((b, t, LANES), f32),
        jax.ShapeDtypeStruct((b, nt, 4 * GATE_STRIDE, tm), f32),
        jax.ShapeDtypeStruct((b, t, ML_INNER), bf16),
        jax.ShapeDtypeStruct((b, t, ML_INNER), bf16),
        jax.ShapeDtypeStruct((b, s, ML_INNER), bf16),
    )
    out_specs = (
        row(Q_LORA), row(KV_LORA), row(LANES), row(LANES),
        pl.BlockSpec((None, None, 4 * GATE_STRIDE, tm), lambda bi, i: (bi, i, 0, 0)),
        row(ML_INNER), row(ML_INNER),
        pl.BlockSpec((None, tm, ML_INNER), lambda bi, i: (bi, jnp.maximum(i - n_ctx_tiles, 0), 0)),
    )
    return pl.pallas_call(
        functools.partial(_inproj_kernel, n_ctx_tiles=n_ctx_tiles),
        out_shape=out_shapes,
        grid=(b, nt),
        in_specs=[
            pl.BlockSpec((None, tm, d), lambda bi, i: (bi, jnp.minimum(i, n_ctx_tiles - 1), 0)),
            pl.BlockSpec((None, tm, d), lambda bi, i: (bi, jnp.maximum(i - n_ctx_tiles, 0), 0)),
            pl.BlockSpec((None, N_MOD, d), mod_idx),
            const(d, P_COLS),
            const(1, Q_LORA),
            const(1, KV_LORA),
            pl.BlockSpec((tm, LANES), lambda bi, i: (i, 0)),
            pl.BlockSpec((tm, LANES), lambda bi, i: (i, 0)),
            const(1, LANES),
        ],
        out_specs=out_specs,
        scratch_shapes=[pltpu.VMEM((tm, d), bf16)],
        compiler_params=_cparams(("arbitrary", "arbitrary")),
        name="in_projection",
    )(ctx, x, modr, w_in_p, g_q, g_kv, ck, sk, bg)


def _kv_kernel(ckv_ref, kr_ref, wk_ref, wvt_ref, k_ref, vt_ref, ksq_ref):
    ckv = ckv_ref[...]
    kn = _dot(ckv, wk_ref[...]).astype(bf16)
    kr = kr_ref[...]
    knf = kn.astype(f32)
    krf = kr.astype(f32)
    hsel = lax.broadcasted_iota(jnp.int32, (MLA_HEADS, MLA_HEADS * MLA_NOPE), 0)
    csel = lax.broadcasted_iota(jnp.int32, (MLA_HEADS, MLA_HEADS * MLA_NOPE), 1) // MLA_NOPE
    sel = jnp.where(hsel == csel, 1.0, 0.0).astype(bf16)
    ksq_ref[...] = (_dot_nt(sel, (knf * knf).astype(bf16))
                    + _dot_nt(jnp.ones((MLA_HEADS, LANES), bf16), (krf * krf).astype(bf16)))
    lane = lax.broadcasted_iota(jnp.int32, kr.shape, 1)
    stab = jnp.logical_and(lane >= MLA_ROPE, lane < MLA_ROPE + STAB_ROWS)
    kr_aug = jnp.where(stab, 1.0, krf).astype(bf16)
    for h in range(MLA_HEADS):
        k_ref[h, :, 0:MLA_NOPE] = kn[:, h * MLA_NOPE:(h + 1) * MLA_NOPE]
        k_ref[h, :, MLA_NOPE:QK_PAD] = kr_aug
        vt_ref[h] = _dot_nt(wvt_ref[h], ckv).astype(bf16)


def _kv_prep(ckv, kr, wk, wvt):
    b, t, _ = ckv.shape
    tm = ROW_TILE
    return pl.pallas_call(
        _kv_kernel,
        out_shape=(
            jax.ShapeDtypeStruct((b, MLA_HEADS, t, QK_PAD), bf16),
            jax.ShapeDtypeStruct((b, MLA_HEADS, MLA_V, t), bf16),
            jax.ShapeDtypeStruct((b, MLA_HEADS, t), f32),
        ),
        grid=(b, t // tm),
        in_specs=[
            pl.BlockSpec((None, tm, KV_LORA), lambda bi, i: (bi, i, 0)),
            pl.BlockSpec((None, tm, LANES), lambda bi, i: (bi, i, 0)),
            pl.BlockSpec((KV_LORA, MLA_HEADS * MLA_NOPE), lambda bi, i: (0, 0)),
            pl.BlockSpec((MLA_HEADS, MLA_V, KV_LORA), lambda bi, i: (0, 0, 0)),
        ],
        out_specs=(
            pl.BlockSpec((None, MLA_HEADS, tm, QK_PAD), lambda bi, i: (bi, 0, i, 0)),
            pl.BlockSpec((None, MLA_HEADS, MLA_V, tm), lambda bi, i: (bi, 0, 0, i)),
            pl.BlockSpec((None, MLA_HEADS, tm), lambda bi, i: (bi, 0, i)),
        ),
        compiler_params=_cparams(("arbitrary", "arbitrary")),
        name="kv_prep",
    )(ckv, kr, wk, wvt)


def _q_kernel(cq_ref, ksq_ref, wm_ref, ct_ref, st_ref, qt_ref, *, scale):
    cq = cq_ref[...]
    ct = ct_ref[...]
    st = st_ref[...]
    tm = cq.shape[0]
    kmax2 = jnp.max(ksq_ref[...], axis=1, keepdims=True)
    r0, r1 = MLA_NOPE, MLA_NOPE + MLA_ROPE
    for h in range(MLA_HEADS):
        qm = _dot_nt(wm_ref[h], cq)
        qn = (qm[0:r0] * scale).astype(bf16)
        qr = ((qm[r0:r1] * ct + qm[r1:QK_PAD] * st) * scale).astype(bf16)
        qnf = qn.astype(f32)
        qrf = qr.astype(f32)
        qsq = jnp.sum(qnf * qnf, axis=0, keepdims=True) + jnp.sum(qrf * qrf, axis=0, keepdims=True)
        bound = jnp.sqrt(qsq * kmax2[h:h + 1, :]) * STAB_SLACK
        qt_ref[h, 0:r0, :] = qn
        qt_ref[h, r0:r1, :] = qr
        qt_ref[h, r1:r1 + STAB_ROWS, :] = jnp.broadcast_to(bound * (-1.0 / STAB_ROWS), (STAB_ROWS, tm)).astype(bf16)
        qt_ref[h, r1 + STAB_ROWS:QK_PAD, :] = jnp.zeros((QK_PAD - r1 - STAB_ROWS, tm), bf16)


def _q_prep(cq, ksq, wm, ct, st, n_ctx_tiles):
    b, t, _ = cq.shape
    tm = ROW_TILE
    s = t - n_ctx_tiles * tm
    scale = float((MLA_NOPE + MLA_ROPE) ** -0.5 * 1.4426950408889634)
    return pl.pallas_call(
        functools.partial(_q_kernel, scale=scale),
        out_shape=jax.ShapeDtypeStruct((b, MLA_HEADS, QK_PAD, s), bf16),
        grid=(b, s // tm),
        in_specs=[
            pl.BlockSpec((None, tm, Q_LORA), lambda bi, i: (bi, i + n_ctx_tiles, 0)),
            pl.BlockSpec((None, MLA_HEADS, t), lambda bi, i: (bi, 0, 0)),
            pl.BlockSpec((MLA_HEADS, QK_PAD, Q_LORA), lambda bi, i: (0, 0, 0)),
            pl.BlockSpec((MLA_ROPE, tm), lambda bi, i: (0, i)),
            pl.BlockSpec((MLA_ROPE, tm), lambda bi, i: (0, i)),
        ],
        out_specs=pl.BlockSpec((None, MLA_HEADS, QK_PAD, tm), lambda bi, i: (bi, 0, 0, i)),
        compiler_params=_cparams(("arbitrary", "arbitrary")),
        name="q_prep",
    )(cq, ksq, wm, ct, st)


def _attn_bound_kernel(qt_ref, k_ref, vt_ref, o_ref, l_ref, *, tq, tk, nk):
    def q_tile(i, carry):
        q0 = pl.multiple_of(i * tq, tq)
        qt = qt_ref[:, pl.ds(q0, tq)]
        l = jnp.zeros((1, tq), f32)
        acc = jnp.zeros((MLA_V, tq), f32)
        s = _dot(k_ref[0:tk, :], qt)
        for j in range(nk):
            s_next = _dot(k_ref[(j + 1) * tk:(j + 2) * tk, :], qt) if j + 1 < nk else None
            p = jnp.exp2(s)
            l = l + jnp.sum(p, axis=0, keepdims=True)
            acc = acc + _dot(vt_ref[:, j * tk:(j + 1) * tk], p.astype(bf16))
            s = s_next
        o_ref[pl.ds(q0, tq), :] = (acc / l).T.astype(o_ref.dtype)
        l_ref[:, pl.ds(q0, tq)] = l
        return carry

    lax.fori_loop(0, qt_ref.shape[1] // tq, q_tile, 0)


def _attn_online_kernel(qt_ref, k_ref, vt_ref, o_ref, *, tk, nk):
    qt = qt_ref[...]
    tq = qt.shape[1]
    m = jnp.full((1, tq), -jnp.inf, f32)
    l = jnp.zeros((1, tq), f32)
    acc = jnp.zeros((MLA_V, tq), f32)
    for j in range(nk):
        s = _dot(k_ref[j * tk:(j + 1) * tk, :], qt)
        m_new = jnp.maximum(m, jnp.max(s, axis=0, keepdims=True))
        alpha = jnp.exp2(m - m_new)
        p = jnp.exp2(s - m_new)
        l = alpha * l + jnp.sum(p, axis=0, keepdims=True)
        acc = alpha * acc + _dot(vt_ref[:, j * tk:(j + 1) * tk], p.astype(bf16))
        m = m_new
    o_ref[...] = (acc / l).T.astype(o_ref.dtype)


def _attention(qt, k, vt):
    b, hh, _, s = qt.shape
    t = k.shape[2]
    tq = min(ATT_TQ, s)
    tk = MXU_DIM
    for cand in range(MXU_DIM, ATT_TK_MAX + 1, MXU_DIM):
        if t % cand == 0:
            tk = cand
    assert s % tq == 0 and t % tk == 0
    in_specs = [
        pl.BlockSpec((None, None, QK_PAD, tq), lambda bi, h, i: (bi, h, 0, i)),
        pl.BlockSpec((None, None, t, QK_PAD), lambda bi, h, i: (bi, h, 0, 0)),
        pl.BlockSpec((None, None, MLA_V, t), lambda bi, h, i: (bi, h, 0, 0)),
    ]
    a_shape = jax.ShapeDtypeStruct((b, s, hh * MLA_V), bf16)
    a_spec = pl.BlockSpec((None, tq, MLA_V), lambda bi, h, i: (bi, i, h))
    sem = ("arbitrary", "arbitrary", "arbitrary")
    a, l = pl.pallas_call(
        functools.partial(_attn_bound_kernel, tq=tq, tk=tk, nk=t // tk),
        out_shape=(a_shape, jax.ShapeDtypeStruct((b, hh, 1, s), f32)),
        grid=(b, hh),
        in_specs=[
            pl.BlockSpec((None, None, QK_PAD, s), lambda bi, h: (bi, h, 0, 0)),
            pl.BlockSpec((None, None, t, QK_PAD), lambda bi, h: (bi, h, 0, 0)),
            pl.BlockSpec((None, None, MLA_V, t), lambda bi, h: (bi, h, 0, 0)),
        ],
        out_specs=(pl.BlockSpec((None, s, MLA_V), lambda bi, h: (bi, 0, h)),
                   pl.BlockSpec((None, None, 1, s), lambda bi, h: (bi, h, 0, 0))),
        compiler_params=_cparams(("arbitrary", "arbitrary")),
        name="attention_bound",
    )(qt, k, vt)

    def online(_):
        return pl.pallas_call(
            functools.partial(_attn_online_kernel, tk=tk, nk=t // tk),
            out_shape=a_shape,
            grid=(b, hh, s // tq),
            in_specs=in_specs,
            out_specs=a_spec,
            compiler_params=_cparams(sem),
            name="attention_online",
        )(qt, k, vt)

    ok = jnp.min(l) >= L_MIN
    return lax.cond(ok, lambda _: a, online, None)


def _mproj_kernel(u_ref, up_ref, un_ref, cw_ref, cb_ref, wq_ref, wkt_ref, q_ref, kt_ref, *,
                  n_ctx_tiles, n_tiles):
    i = pl.program_id(1)
    u = u_ref[...].astype(f32)
    tm = u.shape[0]
    first = jnp.logical_or(i == 0, i == n_ctx_tiles)
    last = jnp.logical_or(i == n_ctx_tiles - 1, i == n_tiles - 1)
    prev_row = jnp.where(first, 0.0, up_ref[HALO - 1:HALO, :].astype(f32))
    next_row = jnp.where(last, 0.0, un_ref[0:1, :].astype(f32))
    ridx = lax.broadcasted_iota(jnp.int32, (tm, 1), 0)
    u_prev = jnp.where(ridx == 0, prev_row, pltpu.roll(u, 1, 0))
    u_next = jnp.where(ridx == tm - 1, next_row, pltpu.roll(u, tm - 1, 0))
    y = cw_ref[0:1, :] * u_prev + cw_ref[1:2, :] * u + cw_ref[2:3, :] * u_next + cb_ref[...]
    act = _silu(y).astype(bf16)
    for h in range(ML_HEADS):
        a_h = act[:, h * ML_V:(h + 1) * ML_V]
        q_ref[:, h * ML_QK:(h + 1) * ML_QK] = (_dot(a_h, wq_ref[h]) * float(ML_QK ** -0.5)).astype(bf16)
        kt_ref[h * ML_QK:(h + 1) * ML_QK, :] = _dot_nt(wkt_ref[h], a_h).astype(bf16)


def _mlstm_projection(u, conv_w, conv_b, w_q, w_kt, n_ctx_tiles):
    b, t, _ = u.shape
    tm = ROW_TILE
    nt = t // tm
    hb = tm // HALO
    return pl.pallas_call(
        functools.partial(_mproj_kernel, n_ctx_tiles=n_ctx_tiles, n_tiles=nt),
        out_shape=(jax.ShapeDtypeStruct((b, t, ML_HEADS * ML_QK), bf16),
                   jax.ShapeDtypeStruct((b, nt, ML_HEADS * ML_QK, tm), bf16)),
        grid=(b, nt),
        in_specs=[
            pl.BlockSpec((None, tm, ML_INNER), lambda bi, i: (bi, i, 0)),
            pl.BlockSpec((None, HALO, ML_INNER), lambda bi, i: (bi, jnp.maximum(i * hb - 1, 0), 0)),
            pl.BlockSpec((None, HALO, ML_INNER),
                         lambda bi, i: (bi, jnp.minimum((i + 1) * hb, nt * hb - 1), 0)),
            pl.BlockSpec((3, ML_INNER), lambda bi, i: (0, 0)),
            pl.BlockSpec((1, ML_INNER), lambda bi, i: (0, 0)),
            pl.BlockSpec((ML_HEADS, ML_V, ML_QK), lambda bi, i: (0, 0, 0)),
            pl.BlockSpec((ML_HEADS, ML_QK, ML_V), lambda bi, i: (0, 0, 0)),
        ],
        out_specs=(pl.BlockSpec((None, tm, ML_HEADS * ML_QK), lambda bi, i: (bi, i, 0)),
                   pl.BlockSpec((None, None, ML_HEADS * ML_QK, tm), lambda bi, i: (bi, i, 0, 0))),
        compiler_params=_cparams(("arbitrary", "arbitrary")),
        name="mlstm_projection",
    )(u, u, u, conv_w, conv_b, w_q, w_kt)


def _scan_kernel(qf_ref, ktf_ref, vf_ref, gf_ref, gtf_ref, qb_ref, ktb_ref, vb_ref, gb_ref, gtb_ref,
                 hf_ref, hb_ref, c_scr, m_scr):
    j = pl.program_id(1)
    L = qf_ref.shape[0]
    GS = GATE_STRIDE

    rr = lax.broadcasted_iota(jnp.int32, (L, L), 0)
    cc = lax.broadcasted_iota(jnp.int32, (L, L), 1)
    visible = (rr >= cc, rr <= cc)
    dirs = ((qf_ref, ktf_ref, vf_ref, gf_ref, gtf_ref, hf_ref), (qb_ref, ktb_ref, vb_ref, gb_ref, gtb_ref, hb_ref))
    ones_col = jnp.where(lax.broadcasted_iota(jnp.int32, (L, LANES), 1) == 0, 1.0, 0.0).astype(bf16)

    chains = []
    for d, (q_ref, kt_ref, v_ref, g_ref, gt_ref, h_ref) in enumerate(dirs):
        g = g_ref[...]
        gt = gt_ref[...]
        for h in range(ML_HEADS):
            ci = 2 * d * GS + h
            cf = ci + GS
            bc_r = gt[cf:cf + 1, :]
            chains.append(dict(
                d=d, h=h, st=d * ML_HEADS + h, h_ref=h_ref,
                q=q_ref[:, h * ML_QK:(h + 1) * ML_QK],
                kt=kt_ref[h * ML_QK:(h + 1) * ML_QK, :],
                v=jnp.concatenate([v_ref[:, h * ML_V:(h + 1) * ML_V], ones_col], axis=1),
                bc_c=g[:, cf:cf + 1],
                ig_r=gt[ci:ci + 1, :], bc_r=bc_r,
                a=bc_r[:, L - 1:L] if d == 0 else bc_r[:, 0:1],
            ))

    def step(first):
        for c in chains:
            c["s"] = _dot(c["q"], c["kt"])
        for c in chains:
            if not first:
                c["c_st"] = c_scr[c["st"]]
                c["m_st"] = m_scr[c["st"]]
                c["qc"] = _dot(c["q"], c["c_st"].astype(bf16))
            else:
                c["m_st"] = 0.0

        for c in chains:
            drow = jnp.where(visible[c["d"]], c["ig_r"] - c["bc_r"], -jnp.inf)
            mm = jnp.maximum(c["m_st"], jnp.max(drow, axis=1, keepdims=True))
            c["w_inter"] = jnp.exp(c["m_st"] - mm)
            c["floor"] = jnp.exp(-(c["bc_c"] + mm))
            c["p"] = jnp.exp(drow - mm) * c["s"]

        for c in chains:
            nd = _dot(c["p"].astype(bf16), c["v"])
            if not first:
                nd = nd + c["w_inter"] * c["qc"]
            den = nd[:, ML_V:ML_V + 1]
            h = c["h"]
            hv = nd[:, 0:ML_V] / jnp.maximum(jnp.abs(den), c["floor"])
            c["h_ref"][:, h * ML_V:(h + 1) * ML_V] = hv.astype(c["h_ref"].dtype)

        for c in chains:
            a, m_st = c["a"], c["m_st"]
            g_log_r = a - c["bc_r"] + c["ig_r"]
            m_new = jnp.maximum(a + m_st, jnp.max(g_log_r, axis=1, keepdims=True))
            kwt = c["kt"].astype(f32) * jnp.exp(g_log_r - m_new)
            c_new = _dot(kwt.astype(bf16), c["v"])
            if not first:
                c_new = c_new + jnp.exp(a + m_st - m_new) * c["c_st"]
            c_scr[c["st"]] = c_new
            m_scr[c["st"]] = m_new

    @pl.when(j == 0)
    def _():
        step(True)

    @pl.when(j != 0)
    def _():
        step(False)


def _mlstm_scan(q, kt, v, g, gt):
    b, t, _ = q.shape
    L = CHUNK
    nc = t // L

    fwd = lambda j: j
    bwd = lambda j: jnp.where(j == 0, 0, nc - j)

    def specs(cidx):
        return [
            pl.BlockSpec((None, L, ML_HEADS * ML_QK), lambda bi, j: (bi, cidx(j), 0)),
            pl.BlockSpec((None, None, ML_HEADS * ML_QK, L), lambda bi, j: (bi, cidx(j), 0, 0)),
            pl.BlockSpec((None, L, ML_INNER), lambda bi, j: (bi, cidx(j), 0)),
            pl.BlockSpec((None, L, LANES), lambda bi, j: (bi, cidx(j), 0)),
            pl.BlockSpec((None, None, 4 * GATE_STRIDE, L), lambda bi, j: (bi, cidx(j), 0, 0)),
        ]

    n_chain = 2 * ML_HEADS
    h_shape = jax.ShapeDtypeStruct((b, t - L, ML_INNER), bf16)
    return pl.pallas_call(
        _scan_kernel,
        out_shape=(h_shape, h_shape),
        grid=(b, nc),
        in_specs=specs(fwd) + specs(bwd),
        out_specs=(pl.BlockSpec((None, L, ML_INNER), lambda bi, j: (bi, jnp.maximum(j - 1, 0), 0)),
                   pl.BlockSpec((None, L, ML_INNER), lambda bi, j: (bi, nc - 1 - jnp.maximum(j, 1), 0))),
        scratch_shapes=[
            pltpu.VMEM((n_chain, ML_QK, ML_V + LANES), f32),
            pltpu.VMEM((n_chain, 1, 1), f32),
        ],
        compiler_params=_cparams(("arbitrary", "arbitrary")),
        name="mlstm_scan",
    )(q, kt, v, g, gt, q, kt, v, g, gt)


def _outproj_kernel(a_ref, hf_ref, hb_ref, o_ref, x_ref, mod_ref, ghn_ref, w_ref, lg_ref, lb_ref,
                    out_ref):
    ghn = ghn_ref[...]
    gate = mod_ref[2:3, :]
    tm = x_ref.shape[0]
    halves = [slice(r, r + tm // 2) for r in (0, tm // 2)]
    lhs = []
    for rows in halves:
        hm = hf_ref[rows, :].astype(f32) + hb_ref[rows, :].astype(f32)
        og = jax.nn.sigmoid(o_ref[rows, :].astype(f32))
        parts = [a_ref[rows, :]]
        for h in range(ML_HEADS):
            sl = slice(h * ML_V, (h + 1) * ML_V)
            seg = hm[:, sl]
            mu = jnp.mean(seg, axis=-1, keepdims=True)
            dd = seg - mu
            var = jnp.mean(dd * dd, axis=-1, keepdims=True)
            parts.append((og[:, sl] * (dd * lax.rsqrt(var + EPS) * ghn[:, sl])).astype(bf16))
        lhs.append(jnp.concatenate(parts, axis=1))
    ys = [_dot(l, w_ref[...]) for l in lhs]
    for rows, y in zip(halves, ys):
        r = ALPHA * x_ref[rows, :] + gate * y
        out_ref[rows, :] = _layer_norm_rows(r, lg_ref[...], lb_ref[...])


def _out_projection(a, hf, hb, o, x, modr, g_hn, w_out, ln_g, ln_b, n_ctx_tiles):
    b, s, d = x.shape
    tm = min(OUT_TM, s)
    assert s % tm == 0
    mix = w_out.shape[0]
    return pl.pallas_call(
        _outproj_kernel,
        out_shape=jax.ShapeDtypeStruct((b, s, d), f32),
        grid=(b, s // tm),
        in_specs=[
            pl.BlockSpec((None, tm, MLA_HEADS * MLA_V), lambda bi, i: (bi, i, 0)),
            pl.BlockSpec((None, tm, ML_INNER), lambda bi, i: (bi, i, 0)),
            pl.BlockSpec((None, tm, ML_INNER), lambda bi, i: (bi, i, 0)),
            pl.BlockSpec((None, tm, ML_INNER), lambda bi, i: (bi, i, 0)),
            pl.BlockSpec((None, tm, d), lambda bi, i: (bi, i, 0)),
            pl.BlockSpec((None, N_MOD, d), lambda bi, i: (bi, 0, 0)),
            pl.BlockSpec((1, ML_INNER), lambda bi, i: (0, 0)),
            pl.BlockSpec((mix, d), lambda bi, i: (0, 0)),
            pl.BlockSpec((1, d), lambda bi, i: (0, 0)),
            pl.BlockSpec((1, d), lambda bi, i: (0, 0)),
        ],
        out_specs=pl.BlockSpec((None, tm, d), lambda bi, i: (bi, i, 0)),
        compiler_params=_cparams(("arbitrary", "arbitrary")),
        name="out_projection",
    )(a, hf, hb, o, x, modr, g_hn, w_out, ln_g, ln_b)


def _ffn_kernel(x_ref, xp_ref, xn_ref, mod_ref, wup_hbm, cw_ref, cb_ref, wd_hbm, lg_ref, lb_ref,
                out_ref, hs_ref, gs_ref, wg_buf, wu_buf, wd_buf, sem, *, n_row_tiles, nf, tf, nbuf):
    i = pl.program_id(1)
    step = pl.program_id(0) * n_row_tiles + i
    n_steps = pl.num_programs(0) * n_row_tiles
    tm = x_ref.shape[0]
    dff = nf * tf

    def weight_copies(j):
        slot = j % nbuf
        return (
            pltpu.make_async_copy(wup_hbm.at[:, pl.ds(j * tf, tf)], wg_buf.at[slot], sem.at[0, slot]),
            pltpu.make_async_copy(wup_hbm.at[:, pl.ds(dff + j * tf, tf)], wu_buf.at[slot], sem.at[1, slot]),
            pltpu.make_async_copy(wd_hbm.at[pl.ds(j * tf, tf), :], wd_buf.at[slot], sem.at[2, slot]),
        )

    @pl.when(step == 0)
    def _():
        for c in weight_copies(0):
            c.start()

    shift = mod_ref[3:4, :]
    scale = mod_ref[4:5, :]
    hs_ref[0:HALO, :] = (xp_ref[...] * (1.0 + scale) + shift).astype(bf16)
    hs_ref[HALO:HALO + tm, :] = (x_ref[...] * (1.0 + scale) + shift).astype(bf16)
    hs_ref[HALO + tm:2 * HALO + tm, :] = (xn_ref[...] * (1.0 + scale) + shift).astype(bf16)
    ridx = lax.broadcasted_iota(jnp.int32, (tm, 1), 0)
    first_row = jnp.logical_and(ridx == 0, i == 0)
    last_row = jnp.logical_and(ridx == tm - 1, i == n_row_tiles - 1)

    for j in range(nf):
        for c in weight_copies((j + 1) % nf):
            c.start()
        for c in weight_copies(j):
            c.wait()
        slot = j % nbuf
        gs = gs_ref.at[j % 2]
        cols = slice(j * tf, (j + 1) * tf)
        gs[...] = _dot(hs_ref[...], wg_buf[slot])
        g_prev = jnp.where(first_row, 0.0, gs[HALO - 1:HALO - 1 + tm, :])
        g_next = jnp.where(last_row, 0.0, gs[HALO + 1:HALO + 1 + tm, :])
        gate = (cw_ref[0:1, cols] * g_prev + cw_ref[1:2, cols] * gs[HALO:HALO + tm, :]
                + cw_ref[2:3, cols] * g_next + cb_ref[:, cols])
        up = _dot(hs_ref[HALO:HALO + tm, :], wu_buf[slot])
        act = (_silu(gate) * up).astype(bf16)
        contrib = _dot(act, wd_buf[slot])
        if j == 0:
            out_ref[...] = contrib
        else:
            out_ref[...] += contrib

    r = ALPHA * x_ref[...] + mod_ref[5:6, :] * out_ref[...]
    out_ref[...] = _layer_norm_rows(r, lg_ref[...], lb_ref[...])

    @pl.when(step == n_steps - 1)
    def _():
        for c in weight_copies(0):
            c.wait()


def _conv_ffn(x, modr, w_up, conv_w, conv_b, w_down, ln_g, ln_b):
    b, s, d = x.shape
    dff = w_down.shape[0]
    tm = min(FFN_TM, s)
    tf = min(FFN_TF, dff)
    assert s % tm == 0 and dff % tf == 0
    nr = s // tm
    nf = dff // tf
    assert nf >= 2
    nbuf = next(n for n in range(2, nf + 1) if (nf - 1) % n != 0)
    hb = tm // HALO
    return pl.pallas_call(
        functools.partial(_ffn_kernel, n_row_tiles=nr, nf=nf, tf=tf, nbuf=nbuf),
        out_shape=jax.ShapeDtypeStruct((b, s, d), f32),
        grid=(b, nr),
        in_specs=[
            pl.BlockSpec((None, tm, d), lambda bi, i: (bi, i, 0)),
            pl.BlockSpec((None, HALO, d), lambda bi, i: (bi, jnp.maximum(i * hb - 1, 0), 0)),
            pl.BlockSpec((None, HALO, d), lambda bi, i: (bi, jnp.minimum((i + 1) * hb, nr * hb - 1), 0)),
            pl.BlockSpec((None, N_MOD, d), lambda bi, i: (bi, 0, 0)),
            pl.BlockSpec(memory_space=pl.ANY),
            pl.BlockSpec((3, dff), lambda bi, i: (0, 0)),
            pl.BlockSpec((1, dff), lambda bi, i: (0, 0)),
            pl.BlockSpec(memory_space=pl.ANY),
            pl.BlockSpec((1, d), lambda bi, i: (0, 0)),
            pl.BlockSpec((1, d), lambda bi, i: (0, 0)),
        ],
        out_specs=pl.BlockSpec((None, tm, d), lambda bi, i: (bi, i, 0)),
        scratch_shapes=[
            pltpu.VMEM((tm + 2 * HALO, d), bf16),
            pltpu.VMEM((2, tm + 2 * HALO, tf), f32),
            pltpu.VMEM((nbuf, d, tf), bf16),
            pltpu.VMEM((nbuf, d, tf), bf16),
            pltpu.VMEM((nbuf, tf, d), bf16),
            pltpu.SemaphoreType.DMA((3, nbuf)),
        ],
        compiler_params=_cparams(("arbitrary", "arbitrary")),
        name="conv_ffn",
    )(x, x, x, modr, w_up, conv_w, conv_b, w_down, ln_g, ln_b)


def _rope_perm():
    q = MLA_ROPE // 4
    return jnp.concatenate([jnp.arange(q, 2 * q), jnp.arange(0, q),
                            jnp.arange(3 * q, 4 * q), jnp.arange(2 * q, 3 * q)])


def _rope_tables(s):
    n_rows = s // GRID_W
    row = jnp.repeat(jnp.arange(n_rows), GRID_W).astype(f32)
    col = jnp.tile(jnp.arange(GRID_W), n_rows).astype(f32)
    n_freq = MLA_ROPE // 4
    inv = ROPE_THETA ** (-jnp.arange(n_freq, dtype=f32) / n_freq)
    ar = row[:, None] * inv
    ac = col[:, None] * inv
    cos = jnp.concatenate([jnp.cos(ar), jnp.cos(ar), jnp.cos(ac), jnp.cos(ac)], axis=-1)
    sin = jnp.concatenate([-jnp.sin(ar), jnp.sin(ar), -jnp.sin(ac), jnp.sin(ac)], axis=-1)
    return cos, sin


def _pack_kernel(w_ref, krp_ref, g_ref, o_ref):
    o_ref[:, 0:OFF_U] = w_ref[:, 0:OFF_U].astype(bf16)
    o_ref[:, OFF_U:P_U] = krp_ref[...].astype(bf16)
    o_ref[:, P_U:P_G] = w_ref[:, OFF_U:OFF_G].astype(bf16)
    o_ref[:, P_G:P_COLS] = g_ref[...].astype(bf16)


def _pack_w_in(w_in):
    d = w_in.shape[0]
    z = lambda n: jnp.zeros((d, n), w_in.dtype)
    gates = []
    for gi in range(4):
        gates += [w_in[:, OFF_G + gi * ML_HEADS:OFF_G + (gi + 1) * ML_HEADS], z(GATE_STRIDE - ML_HEADS)]
    gates = jnp.concatenate(gates + [z(LANES - 4 * GATE_STRIDE)], axis=1)
    krp = w_in[:, OFF_KR:OFF_U][:, _rope_perm()]
    tr = min(ROW_TILE, d)
    assert d % tr == 0
    return pl.pallas_call(
        _pack_kernel,
        out_shape=jax.ShapeDtypeStruct((d, P_COLS), bf16),
        grid=(d // tr,),
        in_specs=[
            pl.BlockSpec((tr, w_in.shape[1]), lambda i: (i, 0)),
            pl.BlockSpec((tr, MLA_ROPE), lambda i: (i, 0)),
            pl.BlockSpec((tr, LANES), lambda i: (i, 0)),
        ],
        out_specs=pl.BlockSpec((tr, P_COLS), lambda i: (i, 0)),
        compiler_params=_cparams(("arbitrary",)),
        name="pack_w_in",
    )(w_in, krp, gates)


def _pack_b_gate(b_gate):
    parts = []
    for gi in range(4):
        parts += [b_gate[gi * ML_HEADS:(gi + 1) * ML_HEADS], jnp.zeros((GATE_STRIDE - ML_HEADS,), f32)]
    parts.append(jnp.zeros((LANES - 4 * GATE_STRIDE,), f32))
    return jnp.concatenate(parts)[None, :]


def kernel(x, c, ctx, c_ctx, w_ada, b_ada, w_in, g_q, w_uq, g_kv, w_ukv, ml_conv_w, ml_conv_b, w_mq, w_mk, b_gate, g_hn, w_out, ln1_g, ln1_b, w_up, ffn_conv_w, ffn_conv_b, w_down, ln2_g, ln2_b):
    b, s, d = x.shape
    n_ctx = ctx.shape[1]
    assert n_ctx == CHUNK and s % ROW_TILE == 0
    n_ctx_tiles = n_ctx // ROW_TILE
    layer = 0

    rows = -(-(b + 1) // SUBLANES) * SUBLANES
    cc = jnp.concatenate([c, c_ctx[None, :], jnp.zeros((rows - b - 1, d), f32)], axis=0)
    modr = _modulation(cc, w_ada[layer], b_ada[layer][None, :]).reshape(rows, N_MOD, d)

    cos, sin = _rope_tables(s)
    pad = jnp.zeros((s, LANES - MLA_ROPE), f32)
    cos_p = jnp.concatenate([cos, pad], axis=1)
    sin_p = jnp.concatenate([sin, pad], axis=1)
    ctx_c = jnp.concatenate([jnp.ones((n_ctx, MLA_ROPE), f32), jnp.zeros((n_ctx, LANES - MLA_ROPE), f32)], axis=1)
    ck = jnp.concatenate([ctx_c, cos_p], axis=0)
    sk = jnp.concatenate([jnp.zeros((n_ctx, LANES), f32), sin_p], axis=0)
    ct = cos.T
    st = sin.T

    cq, ckv, kr, g, gt, u, v, o = _in_projection(
        ctx, x, modr, _pack_w_in(w_in[layer]), g_q[layer][None, :], g_kv[layer][None, :], ck, sk,
        _pack_b_gate(b_gate[layer]), n_ctx_tiles)

    dq = MLA_NOPE + MLA_ROPE
    wq = w_uq[layer].reshape(Q_LORA, MLA_HEADS, dq)
    wq_nope = jnp.transpose(wq[:, :, :MLA_NOPE], (1, 2, 0))
    wq_rope = jnp.transpose(wq[:, :, MLA_NOPE:], (1, 2, 0))
    wm = jnp.concatenate([wq_nope, wq_rope, wq_rope[:, _rope_perm(), :]], axis=1).astype(bf16)
    wkv = w_ukv[layer].reshape(KV_LORA, MLA_HEADS, MLA_NOPE + MLA_V)
    wk = wkv[:, :, :MLA_NOPE].reshape(KV_LORA, MLA_HEADS * MLA_NOPE).astype(bf16)
    wvt = jnp.transpose(wkv[:, :, MLA_NOPE:], (1, 2, 0)).astype(bf16)
    kk, vt, ksq = _kv_prep(ckv, kr, wk, wvt)
    qt = _q_prep(cq, ksq, wm, ct, st, n_ctx_tiles)
    a = _attention(qt, kk, vt)

    w_kt = jnp.transpose(w_mk[layer], (0, 2, 1)).astype(bf16)
    mq, mkt = _mlstm_projection(u, ml_conv_w[layer], ml_conv_b[layer][None, :], w_mq[layer].astype(bf16),
                                w_kt, n_ctx_tiles)
    hf, hb = _mlstm_scan(mq, mkt, v, g, gt)

    x1 = _out_projection(a, hf, hb, o, x, modr, g_hn[layer].reshape(1, ML_INNER), w_out[layer].astype(bf16),
                         ln1_g[layer][None, :], ln1_b[layer][None, :], n_ctx_tiles)
    return _conv_ffn(x1, modr, w_up[layer].astype(bf16), ffn_conv_w[layer], ffn_conv_b[layer][None, :],
                     w_down[layer].astype(bf16), ln2_g[layer][None, :], ln2_b[layer][None, :])
```

```python
import functools

import numpy as np
import jax
import jax.numpy as jnp
from jax import lax
from jax.experimental import pallas as pl
from jax.experimental.pallas import tpu as pltpu

f32 = jnp.float32
bf16 = jnp.bfloat16

GRID_W = 64
MLA_HEADS = 8
MLA_NOPE = 128
MLA_ROPE = 64
MLA_V = 128
Q_LORA = 512
KV_LORA = 256
ROPE_THETA = 10000.0
ML_HEADS = 4
ML_QK = 128
ML_V = 256
ML_INNER = ML_HEADS * ML_V
N_GATES = 4 * ML_HEADS
DEPTH = 1
OFF_CKV = Q_LORA
OFF_KR = Q_LORA + KV_LORA
OFF_U = OFF_KR + MLA_ROPE
OFF_V = OFF_U + ML_INNER
OFF_O = OFF_V + ML_INNER
OFF_G = OFF_O + ML_INNER
ALPHA = (2.0 * DEPTH) ** 0.25
EPS = 1e-6
N_MOD = 6

LANES = 128
SUBLANES = 8
MXU_DIM = 256
VMEM_LIMIT = 56 * 1024 * 1024

QK_PAD = 2 * LANES
GATE_STRIDE = SUBLANES
P_Q = 0
P_CKV = P_Q + Q_LORA
P_KR = P_CKV + KV_LORA
P_U = P_KR + LANES
P_V = P_U + ML_INNER
P_O = P_V + ML_INNER
P_G = P_O + ML_INNER
P_COLS = P_G + LANES
STAB_ROWS = 16
STAB_SLACK = 1.02
L_MIN = 2.0 ** -100

ROW_TILE = 256
CHUNK = ROW_TILE
ATT_TQ = 1024
ATT_TK_MAX = 768
OUT_TM = 512
FFN_TM = 512
FFN_TF = 512
HALO = SUBLANES


def _cparams(sem):
    return pltpu.CompilerParams(dimension_semantics=sem, vmem_limit_bytes=VMEM_LIMIT)


def _silu(x):
    return x * jax.nn.sigmoid(x)


def _dot(a, b):
    return jnp.dot(a, b, preferred_element_type=f32)


def _dot_nt(a, b):
    return lax.dot_general(a, b, (((1,), (1,)), ((), ())), preferred_element_type=f32)


def _layer_norm_rows(r, g, b):
    mu = jnp.mean(r, axis=-1, keepdims=True)
    d = r - mu
    var = jnp.mean(d * d, axis=-1, keepdims=True)
    return d * lax.rsqrt(var + EPS) * g + b


def _mod_kernel(c_ref, w_ref, b_ref, o_ref):
    s = _silu(c_ref[...])
    o_ref[...] = _dot(s.astype(bf16), w_ref[...].astype(bf16)) + b_ref[...]


def _modulation(cc, w_ada, b_ada):
    rows, d = cc.shape
    n = w_ada.shape[1]
    tn = min(n, 1536)
    assert n % tn == 0
    return pl.pallas_call(
        _mod_kernel,
        out_shape=jax.ShapeDtypeStruct((rows, n), f32),
        grid=(n // tn,),
        in_specs=[
            pl.BlockSpec((rows, d), lambda j: (0, 0)),
            pl.BlockSpec((d, tn), lambda j: (0, j)),
            pl.BlockSpec((1, tn), lambda j: (0, j)),
        ],
        out_specs=pl.BlockSpec((rows, tn), lambda j: (0, j)),
        compiler_params=_cparams(("arbitrary",)),
        name="adaln_mod",
    )(cc, w_ada, b_ada)


def _inproj_kernel(ctx_ref, x_ref, mod_ref, w_ref, gq_ref, gkv_ref, ck_ref, sk_ref, bg_ref,
                   cq_ref, ckv_ref, kr_ref, g_ref, gt_ref, u_ref, v_ref, o_ref, h_ref, *, n_ctx_tiles):
    shift = mod_ref[0:1, :]
    scale = mod_ref[1:2, :]
    is_ctx = pl.program_id(1) < n_ctx_tiles
    rows = jnp.where(is_ctx, ctx_ref[...], x_ref[...])
    h_ref[...] = (rows * (1.0 + scale) + shift).astype(bf16)

    def sect(a, n):
        return _dot(h_ref[...], w_ref[:, a:a + n])

    zq = sect(P_Q, Q_LORA)
    cq = zq * lax.rsqrt(jnp.mean(zq * zq, axis=-1, keepdims=True) + EPS) * gq_ref[...]
    cq_ref[...] = cq.astype(bf16)
    zkv = sect(P_CKV, KV_LORA)
    ckv = zkv * lax.rsqrt(jnp.mean(zkv * zkv, axis=-1, keepdims=True) + EPS) * gkv_ref[...]
    ckv_ref[...] = ckv.astype(bf16)
    zkr = sect(P_KR, LANES)
    kr = zkr * ck_ref[...] + pltpu.roll(zkr, LANES - MLA_ROPE, 1) * sk_ref[...]
    kr_ref[...] = kr.astype(bf16)
    g = sect(P_G, LANES) + bg_ref[...]
    tm = g.shape[0]
    lf = jax.nn.log_sigmoid(g)
    tri = jnp.where(lax.broadcasted_iota(jnp.int32, (tm, tm), 0)
                    >= lax.broadcasted_iota(jnp.int32, (tm, tm), 1), 1.0, 0.0).astype(bf16)
    hi = lf.astype(bf16)
    r1 = lf - hi.astype(f32)
    mid = r1.astype(bf16)
    lo = (r1 - mid.astype(f32)).astype(bf16)
    pre = _dot(tri, hi) + _dot(tri, mid) + _dot(tri, lo)
    suf = pre[tm - 1:tm, :] - pre + lf
    grp = lax.broadcasted_iota(jnp.int32, (1, LANES), 1) // GATE_STRIDE
    g2 = jnp.where(grp == 1, pre, jnp.where(grp == 3, suf, g))
    g_ref[...] = g2
    gt_ref[...] = g2.T[0:4 * GATE_STRIDE, :]
    u_ref[...] = sect(P_U, ML_INNER).astype(bf16)
    v_ref[...] = sect(P_V, ML_INNER).astype(bf16)
    o_ref[...] = sect(P_O, ML_INNER).astype(bf16)


def _in_projection(ctx, x, modr, w_in_p, g_q, g_kv, ck, sk, bg, n_ctx_tiles):
    b, s, d = x.shape
    tm = ROW_TILE
    t = s + n_ctx_tiles * tm
    nt = t // tm
    n_batch = b

    def mod_idx(bi, i):
        return (jnp.where(i < n_ctx_tiles, n_batch, bi), 0, 0)

    row = lambda w: pl.BlockSpec((None, tm, w), lambda bi, i: (bi, i, 0))
    const = lambda r, c: pl.BlockSpec((r, c), lambda bi, i: (0, 0))
    out_shapes = (
        jax.ShapeDtypeStruct((b, t, Q_LORA), bf16),
        jax.ShapeDtypeStruct((b, t, KV_LORA), bf16),
        jax.ShapeDtypeStruct((b, t, LANES), bf16),
        jax.ShapeDtypeStruct((b, t, LANES), f32),
        jax.ShapeDtypeStruct((b, nt, 4 * GATE_STRIDE, tm), f32),
        jax.ShapeDtypeStruct((b, t, ML_INNER), bf16),
        jax.ShapeDtypeStruct((b, t, ML_INNER), bf16),
        jax.ShapeDtypeStruct((b, s, ML_INNER), bf16),
    )
    out_specs = (
        row(Q_LORA), row(KV_LORA), row(LANES), row(LANES),
        pl.BlockSpec((None, None, 4 * GATE_STRIDE, tm), lambda bi, i: (bi, i, 0, 0)),
        row(ML_INNER), row(ML_INNER),
        pl.BlockSpec((None, tm, ML_INNER), lambda bi, i: (bi, jnp.maximum(i - n_ctx_tiles, 0), 0)),
    )
    return pl.pallas_call(
        functools.partial(_inproj_kernel, n_ctx_tiles=n_ctx_tiles),
        out_shape=out_shapes,
        grid=(b, nt),
        in_specs=[
            pl.BlockSpec((None, tm, d), lambda bi, i: (bi, jnp.minimum(i, n_ctx_tiles - 1), 0)),
            pl.BlockSpec((None, tm, d), lambda bi, i: (bi, jnp.maximum(i - n_ctx_tiles, 0), 0)),
            pl.BlockSpec((None, N_MOD, d), mod_idx),
            const(d, P_COLS),
            const(1, Q_LORA),
            const(1, KV_LORA),
            pl.BlockSpec((tm, LANES), lambda bi, i: (i, 0)),
            pl.BlockSpec((tm, LANES), lambda bi, i: (i, 0)),
            const(1, LANES),
        ],
        out_specs=out_specs,
        scratch_shapes=[pltpu.VMEM((tm, d), bf16)],
        compiler_params=_cparams(("arbitrary", "arbitrary")),
        name="in_projection",
    )(ctx, x, modr, w_in_p, g_q, g_kv, ck, sk, bg)


def _kv_kernel(ckv_ref, kr_ref, wk_ref, wvt_ref, k_ref, vt_ref, ksq_ref):
    ckv = ckv_ref[...]
    kn = _dot(ckv, wk_ref[...]).astype(bf16)
    kr = kr_ref[...]
    knf = kn.astype(f32)
    krf = kr.astype(f32)
    hsel = lax.broadcasted_iota(jnp.int32, (MLA_HEADS, MLA_HEADS * MLA_NOPE), 0)
    csel = lax.broadcasted_iota(jnp.int32, (MLA_HEADS, MLA_HEADS * MLA_NOPE), 1) // MLA_NOPE
    sel = jnp.where(hsel == csel, 1.0, 0.0).astype(bf16)
    ksq_ref[...] = (_dot_nt(sel, (knf * knf).astype(bf16))
                    + _dot_nt(jnp.ones((MLA_HEADS, LANES), bf16), (krf * krf).astype(bf16)))
    lane = lax.broadcasted_iota(jnp.int32, kr.shape, 1)
    stab = jnp.logical_and(lane >= MLA_ROPE, lane < MLA_ROPE + STAB_ROWS)
    kr_aug = jnp.where(stab, 1.0, krf).astype(bf16)
    for h in range(MLA_HEADS):
        k_ref[h, :, 0:MLA_NOPE] = kn[:, h * MLA_NOPE:(h + 1) * MLA_NOPE]
        k_ref[h, :, MLA_NOPE:QK_PAD] = kr_aug
        vt_ref[h] = _dot_nt(wvt_ref[h], ckv).astype(bf16)


def _kv_prep(ckv, kr, wk, wvt):
    b, t, _ = ckv.shape
    tm = ROW_TILE
    return pl.pallas_call(
        _kv_kernel,
        out_shape=(
            jax.ShapeDtypeStruct((b, MLA_HEADS, t, QK_PAD), bf16),
            jax.ShapeDtypeStruct((b, MLA_HEADS, MLA_V, t), bf16),
            jax.ShapeDtypeStruct((b, MLA_HEADS, t), f32),
        ),
        grid=(b, t // tm),
        in_specs=[
            pl.BlockSpec((None, tm, KV_LORA), lambda bi, i: (bi, i, 0)),
            pl.BlockSpec((None, tm, LANES), lambda bi, i: (bi, i, 0)),
            pl.BlockSpec((KV_LORA, MLA_HEADS * MLA_NOPE), lambda bi, i: (0, 0)),
            pl.BlockSpec((MLA_HEADS, MLA_V, KV_LORA), lambda bi, i: (0, 0, 0)),
        ],
        out_specs=(
            pl.BlockSpec((None, MLA_HEADS, tm, QK_PAD), lambda bi, i: (bi, 0, i, 0)),
            pl.BlockSpec((None, MLA_HEADS, MLA_V, tm), lambda bi, i: (bi, 0, 0, i)),
            pl.BlockSpec((None, MLA_HEADS, tm), lambda bi, i: (bi, 0, i)),
        ),
        compiler_params=_cparams(("arbitrary", "arbitrary")),
        name="kv_prep",
    )(ckv, kr, wk, wvt)


def _q_kernel(cq_ref, ksq_ref, wm_ref, ct_ref, st_ref, qt_ref, *, scale):
    cq = cq_ref[...]
    ct = ct_ref[...]
    st = st_ref[...]
    tm = cq.shape[0]
    kmax2 = jnp.max(ksq_ref[...], axis=1, keepdims=True)
    r0, r1 = MLA_NOPE, MLA_NOPE + MLA_ROPE
    for h in range(MLA_HEADS):
        qm = _dot_nt(wm_ref[h], cq)
        qn = (qm[0:r0] * scale).astype(bf16)
        qr = ((qm[r0:r1] * ct + qm[r1:QK_PAD] * st) * scale).astype(bf16)
        qnf = qn.astype(f32)
        qrf = qr.astype(f32)
        qsq = jnp.sum(qnf * qnf, axis=0, keepdims=True) + jnp.sum(qrf * qrf, axis=0, keepdims=True)
        bound = jnp.sqrt(qsq * kmax2[h:h + 1, :]) * STAB_SLACK
        qt_ref[h, 0:r0, :] = qn
        qt_ref[h, r0:r1, :] = qr
        qt_ref[h, r1:r1 + STAB_ROWS, :] = jnp.broadcast_to(bound * (-1.0 / STAB_ROWS), (STAB_ROWS, tm)).astype(bf16)
        qt_ref[h, r1 + STAB_ROWS:QK_PAD, :] = jnp.zeros((QK_PAD - r1 - STAB_ROWS, tm), bf16)


def _q_prep(cq, ksq, wm, ct, st, n_ctx_tiles):
    b, t, _ = cq.shape
    tm = ROW_TILE
    s = t - n_ctx_tiles * tm
    scale = float((MLA_NOPE + MLA_ROPE) ** -0.5 * 1.4426950408889634)
    return pl.pallas_call(
        functools.partial(_q_kernel, scale=scale),
        out_shape=jax.ShapeDtypeStruct((b, MLA_HEADS, QK_PAD, s), bf16),
        grid=(b, s // tm),
        in_specs=[
            pl.BlockSpec((None, tm, Q_LORA), lambda bi, i: (bi, i + n_ctx_tiles, 0)),
            pl.BlockSpec((None, MLA_HEADS, t), lambda bi, i: (bi, 0, 0)),
            pl.BlockSpec((MLA_HEADS, QK_PAD, Q_LORA), lambda bi, i: (0, 0, 0)),
            pl.BlockSpec((MLA_ROPE, tm), lambda bi, i: (0, i)),
            pl.BlockSpec((MLA_ROPE, tm), lambda bi, i: (0, i)),
        ],
        out_specs=pl.BlockSpec((None, MLA_HEADS, QK_PAD, tm), lambda bi, i: (bi, 0, 0, i)),
        compiler_params=_cparams(("arbitrary", "arbitrary")),
        name="q_prep",
    )(cq, ksq, wm, ct, st)


def _attn_bound_kernel(qt_ref, k_ref, vt_ref, o_ref, l_ref, *, tq, tk, nk):
    def q_tile(i, carry):
        q0 = pl.multiple_of(i * tq, tq)
        qt = qt_ref[:, pl.ds(q0, tq)]
        l = jnp.zeros((1, tq), f32)
        acc = jnp.zeros((MLA_V, tq), f32)
        s = _dot(k_ref[0:tk, :], qt)
        for j in range(nk):
            s_next = _dot(k_ref[(j + 1) * tk:(j + 2) * tk, :], qt) if j + 1 < nk else None
            p = jnp.exp2(s)
            l = l + jnp.sum(p, axis=0, keepdims=True)
            acc = acc + _dot(vt_ref[:, j * tk:(j + 1) * tk], p.astype(bf16))
            s = s_next
        o_ref[pl.ds(q0, tq), :] = (acc / l).T.astype(o_ref.dtype)
        l_ref[:, pl.ds(q0, tq)] = l
        return carry

    lax.fori_loop(0, qt_ref.shape[1] // tq, q_tile, 0)


def _attn_online_kernel(qt_ref, k_ref, vt_ref, o_ref, *, tk, nk):
    qt = qt_ref[...]
    tq = qt.shape[1]
    m = jnp.full((1, tq), -jnp.inf, f32)
    l = jnp.zeros((1, tq), f32)
    acc = jnp.zeros((MLA_V, tq), f32)
    for j in range(nk):
        s = _dot(k_ref[j * tk:(j + 1) * tk, :], qt)
        m_new = jnp.maximum(m, jnp.max(s, axis=0, keepdims=True))
        alpha = jnp.exp2(m - m_new)
        p = jnp.exp2(s - m_new)
        l = alpha * l + jnp.sum(p, axis=0, keepdims=True)
        acc = alpha * acc + _dot(vt_ref[:, j * tk:(j + 1) * tk], p.astype(bf16))
        m = m_new
    o_ref[...] = (acc / l).T.astype(o_ref.dtype)


def _attention(qt, k, vt):
    b, hh, _, s = qt.shape
    t = k.shape[2]
    tq = min(ATT_TQ, s)
    tk = MXU_DIM
    for cand in range(MXU_DIM, ATT_TK_MAX + 1, MXU_DIM):
        if t % cand == 0:
            tk = cand
    assert s % tq == 0 and t % tk == 0
    in_specs = [
        pl.BlockSpec((None, None, QK_PAD, tq), lambda bi, h, i: (bi, h, 0, i)),
        pl.BlockSpec((None, None, t, QK_PAD), lambda bi, h, i: (bi, h, 0, 0)),
        pl.BlockSpec((None, None, MLA_V, t), lambda bi, h, i: (bi, h, 0, 0)),
    ]
    a_shape = jax.ShapeDtypeStruct((b, s, hh * MLA_V), bf16)
    a_spec = pl.BlockSpec((None, tq, MLA_V), lambda bi, h, i: (bi, i, h))
    sem = ("arbitrary", "arbitrary", "arbitrary")
    a, l = pl.pallas_call(
        functools.partial(_attn_bound_kernel, tq=tq, tk=tk, nk=t // tk),
        out_shape=(a_shape, jax.ShapeDtypeStruct((b, hh, 1, s), f32)),
        grid=(b, hh),
        in_specs=[
            pl.BlockSpec((None, None, QK_PAD, s), lambda bi, h: (bi, h, 0, 0)),
            pl.BlockSpec((None, None, t, QK_PAD), lambda bi, h: (bi, h, 0, 0)),
            pl.BlockSpec((None, None, MLA_V, t), lambda bi, h: (bi, h, 0, 0)),
        ],
        out_specs=(pl.BlockSpec((None, s, MLA_V), lambda bi, h: (bi, 0, h)),
                   pl.BlockSpec((None, None, 1, s), lambda bi, h: (bi, h, 0, 0))),
        compiler_params=_cparams(("arbitrary", "arbitrary")),
        name="attention_bound",
    )(qt, k, vt)

    def online(_):
        return pl.pallas_call(
            functools.partial(_attn_online_kernel, tk=tk, nk=t // tk),
            out_shape=a_shape,
            grid=(b, hh, s // tq),
            in_specs=in_specs,
            out_specs=a_spec,
            compiler_params=_cparams(sem),
            name="attention_online",
        )(qt, k, vt)

    ok = jnp.min(l) >= L_MIN
    return lax.cond(ok, lambda _: a, online, None)


def _mproj_kernel(u_ref, up_ref, un_ref, cw_ref, cb_ref, wq_ref, wkt_ref, q_ref, kt_ref, *,
                  n_ctx_tiles, n_tiles):
    i = pl.program_id(1)
    u = u_ref[...].astype(f32)
    tm = u.shape[0]
    first = jnp.logical_or(i == 0, i == n_ctx_tiles)
    last = jnp.logical_or(i == n_ctx_tiles - 1, i == n_tiles - 1)
    prev_row = jnp.where(first, 0.0, up_ref[HALO - 1:HALO, :].astype(f32))
    next_row = jnp.where(last, 0.0, un_ref[0:1, :].astype(f32))
    ridx = lax.broadcasted_iota(jnp.int32, (tm, 1), 0)
    u_prev = jnp.where(ridx == 0, prev_row, pltpu.roll(u, 1, 0))
    u_next = jnp.where(ridx == tm - 1, next_row, pltpu.roll(u, tm - 1, 0))
    y = cw_ref[0:1, :] * u_prev + cw_ref[1:2, :] * u + cw_ref[2:3, :] * u_next + cb_ref[...]
    act = _silu(y).astype(bf16)
    for h in range(ML_HEADS):
        a_h = act[:, h * ML_V:(h + 1) * ML_V]
        q_ref[:, h * ML_QK:(h + 1) * ML_QK] = (_dot(a_h, wq_ref[h]) * float(ML_QK ** -0.5)).astype(bf16)
        kt_ref[h * ML_QK:(h + 1) * ML_QK, :] = _dot_nt(wkt_ref[h], a_h).astype(bf16)


def _mlstm_projection(u, conv_w, conv_b, w_q, w_kt, n_ctx_tiles):
    b, t, _ = u.shape
    tm = ROW_TILE
    nt = t // tm
    hb = tm // HALO
    return pl.pallas_call(
        functools.partial(_mproj_kernel, n_ctx_tiles=n_ctx_tiles, n_tiles=nt),
        out_shape=(jax.ShapeDtypeStruct((b, t, ML_HEADS * ML_QK), bf16),
                   jax.ShapeDtypeStruct((b, nt, ML_HEADS * ML_QK, tm), bf16)),
        grid=(b, nt),
        in_specs=[
            pl.BlockSpec((None, tm, ML_INNER), lambda bi, i: (bi, i, 0)),
            pl.BlockSpec((None, HALO, ML_INNER), lambda bi, i: (bi, jnp.maximum(i * hb - 1, 0), 0)),
            pl.BlockSpec((None, HALO, ML_INNER),
                         lambda bi, i: (bi, jnp.minimum((i + 1) * hb, nt * hb - 1), 0)),
            pl.BlockSpec((3, ML_INNER), lambda bi, i: (0, 0)),
            pl.BlockSpec((1, ML_INNER), lambda bi, i: (0, 0)),
            pl.BlockSpec((ML_HEADS, ML_V, ML_QK), lambda bi, i: (0, 0, 0)),
            pl.BlockSpec((ML_HEADS, ML_QK, ML_V), lambda bi, i: (0, 0, 0)),
        ],
        out_specs=(pl.BlockSpec((None, tm, ML_HEADS * ML_QK), lambda bi, i: (bi, i, 0)),
                   pl.BlockSpec((None, None, ML_HEADS * ML_QK, tm), lambda bi, i: (bi, i, 0, 0))),
        compiler_params=_cparams(("arbitrary", "arbitrary")),
        name="mlstm_projection",
    )(u, u, u, conv_w, conv_b, w_q, w_kt)


def _scan_kernel(qf_ref, ktf_ref, vf_ref, gf_ref, gtf_ref, qb_ref, ktb_ref, vb_ref, gb_ref, gtb_ref,
                 hf_ref, hb_ref, c_scr, m_scr):
    j = pl.program_id(1)
    L = qf_ref.shape[0]
    GS = GATE_STRIDE

    rr = lax.broadcasted_iota(jnp.int32, (L, L), 0)
    cc = lax.broadcasted_iota(jnp.int32, (L, L), 1)
    visible = (rr >= cc, rr <= cc)
    dirs = ((qf_ref, ktf_ref, vf_ref, gf_ref, gtf_ref, hf_ref), (qb_ref, ktb_ref, vb_ref, gb_ref, gtb_ref, hb_ref))
    ones_col = jnp.where(lax.broadcasted_iota(jnp.int32, (L, LANES), 1) == 0, 1.0, 0.0).astype(bf16)

    chains = []
    for d, (q_ref, kt_ref, v_ref, g_ref, gt_ref, h_ref) in enumerate(dirs):
        g = g_ref[...]
        gt = gt_ref[...]
        for h in range(ML_HEADS):
            ci = 2 * d * GS + h
            cf = ci + GS
            bc_r = gt[cf:cf + 1, :]
            chains.append(dict(
                d=d, h=h, st=d * ML_HEADS + h, h_ref=h_ref,
                q=q_ref[:, h * ML_QK:(h + 1) * ML_QK],
                kt=kt_ref[h * ML_QK:(h + 1) * ML_QK, :],
                v=jnp.concatenate([v_ref[:, h * ML_V:(h + 1) * ML_V], ones_col], axis=1),
                bc_c=g[:, cf:cf + 1],
                ig_r=gt[ci:ci + 1, :], bc_r=bc_r,
                a=bc_r[:, L - 1:L] if d == 0 else bc_r[:, 0:1],
            ))

    def step(first):
        for c in chains:
            c["s"] = _dot(c["q"], c["kt"])
        for c in chains:
            if not first:
                c["c_st"] = c_scr[c["st"]]
                c["m_st"] = m_scr[c["st"]]
                c["qc"] = _dot(c["q"], c["c_st"].astype(bf16))
            else:
                c["m_st"] = 0.0

        for c in chains:
            drow = jnp.where(visible[c["d"]], c["ig_r"] - c["bc_r"], -jnp.inf)
            mm = jnp.maximum(c["m_st"], jnp.max(drow, axis=1, keepdims=True))
            c["w_inter"] = jnp.exp(c["m_st"] - mm)
            c["floor"] = jnp.exp(-(c["bc_c"] + mm))
            c["p"] = jnp.exp(drow - mm) * c["s"]

        for c in chains:
            nd = _dot(c["p"].astype(bf16), c["v"])
            if not first:
                nd = nd + c["w_inter"] * c["qc"]
            den = nd[:, ML_V:ML_V + 1]
            h = c["h"]
            hv = nd[:, 0:ML_V] / jnp.maximum(jnp.abs(den), c["floor"])
            c["h_ref"][:, h * ML_V:(h + 1) * ML_V] = hv.astype(c["h_ref"].dtype)

        for c in chains:
            a, m_st = c["a"], c["m_st"]
            g_log_r = a - c["bc_r"] + c["ig_r"]
            m_new = jnp.maximum(a + m_st, jnp.max(g_log_r, axis=1, keepdims=True))
            kwt = c["kt"].astype(f32) * jnp.exp(g_log_r - m_new)
            c_new = _dot(kwt.astype(bf16), c["v"])
            if not first:
                c_new = c_new + jnp.exp(a + m_st - m_new) * c["c_st"]
            c_scr[c["st"]] = c_new
            m_scr[c["st"]] = m_new

    @pl.when(j == 0)
    def _():
        step(True)

    @pl.when(j != 0)
    def _():
        step(False)


def _mlstm_scan(q, kt, v, g, gt):
    b, t, _ = q.shape
    L = CHUNK
    nc = t // L

    fwd = lambda j: j
    bwd = lambda j: jnp.where(j == 0, 0, nc - j)

    def specs(cidx):
        return [
            pl.BlockSpec((None, L, ML_HEADS * ML_QK), lambda bi, j: (bi, cidx(j), 0)),
            pl.BlockSpec((None, None, ML_HEADS * ML_QK, L), lambda bi, j: (bi, cidx(j), 0, 0)),
            pl.BlockSpec((None, L, ML_INNER), lambda bi, j: (bi, cidx(j), 0)),
            pl.BlockSpec((None, L, LANES), lambda bi, j: (bi, cidx(j), 0)),
            pl.BlockSpec((None, None, 4 * GATE_STRIDE, L), lambda bi, j: (bi, cidx(j), 0, 0)),
        ]

    n_chain = 2 * ML_HEADS
    h_shape = jax.ShapeDtypeStruct((b, t - L, ML_INNER), bf16)
    return pl.pallas_call(
        _scan_kernel,
        out_shape=(h_shape, h_shape),
        grid=(b, nc),
        in_specs=specs(fwd) + specs(bwd),
        out_specs=(pl.BlockSpec((None, L, ML_INNER), lambda bi, j: (bi, jnp.maximum(j - 1, 0), 0)),
                   pl.BlockSpec((None, L, ML_INNER), lambda bi, j: (bi, nc - 1 - jnp.maximum(j, 1), 0))),
        scratch_shapes=[
            pltpu.VMEM((n_chain, ML_QK, ML_V + LANES), f32),
            pltpu.VMEM((n_chain, 1, 1), f32),
        ],
        compiler_params=_cparams(("arbitrary", "arbitrary")),
        name="mlstm_scan",
    )(q, kt, v, g, gt, q, kt, v, g, gt)


def _outproj_kernel(a_ref, hf_ref, hb_ref, o_ref, x_ref, mod_ref, ghn_ref, w_ref, lg_ref, lb_ref,
                    out_ref):
    ghn = ghn_ref[...]
    gate = mod_ref[2:3, :]
    tm = x_ref.shape[0]
    halves = [slice(r, r + tm // 2) for r in (0, tm // 2)]
    lhs = []
    for rows in halves:
        hm = hf_ref[rows, :].astype(f32) + hb_ref[rows, :].astype(f32)
        og = jax.nn.sigmoid(o_ref[rows, :].astype(f32))
        parts = [a_ref[rows, :]]
        for h in range(ML_HEADS):
            sl = slice(h * ML_V, (h + 1) * ML_V)
            seg = hm[:, sl]
            mu = jnp.mean(seg, axis=-1, keepdims=True)
            dd = seg - mu
            var = jnp.mean(dd * dd, axis=-1, keepdims=True)
            parts.append((og[:, sl] * (dd * lax.rsqrt(var + EPS) * ghn[:, sl])).astype(bf16))
        lhs.append(jnp.concatenate(parts, axis=1))
    ys = [_dot(l, w_ref[...]) for l in lhs]
    for rows, y in zip(halves, ys):
        r = ALPHA * x_ref[rows, :] + gate * y
        out_ref[rows, :] = _layer_norm_rows(r, lg_ref[...], lb_ref[...])


def _out_projection(a, hf, hb, o, x, modr, g_hn, w_out, ln_g, ln_b, n_ctx_tiles):
    b, s, d = x.shape
    tm = min(OUT_TM, s)
    assert s % tm == 0
    mix = w_out.shape[0]
    return pl.pallas_call(
        _outproj_kernel,
        out_shape=jax.ShapeDtypeStruct((b, s, d), f32),
        grid=(b, s // tm),
        in_specs=[
            pl.BlockSpec((None, tm, MLA_HEADS * MLA_V), lambda bi, i: (bi, i, 0)),
            pl.BlockSpec((None, tm, ML_INNER), lambda bi, i: (bi, i, 0)),
            pl.BlockSpec((None, tm, ML_INNER), lambda bi, i: (bi, i, 0)),
            pl.BlockSpec((None, tm, ML_INNER), lambda bi, i: (bi, i, 0)),
            pl.BlockSpec((None, tm, d), lambda bi, i: (bi, i, 0)),
            pl.BlockSpec((None, N_MOD, d), lambda bi, i: (bi, 0, 0)),
            pl.BlockSpec((1, ML_INNER), lambda bi, i: (0, 0)),
            pl.BlockSpec((mix, d), lambda bi, i: (0, 0)),
            pl.BlockSpec((1, d), lambda bi, i: (0, 0)),
            pl.BlockSpec((1, d), lambda bi, i: (0, 0)),
        ],
        out_specs=pl.BlockSpec((None, tm, d), lambda bi, i: (bi, i, 0)),
        compiler_params=_cparams(("arbitrary", "arbitrary")),
        name="out_projection",
    )(a, hf, hb, o, x, modr, g_hn, w_out, ln_g, ln_b)


def _ffn_kernel(x_ref, xp_ref, xn_ref, mod_ref, wup_hbm, cw_ref, cb_ref, wd_hbm, lg_ref, lb_ref,
                out_ref, hs_ref, gs_ref, wg_buf, wu_buf, wd_buf, sem, *, n_row_tiles, nf, tf, nbuf):
    i = pl.program_id(1)
    step = pl.program_id(0) * n_row_tiles + i
    n_steps = pl.num_programs(0) * n_row_tiles
    tm = x_ref.shape[0]
    dff = nf * tf

    def weight_copies(j):
        slot = j % nbuf
        return (
            pltpu.make_async_copy(wup_hbm.at[:, pl.ds(j * tf, tf)], wg_buf.at[slot], sem.at[0, slot]),
            pltpu.make_async_copy(wup_hbm.at[:, pl.ds(dff + j * tf, tf)], wu_buf.at[slot], sem.at[1, slot]),
            pltpu.make_async_copy(wd_hbm.at[pl.ds(j * tf, tf), :], wd_buf.at[slot], sem.at[2, slot]),
        )

    @pl.when(step == 0)
    def _():
        for c in weight_copies(0):
            c.start()

    shift = mod_ref[3:4, :]
    scale = mod_ref[4:5, :]
    hs_ref[0:HALO, :] = (xp_ref[...] * (1.0 + scale) + shift).astype(bf16)
    hs_ref[HALO:HALO + tm, :] = (x_ref[...] * (1.0 + scale) + shift).astype(bf16)
    hs_ref[HALO + tm:2 * HALO + tm, :] = (xn_ref[...] * (1.0 + scale) + shift).astype(bf16)
    ridx = lax.broadcasted_iota(jnp.int32, (tm, 1), 0)
    first_row = jnp.logical_and(ridx == 0, i == 0)
    last_row = jnp.logical_and(ridx == tm - 1, i == n_row_tiles - 1)

    for j in range(nf):
        for c in weight_copies((j + 1) % nf):
            c.start()
        for c in weight_copies(j):
            c.wait()
        slot = j % nbuf
        gs = gs_ref.at[j % 2]
        cols = slice(j * tf, (j + 1) * tf)
        gs[...] = _dot(hs_ref[...], wg_buf[slot])
        g_prev = jnp.where(first_row, 0.0, gs[HALO - 1:HALO - 1 + tm, :])
        g_next = jnp.where(last_row, 0.0, gs[HALO + 1:HALO + 1 + tm, :])
        gate = (cw_ref[0:1, cols] * g_prev + cw_ref[1:2, cols] * gs[HALO:HALO + tm, :]
                + cw_ref[2:3, cols] * g_next + cb_ref[:, cols])
        up = _dot(hs_ref[HALO:HALO + tm, :], wu_buf[slot])
        act = (_silu(gate) * up).astype(bf16)
        contrib = _dot(act, wd_buf[slot])
        if j == 0:
            out_ref[...] = contrib
        else:
            out_ref[...] += contrib

    r = ALPHA * x_ref[...] + mod_ref[5:6, :] * out_ref[...]
    out_ref[...] = _layer_norm_rows(r, lg_ref[...], lb_ref[...])

    @pl.when(step == n_steps - 1)
    def _():
        for c in weight_copies(0):
            c.wait()


def _conv_ffn(x, modr, w_up, conv_w, conv_b, w_down, ln_g, ln_b):
    b, s, d = x.shape
    dff = w_down.shape[0]
    tm = min(FFN_TM, s)
    tf = min(FFN_TF, dff)
    assert s % tm == 0 and dff % tf == 0
    nr = s // tm
    nf = dff // tf
    assert nf >= 2
    nbuf = next(n for n in range(2, nf + 1) if (nf - 1) % n != 0)
    hb = tm // HALO
    return pl.pallas_call(
        functools.partial(_ffn_kernel, n_row_tiles=nr, nf=nf, tf=tf, nbuf=nbuf),
        out_shape=jax.ShapeDtypeStruct((b, s, d), f32),
        grid=(b, nr),
        in_specs=[
            pl.BlockSpec((None, tm, d), lambda bi, i: (bi, i, 0)),
            pl.BlockSpec((None, HALO, d), lambda bi, i: (bi, jnp.maximum(i * hb - 1, 0), 0)),
            pl.BlockSpec((None, HALO, d), lambda bi, i: (bi, jnp.minimum((i + 1) * hb, nr * hb - 1), 0)),
            pl.BlockSpec((None, N_MOD, d), lambda bi, i: (bi, 0, 0)),
            pl.BlockSpec(memory_space=pl.ANY),
            pl.BlockSpec((3, dff), lambda bi, i: (0, 0)),
            pl.BlockSpec((1, dff), lambda bi, i: (0, 0)),
            pl.BlockSpec(memory_space=pl.ANY),
            pl.BlockSpec((1, d), lambda bi, i: (0, 0)),
            pl.BlockSpec((1, d), lambda bi, i: (0, 0)),
        ],
        out_specs=pl.BlockSpec((None, tm, d), lambda bi, i: (bi, i, 0)),
        scratch_shapes=[
            pltpu.VMEM((tm + 2 * HALO, d), bf16),
            pltpu.VMEM((2, tm + 2 * HALO, tf), f32),
            pltpu.VMEM((nbuf, d, tf), bf16),
            pltpu.VMEM((nbuf, d, tf), bf16),
            pltpu.VMEM((nbuf, tf, d), bf16),
            pltpu.SemaphoreType.DMA((3, nbuf)),
        ],
        compiler_params=_cparams(("arbitrary", "arbitrary")),
        name="conv_ffn",
    )(x, x, x, modr, w_up, conv_w, conv_b, w_down, ln_g, ln_b)


def _rope_perm():
    q = MLA_ROPE // 4
    return jnp.concatenate([jnp.arange(q, 2 * q), jnp.arange(0, q),
                            jnp.arange(3 * q, 4 * q), jnp.arange(2 * q, 3 * q)])


def _rope_tables(s, n_ctx):
    n_rows = s // GRID_W
    row = np.repeat(np.arange(n_rows), GRID_W).astype(np.float32)
    col = np.tile(np.arange(GRID_W), n_rows).astype(np.float32)
    n_freq = MLA_ROPE // 4
    inv = (np.float32(ROPE_THETA) ** (-np.arange(n_freq, dtype=np.float32) / np.float32(n_freq))).astype(np.float32)
    ar = row[:, None] * inv
    ac = col[:, None] * inv
    cos = np.concatenate([np.cos(ar), np.cos(ar), np.cos(ac), np.cos(ac)], axis=-1).astype(np.float32)
    sin = np.concatenate([-np.sin(ar), np.sin(ar), -np.sin(ac), np.sin(ac)], axis=-1).astype(np.float32)
    pad = np.zeros((s, LANES - MLA_ROPE), np.float32)
    ctx_c = np.concatenate([np.ones((n_ctx, MLA_ROPE), np.float32),
                            np.zeros((n_ctx, LANES - MLA_ROPE), np.float32)], axis=1)
    ck = np.concatenate([ctx_c, np.concatenate([cos, pad], axis=1)], axis=0)
    sk = np.concatenate([np.zeros((n_ctx, LANES), np.float32), np.concatenate([sin, pad], axis=1)], axis=0)
    return ck, sk, np.ascontiguousarray(cos.T), np.ascontiguousarray(sin.T)


def _wprep_kernel(wuq_ref, wukv_ref, wmq_ref, wmk_ref, wm_ref, wk_ref, wvt_ref, wq_ref, wkt_ref):
    dq = MLA_NOPE + MLA_ROPE
    g = MLA_ROPE // 4
    tq = wuq_ref[...].T
    for h in range(MLA_HEADS):
        wm_ref[h, 0:dq, :] = tq[h * dq:(h + 1) * dq].astype(bf16)
        rope = tq[h * dq + MLA_NOPE:(h + 1) * dq]
        for dst, src in enumerate((1, 0, 3, 2)):
            wm_ref[h, dq + dst * g:dq + (dst + 1) * g, :] = rope[src * g:(src + 1) * g].astype(bf16)
    tkv = wukv_ref[...].T
    dkv = MLA_NOPE + MLA_V
    for h in range(MLA_HEADS):
        wk_ref[:, h * MLA_NOPE:(h + 1) * MLA_NOPE] = wukv_ref[:, h * dkv:h * dkv + MLA_NOPE].astype(bf16)
        wvt_ref[h] = tkv[h * dkv + MLA_NOPE:(h + 1) * dkv].astype(bf16)
    for h in range(ML_HEADS):
        wq_ref[h] = wmq_ref[h].astype(bf16)
        wkt_ref[h] = wmk_ref[h].T.astype(bf16)


def _weight_prep(w_uq, w_ukv, w_mq, w_mk):
    return pl.pallas_call(
        _wprep_kernel,
        out_shape=(
            jax.ShapeDtypeStruct((MLA_HEADS, QK_PAD, Q_LORA), bf16),
            jax.ShapeDtypeStruct((KV_LORA, MLA_HEADS * MLA_NOPE), bf16),
            jax.ShapeDtypeStruct((MLA_HEADS, MLA_V, KV_LORA), bf16),
            jax.ShapeDtypeStruct((ML_HEADS, ML_V, ML_QK), bf16),
            jax.ShapeDtypeStruct((ML_HEADS, ML_QK, ML_V), bf16),
        ),
        compiler_params=pltpu.CompilerParams(vmem_limit_bytes=VMEM_LIMIT),
        name="weight_prep",
    )(w_uq, w_ukv, w_mq, w_mk)


def _pack_kernel(w_ref, krp_ref, g_ref, o_ref):
    o_ref[:, 0:OFF_U] = w_ref[:, 0:OFF_U].astype(bf16)
    o_ref[:, OFF_U:P_U] = krp_ref[...].astype(bf16)
    o_ref[:, P_U:P_G] = w_ref[:, OFF_U:OFF_G].astype(bf16)
    o_ref[:, P_G:P_COLS] = g_ref[...].astype(bf16)


def _pack_w_in(w_in):
    d = w_in.shape[0]
    z = lambda n: jnp.zeros((d, n), w_in.dtype)
    gates = []
    for gi in range(4):
        gates += [w_in[:, OFF_G + gi * ML_HEADS:OFF_G + (gi + 1) * ML_HEADS], z(GATE_STRIDE - ML_HEADS)]
    gates = jnp.concatenate(gates + [z(LANES - 4 * GATE_STRIDE)], axis=1)
    krp = w_in[:, OFF_KR:OFF_U][:, _rope_perm()]
    tr = min(ROW_TILE, d)
    assert d % tr == 0
    return pl.pallas_call(
        _pack_kernel,
        out_shape=jax.ShapeDtypeStruct((d, P_COLS), bf16),
        grid=(d // tr,),
        in_specs=[
            pl.BlockSpec((tr, w_in.shape[1]), lambda i: (i, 0)),
            pl.BlockSpec((tr, MLA_ROPE), lambda i: (i, 0)),
            pl.BlockSpec((tr, LANES), lambda i: (i, 0)),
        ],
        out_specs=pl.BlockSpec((tr, P_COLS), lambda i: (i, 0)),
        compiler_params=_cparams(("arbitrary",)),
        name="pack_w_in",
    )(w_in, krp, gates)


def _pack_b_gate(b_gate):
    parts = []
    for gi in range(4):
        parts += [b_gate[gi * ML_HEADS:(gi + 1) * ML_HEADS], jnp.zeros((GATE_STRIDE - ML_HEADS,), f32)]
    parts.append(jnp.zeros((LANES - 4 * GATE_STRIDE,), f32))
    return jnp.concatenate(parts)[None, :]


def kernel(x, c, ctx, c_ctx, w_ada, b_ada, w_in, g_q, w_uq, g_kv, w_ukv, ml_conv_w, ml_conv_b, w_mq, w_mk, b_gate, g_hn, w_out, ln1_g, ln1_b, w_up, ffn_conv_w, ffn_conv_b, w_down, ln2_g, ln2_b):
    b, s, d = x.shape
    n_ctx = ctx.shape[1]
    assert n_ctx == CHUNK and s % ROW_TILE == 0
    n_ctx_tiles = n_ctx // ROW_TILE
    layer = 0

    rows = -(-(b + 1) // SUBLANES) * SUBLANES
    cc = jnp.concatenate([c, c_ctx[None, :], jnp.zeros((rows - b - 1, d), f32)], axis=0)
    modr = _modulation(cc, w_ada[layer], b_ada[layer][None, :]).reshape(rows, N_MOD, d)

    ck, sk, ct, st = (jnp.asarray(tbl) for tbl in _rope_tables(s, n_ctx))

    cq, ckv, kr, g, gt, u, v, o = _in_projection(
        ctx, x, modr, _pack_w_in(w_in[layer]), g_q[layer][None, :], g_kv[layer][None, :], ck, sk,
        _pack_b_gate(b_gate[layer]), n_ctx_tiles)

    wm, wk, wvt, w_q, w_kt = _weight_prep(w_uq[layer], w_ukv[layer], w_mq[layer], w_mk[layer])
    kk, vt, ksq = _kv_prep(ckv, kr, wk, wvt)
    qt = _q_prep(cq, ksq, wm, ct, st, n_ctx_tiles)
    a = _attention(qt, kk, vt)

    mq, mkt = _mlstm_projection(u, ml_conv_w[layer], ml_conv_b[layer][None, :], w_q, w_kt, n_ctx_tiles)
    hf, hb = _mlstm_scan(mq, mkt, v, g, gt)

    x1 = _out_projection(a, hf, hb, o, x, modr, g_hn[layer].reshape(1, ML_INNER), w_out[layer].astype(bf16),
                         ln1_g[layer][None, :], ln1_b[layer][None, :], n_ctx_tiles)
    return _conv_ffn(x1, modr, w_up[layer].astype(bf16), ffn_conv_w[layer], ffn_conv_b[layer][None, :],
                     w_down[layer].astype(bf16), ln2_g[layer][None, :], ln2_b[layer][None, :])
```

```python
import functools

import numpy as np
import jax
import jax.numpy as jnp
from jax import lax
from jax.experimental import pallas as pl
from jax.experimental.pallas import tpu as pltpu

f32 = jnp.float32
bf16 = jnp.bfloat16

GRID_W = 64
MLA_HEADS = 8
MLA_NOPE = 128
MLA_ROPE = 64
MLA_V = 128
Q_LORA = 512
KV_LORA = 256
ROPE_THETA = 10000.0
ML_HEADS = 4
ML_QK = 128
ML_V = 256
ML_INNER = ML_HEADS * ML_V
N_GATES = 4 * ML_HEADS
DEPTH = 1
OFF_CKV = Q_LORA
OFF_KR = Q_LORA + KV_LORA
OFF_U = OFF_KR + MLA_ROPE
OFF_V = OFF_U + ML_INNER
OFF_O = OFF_V + ML_INNER
OFF_G = OFF_O + ML_INNER
ALPHA = (2.0 * DEPTH) ** 0.25
EPS = 1e-6
N_MOD = 6

LANES = 128
SUBLANES = 8
MXU_DIM = 256
VMEM_LIMIT = 56 * 1024 * 1024

QK_PAD = 2 * LANES
GATE_STRIDE = SUBLANES
P_Q = 0
P_CKV = P_Q + Q_LORA
P_KR = P_CKV + KV_LORA
P_U = P_KR + LANES
P_V = P_U + ML_INNER
P_O = P_V + ML_INNER
P_G = P_O + ML_INNER
P_COLS = P_G + LANES
STAB_ROWS = 16
STAB_SLACK = 1.02
L_MIN = 2.0 ** -100

ROW_TILE = 256
CHUNK = ROW_TILE
ATT_TQ = 1024
ATT_TK_MAX = 2816
OUT_TM = 512
FFN_TM = 512
FFN_TF = 512
HALO = SUBLANES


def _cparams(sem):
    return pltpu.CompilerParams(dimension_semantics=sem, vmem_limit_bytes=VMEM_LIMIT)


def _silu(x):
    return x * jax.nn.sigmoid(x)


def _dot(a, b):
    return jnp.dot(a, b, preferred_element_type=f32)


def _dot_nt(a, b):
    return lax.dot_general(a, b, (((1,), (1,)), ((), ())), preferred_element_type=f32)


def _layer_norm_rows(r, g, b):
    mu = jnp.mean(r, axis=-1, keepdims=True)
    d = r - mu
    var = jnp.mean(d * d, axis=-1, keepdims=True)
    return d * lax.rsqrt(var + EPS) * g + b


def _mod_kernel(c_ref, w_ref, b_ref, o_ref):
    s = _silu(c_ref[...])
    o_ref[...] = _dot(s.astype(bf16), w_ref[...].astype(bf16)) + b_ref[...]


def _modulation(cc, w_ada, b_ada):
    rows, d = cc.shape
    n = w_ada.shape[1]
    tn = min(n, 1536)
    assert n % tn == 0
    return pl.pallas_call(
        _mod_kernel,
        out_shape=jax.ShapeDtypeStruct((rows, n), f32),
        grid=(n // tn,),
        in_specs=[
            pl.BlockSpec((rows, d), lambda j: (0, 0)),
            pl.BlockSpec((d, tn), lambda j: (0, j)),
            pl.BlockSpec((1, tn), lambda j: (0, j)),
        ],
        out_specs=pl.BlockSpec((rows, tn), lambda j: (0, j)),
        compiler_params=_cparams(("arbitrary",)),
        name="adaln_mod",
    )(cc, w_ada, b_ada)


def _inproj_kernel(ctx_ref, x_ref, mod_ref, w_ref, gq_ref, gkv_ref, ck_ref, sk_ref, bg_ref, wk_ref, wvt_ref,
                   cq_ref, k_ref, vt_ref, ksq_ref, g_ref, gt_ref, u_ref, v_ref, o_ref, h_ref, *, n_ctx_tiles):
    shift = mod_ref[0:1, :]
    scale = mod_ref[1:2, :]
    is_ctx = pl.program_id(1) < n_ctx_tiles
    rows = jnp.where(is_ctx, ctx_ref[...], x_ref[...])
    h_ref[...] = (rows * (1.0 + scale) + shift).astype(bf16)

    def sect(a, n):
        return _dot(h_ref[...], w_ref[:, a:a + n])

    zq = sect(P_Q, Q_LORA)
    cq = zq * lax.rsqrt(jnp.mean(zq * zq, axis=-1, keepdims=True) + EPS) * gq_ref[...]
    cq_ref[...] = cq.astype(bf16)
    zkv = sect(P_CKV, KV_LORA)
    ckv = zkv * lax.rsqrt(jnp.mean(zkv * zkv, axis=-1, keepdims=True) + EPS) * gkv_ref[...]
    zkr = sect(P_KR, LANES)
    kr = zkr * ck_ref[...] + pltpu.roll(zkr, LANES - MLA_ROPE, 1) * sk_ref[...]
    _kv_outputs(ckv.astype(bf16), kr.astype(bf16), wk_ref, wvt_ref, k_ref, vt_ref, ksq_ref)
    g = sect(P_G, LANES) + bg_ref[...]
    tm = g.shape[0]
    lf = jax.nn.log_sigmoid(g)
    tri = jnp.where(lax.broadcasted_iota(jnp.int32, (tm, tm), 0)
                    >= lax.broadcasted_iota(jnp.int32, (tm, tm), 1), 1.0, 0.0).astype(bf16)
    hi = lf.astype(bf16)
    r1 = lf - hi.astype(f32)
    mid = r1.astype(bf16)
    lo = (r1 - mid.astype(f32)).astype(bf16)
    pre = _dot(tri, hi) + _dot(tri, mid) + _dot(tri, lo)
    suf = pre[tm - 1:tm, :] - pre + lf
    grp = lax.broadcasted_iota(jnp.int32, (1, LANES), 1) // GATE_STRIDE
    g2 = jnp.where(grp == 1, pre, jnp.where(grp == 3, suf, g))
    g_ref[...] = g2
    gt_ref[...] = g2.T[0:4 * GATE_STRIDE, :]
    u_ref[...] = sect(P_U, ML_INNER).astype(bf16)
    v_ref[...] = sect(P_V, ML_INNER).astype(bf16)
    o_ref[...] = sect(P_O, ML_INNER).astype(bf16)


def _in_projection(ctx, x, modr, w_in_p, g_q, g_kv, ck, sk, bg, wk, wvt, n_ctx_tiles):
    b, s, d = x.shape
    tm = ROW_TILE
    t = s + n_ctx_tiles * tm
    nt = t // tm
    n_batch = b

    def mod_idx(bi, i):
        return (jnp.where(i < n_ctx_tiles, n_batch, bi), 0, 0)

    row = lambda w: pl.BlockSpec((None, tm, w), lambda bi, i: (bi, i, 0))
    const = lambda r, c: pl.BlockSpec((r, c), lambda bi, i: (0, 0))
    out_shapes = (
        jax.ShapeDtypeStruct((b, t, Q_LORA), bf16),
        jax.ShapeDtypeStruct((b, MLA_HEADS, t, QK_PAD), bf16),
        jax.ShapeDtypeStruct((b, MLA_HEADS, MLA_V, t), bf16),
        jax.ShapeDtypeStruct((b, MLA_HEADS, t), f32),
        jax.ShapeDtypeStruct((b, t, LANES), f32),
        jax.ShapeDtypeStruct((b, nt, 4 * GATE_STRIDE, tm), f32),
        jax.ShapeDtypeStruct((b, t, ML_INNER), bf16),
        jax.ShapeDtypeStruct((b, t, ML_INNER), bf16),
        jax.ShapeDtypeStruct((b, s, ML_INNER), bf16),
    )
    out_specs = (
        row(Q_LORA),
        pl.BlockSpec((None, MLA_HEADS, tm, QK_PAD), lambda bi, i: (bi, 0, i, 0)),
        pl.BlockSpec((None, MLA_HEADS, MLA_V, tm), lambda bi, i: (bi, 0, 0, i)),
        pl.BlockSpec((None, MLA_HEADS, tm), lambda bi, i: (bi, 0, i)),
        row(LANES),
        pl.BlockSpec((None, None, 4 * GATE_STRIDE, tm), lambda bi, i: (bi, i, 0, 0)),
        row(ML_INNER), row(ML_INNER),
        pl.BlockSpec((None, tm, ML_INNER), lambda bi, i: (bi, jnp.maximum(i - n_ctx_tiles, 0), 0)),
    )
    return pl.pallas_call(
        functools.partial(_inproj_kernel, n_ctx_tiles=n_ctx_tiles),
        out_shape=out_shapes,
        grid=(b, nt),
        in_specs=[
            pl.BlockSpec((None, tm, d), lambda bi, i: (bi, jnp.minimum(i, n_ctx_tiles - 1), 0)),
            pl.BlockSpec((None, tm, d), lambda bi, i: (bi, jnp.maximum(i - n_ctx_tiles, 0), 0)),
            pl.BlockSpec((None, N_MOD, d), mod_idx),
            pl.BlockSpec((d, P_COLS), lambda bi, i: (0, 0), pipeline_mode=pl.Buffered(1)),
            const(1, Q_LORA),
            const(1, KV_LORA),
            pl.BlockSpec((tm, LANES), lambda bi, i: (i, 0)),
            pl.BlockSpec((tm, LANES), lambda bi, i: (i, 0)),
            const(1, LANES),
            pl.BlockSpec((KV_LORA, MLA_HEADS * MLA_NOPE), lambda bi, i: (0, 0), pipeline_mode=pl.Buffered(1)),
            pl.BlockSpec((MLA_HEADS, MLA_V, KV_LORA), lambda bi, i: (0, 0, 0), pipeline_mode=pl.Buffered(1)),
        ],
        out_specs=out_specs,
        scratch_shapes=[pltpu.VMEM((tm, d), bf16)],
        compiler_params=_cparams(("arbitrary", "arbitrary")),
        name="in_projection",
    )(ctx, x, modr, w_in_p, g_q, g_kv, ck, sk, bg, wk, wvt)


def _kv_outputs(ckv, kr, wk_ref, wvt_ref, k_ref, vt_ref, ksq_ref):
    kn = _dot(ckv, wk_ref[...]).astype(bf16)
    knf = kn.astype(f32)
    krf = kr.astype(f32)
    hsel = lax.broadcasted_iota(jnp.int32, (MLA_HEADS, MLA_HEADS * MLA_NOPE), 0)
    csel = lax.broadcasted_iota(jnp.int32, (MLA_HEADS, MLA_HEADS * MLA_NOPE), 1) // MLA_NOPE
    sel = jnp.where(hsel == csel, 1.0, 0.0).astype(bf16)
    ksq_ref[...] = (_dot_nt(sel, (knf * knf).astype(bf16))
                    + _dot_nt(jnp.ones((MLA_HEADS, LANES), bf16), (krf * krf).astype(bf16)))
    lane = lax.broadcasted_iota(jnp.int32, kr.shape, 1)
    stab = jnp.logical_and(lane >= MLA_ROPE, lane < MLA_ROPE + STAB_ROWS)
    kr_aug = jnp.where(stab, 1.0, krf).astype(bf16)
    for h in range(MLA_HEADS):
        k_ref[h, :, 0:MLA_NOPE] = kn[:, h * MLA_NOPE:(h + 1) * MLA_NOPE]
        k_ref[h, :, MLA_NOPE:QK_PAD] = kr_aug
        vt_ref[h] = _dot_nt(wvt_ref[h], ckv).astype(bf16)


def _q_kernel(cq_ref, ksq_ref, wm_ref, ct_ref, st_ref, qt_ref, *, scale):
    cq = cq_ref[...]
    ct = ct_ref[...]
    st = st_ref[...]
    tm = cq.shape[0]
    kmax2 = jnp.max(ksq_ref[...], axis=1, keepdims=True)
    r0, r1 = MLA_NOPE, MLA_NOPE + MLA_ROPE
    for h in range(MLA_HEADS):
        qm = _dot_nt(wm_ref[h], cq)
        qn = (qm[0:r0] * scale).astype(bf16)
        qr = ((qm[r0:r1] * ct + qm[r1:QK_PAD] * st) * scale).astype(bf16)
        qnf = qn.astype(f32)
        qrf = qr.astype(f32)
        qsq = jnp.sum(qnf * qnf, axis=0, keepdims=True) + jnp.sum(qrf * qrf, axis=0, keepdims=True)
        bound = jnp.sqrt(qsq * kmax2[h:h + 1, :]) * STAB_SLACK
        qt_ref[h, 0:r0, :] = qn
        qt_ref[h, r0:r1, :] = qr
        qt_ref[h, r1:r1 + STAB_ROWS, :] = jnp.broadcast_to(bound * (-1.0 / STAB_ROWS), (STAB_ROWS, tm)).astype(bf16)
        qt_ref[h, r1 + STAB_ROWS:QK_PAD, :] = jnp.zeros((QK_PAD - r1 - STAB_ROWS, tm), bf16)


def _q_prep(cq, ksq, wm, ct, st, n_ctx_tiles):
    b, t, _ = cq.shape
    tm = ROW_TILE
    s = t - n_ctx_tiles * tm
    scale = float((MLA_NOPE + MLA_ROPE) ** -0.5 * 1.4426950408889634)
    return pl.pallas_call(
        functools.partial(_q_kernel, scale=scale),
        out_shape=jax.ShapeDtypeStruct((b, MLA_HEADS, QK_PAD, s), bf16),
        grid=(b, s // tm),
        in_specs=[
            pl.BlockSpec((None, tm, Q_LORA), lambda bi, i: (bi, i + n_ctx_tiles, 0)),
            pl.BlockSpec((None, MLA_HEADS, t), lambda bi, i: (bi, 0, 0)),
            pl.BlockSpec((MLA_HEADS, QK_PAD, Q_LORA), lambda bi, i: (0, 0, 0)),
            pl.BlockSpec((MLA_ROPE, tm), lambda bi, i: (0, i)),
            pl.BlockSpec((MLA_ROPE, tm), lambda bi, i: (0, i)),
        ],
        out_specs=pl.BlockSpec((None, MLA_HEADS, QK_PAD, tm), lambda bi, i: (bi, 0, 0, i)),
        compiler_params=_cparams(("arbitrary", "arbitrary")),
        name="q_prep",
    )(cq, ksq, wm, ct, st)


def _attn_bound_kernel(qt_ref, k_ref, vt_ref, o_ref, l_ref, *, tq, tk, nk):
    def q_tile(i, carry):
        q0 = pl.multiple_of(i * tq, tq)
        qt = qt_ref[:, pl.ds(q0, tq)]
        l = jnp.zeros((1, tq), f32)
        acc = jnp.zeros((MLA_V, tq), f32)
        s = _dot(k_ref[0:tk, :], qt)
        for j in range(nk):
            s_next = _dot(k_ref[(j + 1) * tk:(j + 2) * tk, :], qt) if j + 1 < nk else None
            p = jnp.exp2(s)
            l = l + jnp.sum(p, axis=0, keepdims=True)
            acc = acc + _dot(vt_ref[:, j * tk:(j + 1) * tk], p.astype(bf16))
            s = s_next
        o_ref[pl.ds(q0, tq), :] = (acc / l).T.astype(o_ref.dtype)
        l_ref[:, pl.ds(q0, tq)] = l
        return carry

    lax.fori_loop(0, qt_ref.shape[1] // tq, q_tile, 0)


def _attn_online_kernel(qt_ref, k_ref, vt_ref, o_ref, *, tk, nk):
    qt = qt_ref[...]
    tq = qt.shape[1]
    m = jnp.full((1, tq), -jnp.inf, f32)
    l = jnp.zeros((1, tq), f32)
    acc = jnp.zeros((MLA_V, tq), f32)
    for j in range(nk):
        s = _dot(k_ref[j * tk:(j + 1) * tk, :], qt)
        m_new = jnp.maximum(m, jnp.max(s, axis=0, keepdims=True))
        alpha = jnp.exp2(m - m_new)
        p = jnp.exp2(s - m_new)
        l = alpha * l + jnp.sum(p, axis=0, keepdims=True)
        acc = alpha * acc + _dot(vt_ref[:, j * tk:(j + 1) * tk], p.astype(bf16))
        m = m_new
    o_ref[...] = (acc / l).T.astype(o_ref.dtype)


def _attention(qt, k, vt):
    b, hh, _, s = qt.shape
    t = k.shape[2]
    tq = min(ATT_TQ, s)
    tk = MXU_DIM
    for cand in range(MXU_DIM, ATT_TK_MAX + 1, MXU_DIM):
        if t % cand == 0:
            tk = cand
    assert s % tq == 0 and t % tk == 0
    in_specs = [
        pl.BlockSpec((None, None, QK_PAD, tq), lambda bi, h, i: (bi, h, 0, i)),
        pl.BlockSpec((None, None, t, QK_PAD), lambda bi, h, i: (bi, h, 0, 0)),
        pl.BlockSpec((None, None, MLA_V, t), lambda bi, h, i: (bi, h, 0, 0)),
    ]
    a_shape = jax.ShapeDtypeStruct((b, s, hh * MLA_V), bf16)
    a_spec = pl.BlockSpec((None, tq, MLA_V), lambda bi, h, i: (bi, i, h))
    sem = ("arbitrary", "arbitrary", "arbitrary")
    a, l = pl.pallas_call(
        functools.partial(_attn_bound_kernel, tq=tq, tk=tk, nk=t // tk),
        out_shape=(a_shape, jax.ShapeDtypeStruct((b, hh, 1, s), f32)),
        grid=(b, hh),
        in_specs=[
            pl.BlockSpec((None, None, QK_PAD, s), lambda bi, h: (bi, h, 0, 0)),
            pl.BlockSpec((None, None, t, QK_PAD), lambda bi, h: (bi, h, 0, 0)),
            pl.BlockSpec((None, None, MLA_V, t), lambda bi, h: (bi, h, 0, 0)),
        ],
        out_specs=(pl.BlockSpec((None, s, MLA_V), lambda bi, h: (bi, 0, h)),
                   pl.BlockSpec((None, None, 1, s), lambda bi, h: (bi, h, 0, 0))),
        compiler_params=_cparams(("arbitrary", "arbitrary")),
        name="attention_bound",
    )(qt, k, vt)

    def online(_):
        return pl.pallas_call(
            functools.partial(_attn_online_kernel, tk=tk, nk=t // tk),
            out_shape=a_shape,
            grid=(b, hh, s // tq),
            in_specs=in_specs,
            out_specs=a_spec,
            compiler_params=_cparams(sem),
            name="attention_online",
        )(qt, k, vt)

    ok = jnp.min(l) >= L_MIN
    return lax.cond(ok, lambda _: a, online, None)


def _mproj_kernel(u_ref, up_ref, un_ref, cw_ref, cb_ref, wq_ref, wkt_ref, q_ref, kt_ref, *,
                  n_ctx_tiles, n_tiles):
    i = pl.program_id(1)
    u = u_ref[...].astype(f32)
    tm = u.shape[0]
    first = jnp.logical_or(i == 0, i == n_ctx_tiles)
    last = jnp.logical_or(i == n_ctx_tiles - 1, i == n_tiles - 1)
    prev_row = jnp.where(first, 0.0, up_ref[HALO - 1:HALO, :].astype(f32))
    next_row = jnp.where(last, 0.0, un_ref[0:1, :].astype(f32))
    ridx = lax.broadcasted_iota(jnp.int32, (tm, 1), 0)
    u_prev = jnp.where(ridx == 0, prev_row, pltpu.roll(u, 1, 0))
    u_next = jnp.where(ridx == tm - 1, next_row, pltpu.roll(u, tm - 1, 0))
    y = cw_ref[0:1, :] * u_prev + cw_ref[1:2, :] * u + cw_ref[2:3, :] * u_next + cb_ref[...]
    act = _silu(y).astype(bf16)
    for h in range(ML_HEADS):
        a_h = act[:, h * ML_V:(h + 1) * ML_V]
        q_ref[:, h * ML_QK:(h + 1) * ML_QK] = (_dot(a_h, wq_ref[h]) * float(ML_QK ** -0.5)).astype(bf16)
        kt_ref[h * ML_QK:(h + 1) * ML_QK, :] = _dot_nt(wkt_ref[h], a_h).astype(bf16)


def _mlstm_projection(u, conv_w, conv_b, w_q, w_kt, n_ctx_tiles):
    b, t, _ = u.shape
    tm = ROW_TILE
    nt = t // tm
    hb = tm // HALO
    return pl.pallas_call(
        functools.partial(_mproj_kernel, n_ctx_tiles=n_ctx_tiles, n_tiles=nt),
        out_shape=(jax.ShapeDtypeStruct((b, t, ML_HEADS * ML_QK), bf16),
                   jax.ShapeDtypeStruct((b, nt, ML_HEADS * ML_QK, tm), bf16)),
        grid=(b, nt),
        in_specs=[
            pl.BlockSpec((None, tm, ML_INNER), lambda bi, i: (bi, i, 0)),
            pl.BlockSpec((None, HALO, ML_INNER), lambda bi, i: (bi, jnp.maximum(i * hb - 1, 0), 0)),
            pl.BlockSpec((None, HALO, ML_INNER),
                         lambda bi, i: (bi, jnp.minimum((i + 1) * hb, nt * hb - 1), 0)),
            pl.BlockSpec((3, ML_INNER), lambda bi, i: (0, 0)),
            pl.BlockSpec((1, ML_INNER), lambda bi, i: (0, 0)),
            pl.BlockSpec((ML_HEADS, ML_V, ML_QK), lambda bi, i: (0, 0, 0)),
            pl.BlockSpec((ML_HEADS, ML_QK, ML_V), lambda bi, i: (0, 0, 0)),
        ],
        out_specs=(pl.BlockSpec((None, tm, ML_HEADS * ML_QK), lambda bi, i: (bi, i, 0)),
                   pl.BlockSpec((None, None, ML_HEADS * ML_QK, tm), lambda bi, i: (bi, i, 0, 0))),
        compiler_params=_cparams(("arbitrary", "arbitrary")),
        name="mlstm_projection",
    )(u, u, u, conv_w, conv_b, w_q, w_kt)


def _scan_kernel(qf_ref, ktf_ref, vf_ref, gf_ref, gtf_ref, qb_ref, ktb_ref, vb_ref, gb_ref, gtb_ref,
                 hf_ref, hb_ref, c_scr, m_scr):
    j = pl.program_id(1)
    L = qf_ref.shape[0]
    GS = GATE_STRIDE

    rr = lax.broadcasted_iota(jnp.int32, (L, L), 0)
    cc = lax.broadcasted_iota(jnp.int32, (L, L), 1)
    visible = (rr >= cc, rr <= cc)
    dirs = ((qf_ref, ktf_ref, vf_ref, gf_ref, gtf_ref, hf_ref), (qb_ref, ktb_ref, vb_ref, gb_ref, gtb_ref, hb_ref))
    ones_col = jnp.where(lax.broadcasted_iota(jnp.int32, (L, LANES), 1) == 0, 1.0, 0.0).astype(bf16)

    chains = []
    for d, (q_ref, kt_ref, v_ref, g_ref, gt_ref, h_ref) in enumerate(dirs):
        g = g_ref[...]
        gt = gt_ref[...]
        for h in range(ML_HEADS):
            ci = 2 * d * GS + h
            cf = ci + GS
            bc_r = gt[cf:cf + 1, :]
            chains.append(dict(
                d=d, h=h, st=d * ML_HEADS + h, h_ref=h_ref,
                q=q_ref[:, h * ML_QK:(h + 1) * ML_QK],
                kt=kt_ref[h * ML_QK:(h + 1) * ML_QK, :],
                v=jnp.concatenate([v_ref[:, h * ML_V:(h + 1) * ML_V], ones_col], axis=1),
                bc_c=g[:, cf:cf + 1],
                ig_r=gt[ci:ci + 1, :], bc_r=bc_r,
                a=bc_r[:, L - 1:L] if d == 0 else bc_r[:, 0:1],
            ))

    def step(first):
        for c in chains:
            c["s"] = _dot(c["q"], c["kt"])
        for c in chains:
            if not first:
                c["c_st"] = c_scr[c["st"]]
                c["m_st"] = m_scr[c["st"]]
                c["qc"] = _dot(c["q"], c["c_st"].astype(bf16))
            else:
                c["m_st"] = 0.0

        for c in chains:
            drow = jnp.where(visible[c["d"]], c["ig_r"] - c["bc_r"], -jnp.inf)
            mm = jnp.maximum(c["m_st"], jnp.max(drow, axis=1, keepdims=True))
            c["w_inter"] = jnp.exp(c["m_st"] - mm)
            c["floor"] = jnp.exp(-(c["bc_c"] + mm))
            c["p"] = jnp.exp(drow - mm) * c["s"]

        for c in chains:
            nd = _dot(c["p"].astype(bf16), c["v"])
            if not first:
                nd = nd + c["w_inter"] * c["qc"]
            den = nd[:, ML_V:ML_V + 1]
            h = c["h"]
            hv = nd[:, 0:ML_V] / jnp.maximum(jnp.abs(den), c["floor"])
            c["h_ref"][:, h * ML_V:(h + 1) * ML_V] = hv.astype(c["h_ref"].dtype)

        for c in chains:
            a, m_st = c["a"], c["m_st"]
            g_log_r = a - c["bc_r"] + c["ig_r"]
            m_new = jnp.maximum(a + m_st, jnp.max(g_log_r, axis=1, keepdims=True))
            kwt = c["kt"].astype(f32) * jnp.exp(g_log_r - m_new)
            c_new = _dot(kwt.astype(bf16), c["v"])
            if not first:
                c_new = c_new + jnp.exp(a + m_st - m_new) * c["c_st"]
            c_scr[c["st"]] = c_new
            m_scr[c["st"]] = m_new

    @pl.when(j == 0)
    def _():
        step(True)

    @pl.when(j != 0)
    def _():
        step(False)


def _mlstm_scan(q, kt, v, g, gt):
    b, t, _ = q.shape
    L = CHUNK
    nc = t // L

    fwd = lambda j: j
    bwd = lambda j: jnp.where(j == 0, 0, nc - j)

    def specs(cidx):
        return [
            pl.BlockSpec((None, L, ML_HEADS * ML_QK), lambda bi, j: (bi, cidx(j), 0)),
            pl.BlockSpec((None, None, ML_HEADS * ML_QK, L), lambda bi, j: (bi, cidx(j), 0, 0)),
            pl.BlockSpec((None, L, ML_INNER), lambda bi, j: (bi, cidx(j), 0)),
            pl.BlockSpec((None, L, LANES), lambda bi, j: (bi, cidx(j), 0)),
            pl.BlockSpec((None, None, 4 * GATE_STRIDE, L), lambda bi, j: (bi, cidx(j), 0, 0)),
        ]

    n_chain = 2 * ML_HEADS
    h_shape = jax.ShapeDtypeStruct((b, t - L, ML_INNER), bf16)
    return pl.pallas_call(
        _scan_kernel,
        out_shape=(h_shape, h_shape),
        grid=(b, nc),
        in_specs=specs(fwd) + specs(bwd),
        out_specs=(pl.BlockSpec((None, L, ML_INNER), lambda bi, j: (bi, jnp.maximum(j - 1, 0), 0)),
                   pl.BlockSpec((None, L, ML_INNER), lambda bi, j: (bi, nc - 1 - jnp.maximum(j, 1), 0))),
        scratch_shapes=[
            pltpu.VMEM((n_chain, ML_QK, ML_V + LANES), f32),
            pltpu.VMEM((n_chain, 1, 1), f32),
        ],
        compiler_params=_cparams(("arbitrary", "arbitrary")),
        name="mlstm_scan",
    )(q, kt, v, g, gt, q, kt, v, g, gt)


def _outproj_kernel(a_ref, hf_ref, hb_ref, o_ref, x_ref, mod_ref, ghn_ref, w_ref, lg_ref, lb_ref,
                    out_ref):
    ghn = ghn_ref[...]
    gate = mod_ref[2:3, :]
    tm = x_ref.shape[0]
    halves = [slice(r, r + tm // 2) for r in (0, tm // 2)]
    lhs = []
    for rows in halves:
        hm = hf_ref[rows, :].astype(f32) + hb_ref[rows, :].astype(f32)
        og = jax.nn.sigmoid(o_ref[rows, :].astype(f32))
        parts = [a_ref[rows, :]]
        for h in range(ML_HEADS):
            sl = slice(h * ML_V, (h + 1) * ML_V)
            seg = hm[:, sl]
            mu = jnp.mean(seg, axis=-1, keepdims=True)
            dd = seg - mu
            var = jnp.mean(dd * dd, axis=-1, keepdims=True)
            parts.append((og[:, sl] * (dd * lax.rsqrt(var + EPS) * ghn[:, sl])).astype(bf16))
        lhs.append(jnp.concatenate(parts, axis=1))
    ys = [_dot(l, w_ref[...]) for l in lhs]
    for rows, y in zip(halves, ys):
        r = ALPHA * x_ref[rows, :] + gate * y
        out_ref[rows, :] = _layer_norm_rows(r, lg_ref[...], lb_ref[...])


def _out_projection(a, hf, hb, o, x, modr, g_hn, w_out, ln_g, ln_b, n_ctx_tiles):
    b, s, d = x.shape
    tm = min(OUT_TM, s)
    assert s % tm == 0
    mix = w_out.shape[0]
    return pl.pallas_call(
        _outproj_kernel,
        out_shape=jax.ShapeDtypeStruct((b, s, d), f32),
        grid=(b, s // tm),
        in_specs=[
            pl.BlockSpec((None, tm, MLA_HEADS * MLA_V), lambda bi, i: (bi, i, 0)),
            pl.BlockSpec((None, tm, ML_INNER), lambda bi, i: (bi, i, 0)),
            pl.BlockSpec((None, tm, ML_INNER), lambda bi, i: (bi, i, 0)),
            pl.BlockSpec((None, tm, ML_INNER), lambda bi, i: (bi, i, 0)),
            pl.BlockSpec((None, tm, d), lambda bi, i: (bi, i, 0)),
            pl.BlockSpec((None, N_MOD, d), lambda bi, i: (bi, 0, 0)),
            pl.BlockSpec((1, ML_INNER), lambda bi, i: (0, 0)),
            pl.BlockSpec((mix, d), lambda bi, i: (0, 0)),
            pl.BlockSpec((1, d), lambda bi, i: (0, 0)),
            pl.BlockSpec((1, d), lambda bi, i: (0, 0)),
        ],
        out_specs=pl.BlockSpec((None, tm, d), lambda bi, i: (bi, i, 0)),
        compiler_params=_cparams(("arbitrary", "arbitrary")),
        name="out_projection",
    )(a, hf, hb, o, x, modr, g_hn, w_out, ln_g, ln_b)


def _ffn_kernel(x_ref, xp_ref, xn_ref, mod_ref, wup_hbm, cw_ref, cb_ref, wd_hbm, lg_ref, lb_ref,
                out_ref, hs_ref, gs_ref, wg_buf, wu_buf, wd_buf, sem, *, n_row_tiles, nf, tf, nbuf):
    i = pl.program_id(1)
    step = pl.program_id(0) * n_row_tiles + i
    n_steps = pl.num_programs(0) * n_row_tiles
    tm = x_ref.shape[0]
    dff = nf * tf

    def weight_copies(j):
        slot = j % nbuf
        return (
            pltpu.make_async_copy(wup_hbm.at[:, pl.ds(j * tf, tf)], wg_buf.at[slot], sem.at[0, slot]),
            pltpu.make_async_copy(wup_hbm.at[:, pl.ds(dff + j * tf, tf)], wu_buf.at[slot], sem.at[1, slot]),
            pltpu.make_async_copy(wd_hbm.at[pl.ds(j * tf, tf), :], wd_buf.at[slot], sem.at[2, slot]),
        )

    @pl.when(step == 0)
    def _():
        for c in weight_copies(0):
            c.start()

    shift = mod_ref[3:4, :]
    scale = mod_ref[4:5, :]
    hs_ref[0:HALO, :] = (xp_ref[...] * (1.0 + scale) + shift).astype(bf16)
    hs_ref[HALO:HALO + tm, :] = (x_ref[...] * (1.0 + scale) + shift).astype(bf16)
    hs_ref[HALO + tm:2 * HALO + tm, :] = (xn_ref[...] * (1.0 + scale) + shift).astype(bf16)
    ridx = lax.broadcasted_iota(jnp.int32, (tm, 1), 0)
    first_row = jnp.logical_and(ridx == 0, i == 0)
    last_row = jnp.logical_and(ridx == tm - 1, i == n_row_tiles - 1)

    for j in range(nf):
        for c in weight_copies((j + 1) % nf):
            c.start()
        for c in weight_copies(j):
            c.wait()
        slot = j % nbuf
        gs = gs_ref.at[j % 2]
        cols = slice(j * tf, (j + 1) * tf)
        gs[...] = _dot(hs_ref[...], wg_buf[slot])
        g_prev = jnp.where(first_row, 0.0, gs[HALO - 1:HALO - 1 + tm, :])
        g_next = jnp.where(last_row, 0.0, gs[HALO + 1:HALO + 1 + tm, :])
        gate = (cw_ref[0:1, cols] * g_prev + cw_ref[1:2, cols] * gs[HALO:HALO + tm, :]
                + cw_ref[2:3, cols] * g_next + cb_ref[:, cols])
        up = _dot(hs_ref[HALO:HALO + tm, :], wu_buf[slot])
        act = (_silu(gate) * up).astype(bf16)
        contrib = _dot(act, wd_buf[slot])
        if j == 0:
            out_ref[...] = contrib
        else:
            out_ref[...] += contrib

    r = ALPHA * x_ref[...] + mod_ref[5:6, :] * out_ref[...]
    out_ref[...] = _layer_norm_rows(r, lg_ref[...], lb_ref[...])

    @pl.when(step == n_steps - 1)
    def _():
        for c in weight_copies(0):
            c.wait()


def _conv_ffn(x, modr, w_up, conv_w, conv_b, w_down, ln_g, ln_b):
    b, s, d = x.shape
    dff = w_down.shape[0]
    tm = min(FFN_TM, s)
    tf = min(FFN_TF, dff)
    assert s % tm == 0 and dff % tf == 0
    nr = s // tm
    nf = dff // tf
    assert nf >= 2
    nbuf = next(n for n in range(2, nf + 1) if (nf - 1) % n != 0)
    hb = tm // HALO
    return pl.pallas_call(
        functools.partial(_ffn_kernel, n_row_tiles=nr, nf=nf, tf=tf, nbuf=nbuf),
        out_shape=jax.ShapeDtypeStruct((b, s, d), f32),
        grid=(b, nr),
        in_specs=[
            pl.BlockSpec((None, tm, d), lambda bi, i: (bi, i, 0)),
            pl.BlockSpec((None, HALO, d), lambda bi, i: (bi, jnp.maximum(i * hb - 1, 0), 0)),
            pl.BlockSpec((None, HALO, d), lambda bi, i: (bi, jnp.minimum((i + 1) * hb, nr * hb - 1), 0)),
            pl.BlockSpec((None, N_MOD, d), lambda bi, i: (bi, 0, 0)),
            pl.BlockSpec(memory_space=pl.ANY),
            pl.BlockSpec((3, dff), lambda bi, i: (0, 0)),
            pl.BlockSpec((1, dff), lambda bi, i: (0, 0)),
            pl.BlockSpec(memory_space=pl.ANY),
            pl.BlockSpec((1, d), lambda bi, i: (0, 0)),
            pl.BlockSpec((1, d), lambda bi, i: (0, 0)),
        ],
        out_specs=pl.BlockSpec((None, tm, d), lambda bi, i: (bi, i, 0)),
        scratch_shapes=[
            pltpu.VMEM((tm + 2 * HALO, d), bf16),
            pltpu.VMEM((2, tm + 2 * HALO, tf), f32),
            pltpu.VMEM((nbuf, d, tf), bf16),
            pltpu.VMEM((nbuf, d, tf), bf16),
            pltpu.VMEM((nbuf, tf, d), bf16),
            pltpu.SemaphoreType.DMA((3, nbuf)),
        ],
        compiler_params=_cparams(("arbitrary", "arbitrary")),
        name="conv_ffn",
    )(x, x, x, modr, w_up, conv_w, conv_b, w_down, ln_g, ln_b)


def _rope_perm():
    q = MLA_ROPE // 4
    return jnp.concatenate([jnp.arange(q, 2 * q), jnp.arange(0, q),
                            jnp.arange(3 * q, 4 * q), jnp.arange(2 * q, 3 * q)])


def _rope_tables(s, n_ctx):
    n_rows = s // GRID_W
    row = np.repeat(np.arange(n_rows), GRID_W).astype(np.float32)
    col = np.tile(np.arange(GRID_W), n_rows).astype(np.float32)
    n_freq = MLA_ROPE // 4
    inv = (np.float32(ROPE_THETA) ** (-np.arange(n_freq, dtype=np.float32) / np.float32(n_freq))).astype(np.float32)
    ar = row[:, None] * inv
    ac = col[:, None] * inv
    cos = np.concatenate([np.cos(ar), np.cos(ar), np.cos(ac), np.cos(ac)], axis=-1).astype(np.float32)
    sin = np.concatenate([-np.sin(ar), np.sin(ar), -np.sin(ac), np.sin(ac)], axis=-1).astype(np.float32)
    pad = np.zeros((s, LANES - MLA_ROPE), np.float32)
    ctx_c = np.concatenate([np.ones((n_ctx, MLA_ROPE), np.float32),
                            np.zeros((n_ctx, LANES - MLA_ROPE), np.float32)], axis=1)
    ck = np.concatenate([ctx_c, np.concatenate([cos, pad], axis=1)], axis=0)
    sk = np.concatenate([np.zeros((n_ctx, LANES), np.float32), np.concatenate([sin, pad], axis=1)], axis=0)
    return ck, sk, np.ascontiguousarray(cos.T), np.ascontiguousarray(sin.T)


def _wprep_kernel(wuq_ref, wukv_ref, wmq_ref, wmk_ref, wm_ref, wk_ref, wvt_ref, wq_ref, wkt_ref):
    dq = MLA_NOPE + MLA_ROPE
    g = MLA_ROPE // 4
    tq = wuq_ref[...].T
    for h in range(MLA_HEADS):
        wm_ref[h, 0:dq, :] = tq[h * dq:(h + 1) * dq].astype(bf16)
        rope = tq[h * dq + MLA_NOPE:(h + 1) * dq]
        for dst, src in enumerate((1, 0, 3, 2)):
            wm_ref[h, dq + dst * g:dq + (dst + 1) * g, :] = rope[src * g:(src + 1) * g].astype(bf16)
    tkv = wukv_ref[...].T
    dkv = MLA_NOPE + MLA_V
    for h in range(MLA_HEADS):
        wk_ref[:, h * MLA_NOPE:(h + 1) * MLA_NOPE] = wukv_ref[:, h * dkv:h * dkv + MLA_NOPE].astype(bf16)
        wvt_ref[h] = tkv[h * dkv + MLA_NOPE:(h + 1) * dkv].astype(bf16)
    for h in range(ML_HEADS):
        wq_ref[h] = wmq_ref[h].astype(bf16)
        wkt_ref[h] = wmk_ref[h].T.astype(bf16)


def _weight_prep(w_uq, w_ukv, w_mq, w_mk):
    return pl.pallas_call(
        _wprep_kernel,
        out_shape=(
            jax.ShapeDtypeStruct((MLA_HEADS, QK_PAD, Q_LORA), bf16),
            jax.ShapeDtypeStruct((KV_LORA, MLA_HEADS * MLA_NOPE), bf16),
            jax.ShapeDtypeStruct((MLA_HEADS, MLA_V, KV_LORA), bf16),
            jax.ShapeDtypeStruct((ML_HEADS, ML_V, ML_QK), bf16),
            jax.ShapeDtypeStruct((ML_HEADS, ML_QK, ML_V), bf16),
        ),
        compiler_params=pltpu.CompilerParams(vmem_limit_bytes=VMEM_LIMIT),
        name="weight_prep",
    )(w_uq, w_ukv, w_mq, w_mk)


def _pack_kernel(w_ref, krp_ref, g_ref, o_ref):
    o_ref[:, 0:OFF_U] = w_ref[:, 0:OFF_U].astype(bf16)
    o_ref[:, OFF_U:P_U] = krp_ref[...].astype(bf16)
    o_ref[:, P_U:P_G] = w_ref[:, OFF_U:OFF_G].astype(bf16)
    o_ref[:, P_G:P_COLS] = g_ref[...].astype(bf16)


def _pack_w_in(w_in):
    d = w_in.shape[0]
    z = lambda n: jnp.zeros((d, n), w_in.dtype)
    gates = []
    for gi in range(4):
        gates += [w_in[:, OFF_G + gi * ML_HEADS:OFF_G + (gi + 1) * ML_HEADS], z(GATE_STRIDE - ML_HEADS)]
    gates = jnp.concatenate(gates + [z(LANES - 4 * GATE_STRIDE)], axis=1)
    krp = w_in[:, OFF_KR:OFF_U][:, _rope_perm()]
    tr = min(ROW_TILE, d)
    assert d % tr == 0
    return pl.pallas_call(
        _pack_kernel,
        out_shape=jax.ShapeDtypeStruct((d, P_COLS), bf16),
        grid=(d // tr,),
        in_specs=[
            pl.BlockSpec((tr, w_in.shape[1]), lambda i: (i, 0)),
            pl.BlockSpec((tr, MLA_ROPE), lambda i: (i, 0)),
            pl.BlockSpec((tr, LANES), lambda i: (i, 0)),
        ],
        out_specs=pl.BlockSpec((tr, P_COLS), lambda i: (i, 0)),
        compiler_params=_cparams(("arbitrary",)),
        name="pack_w_in",
    )(w_in, krp, gates)


def _pack_b_gate(b_gate):
    parts = []
    for gi in range(4):
        parts += [b_gate[gi * ML_HEADS:(gi + 1) * ML_HEADS], jnp.zeros((GATE_STRIDE - ML_HEADS,), f32)]
    parts.append(jnp.zeros((LANES - 4 * GATE_STRIDE,), f32))
    return jnp.concatenate(parts)[None, :]


def kernel(x, c, ctx, c_ctx, w_ada, b_ada, w_in, g_q, w_uq, g_kv, w_ukv, ml_conv_w, ml_conv_b, w_mq, w_mk, b_gate, g_hn, w_out, ln1_g, ln1_b, w_up, ffn_conv_w, ffn_conv_b, w_down, ln2_g, ln2_b):
    b, s, d = x.shape
    n_ctx = ctx.shape[1]
    assert n_ctx == CHUNK and s % ROW_TILE == 0
    n_ctx_tiles = n_ctx // ROW_TILE
    layer = 0

    rows = -(-(b + 1) // SUBLANES) * SUBLANES
    cc = jnp.concatenate([c, c_ctx[None, :], jnp.zeros((rows - b - 1, d), f32)], axis=0)
    modr = _modulation(cc, w_ada[layer], b_ada[layer][None, :]).reshape(rows, N_MOD, d)

    ck, sk, ct, st = (jnp.asarray(tbl) for tbl in _rope_tables(s, n_ctx))

    wm, wk, wvt, w_q, w_kt = _weight_prep(w_uq[layer], w_ukv[layer], w_mq[layer], w_mk[layer])
    cq, kk, vt, ksq, g, gt, u, v, o = _in_projection(
        ctx, x, modr, _pack_w_in(w_in[layer]), g_q[layer][None, :], g_kv[layer][None, :], ck, sk,
        _pack_b_gate(b_gate[layer]), wk, wvt, n_ctx_tiles)

    qt = _q_prep(cq, ksq, wm, ct, st, n_ctx_tiles)
    a = _attention(qt, kk, vt)

    mq, mkt = _mlstm_projection(u, ml_conv_w[layer], ml_conv_b[layer][None, :], w_q, w_kt, n_ctx_tiles)
    hf, hb = _mlstm_scan(mq, mkt, v, g, gt)

    x1 = _out_projection(a, hf, hb, o, x, modr, g_hn[layer].reshape(1, ML_INNER), w_out[layer].astype(bf16),
                         ln1_g[layer][None, :], ln1_b[layer][None, :], n_ctx_tiles)
    return _conv_ffn(x1, modr, w_up[layer].astype(bf16), ffn_conv_w[layer], ffn_conv_b[layer][None, :],
                     w_down[layer].astype(bf16), ln2_g[layer][None, :], ln2_b[layer][None, :])
```

```python
import functools

import numpy as np
import jax
import jax.numpy as jnp
from jax import lax
from jax.experimental import pallas as pl
from jax.experimental.pallas import tpu as pltpu

f32 = jnp.float32
bf16 = jnp.bfloat16

GRID_W = 64
MLA_HEADS = 8
MLA_NOPE = 128
MLA_ROPE = 64
MLA_V = 128
Q_LORA = 512
KV_LORA = 256
ROPE_THETA = 10000.0
ML_HEADS = 4
ML_QK = 128
ML_V = 256
ML_INNER = ML_HEADS * ML_V
N_GATES = 4 * ML_HEADS
DEPTH = 1
OFF_CKV = Q_LORA
OFF_KR = Q_LORA + KV_LORA
OFF_U = OFF_KR + MLA_ROPE
OFF_V = OFF_U + ML_INNER
OFF_O = OFF_V + ML_INNER
OFF_G = OFF_O + ML_INNER
ALPHA = (2.0 * DEPTH) ** 0.25
EPS = 1e-6
N_MOD = 6

LANES = 128
SUBLANES = 8
MXU_DIM = 256
VMEM_LIMIT = 56 * 1024 * 1024

QK_PAD = 2 * LANES
GATE_STRIDE = SUBLANES
P_Q = 0
P_CKV = P_Q + Q_LORA
P_KR = P_CKV + KV_LORA
P_U = P_KR + LANES
P_V = P_U + ML_INNER
P_O = P_V + ML_INNER
P_G = P_O + ML_INNER
P_COLS = P_G + LANES
STAB_ROWS = 16
STAB_SLACK = 1.02
L_MIN = 2.0 ** -100

ROW_TILE = 256
CHUNK = ROW_TILE
ATT_TQ = 1024
ATT_TK_MAX = 2816
OUT_TM = 512
FFN_TM = 512
FFN_TF = 512
HALO = SUBLANES


def _cparams(sem):
    return pltpu.CompilerParams(dimension_semantics=sem, vmem_limit_bytes=VMEM_LIMIT)


def _silu(x):
    return x * jax.nn.sigmoid(x)


def _dot(a, b):
    return jnp.dot(a, b, preferred_element_type=f32)


def _dot_nt(a, b):
    return lax.dot_general(a, b, (((1,), (1,)), ((), ())), preferred_element_type=f32)


def _layer_norm_rows(r, g, b):
    mu = jnp.mean(r, axis=-1, keepdims=True)
    d = r - mu
    var = jnp.mean(d * d, axis=-1, keepdims=True)
    return d * lax.rsqrt(var + EPS) * g + b


def _mod_kernel(c_ref, w_ref, b_ref, o_ref):
    s = _silu(c_ref[...])
    o_ref[...] = _dot(s.astype(bf16), w_ref[...].astype(bf16)) + b_ref[...]


def _modulation(cc, w_ada, b_ada):
    rows, d = cc.shape
    n = w_ada.shape[1]
    tn = min(n, 1536)
    assert n % tn == 0
    return pl.pallas_call(
        _mod_kernel,
        out_shape=jax.ShapeDtypeStruct((rows, n), f32),
        grid=(n // tn,),
        in_specs=[
            pl.BlockSpec((rows, d), lambda j: (0, 0)),
            pl.BlockSpec((d, tn), lambda j: (0, j)),
            pl.BlockSpec((1, tn), lambda j: (0, j)),
        ],
        out_specs=pl.BlockSpec((rows, tn), lambda j: (0, j)),
        compiler_params=_cparams(("arbitrary",)),
        name="adaln_mod",
    )(cc, w_ada, b_ada)


def _inproj_kernel(ctx_ref, x_ref, mod_ref, w_ref, gq_ref, gkv_ref, ck_ref, sk_ref, bg_ref, wk_ref, wvt_ref,
                   wm_ref, ct_ref, st_ref,
                   qt_ref, qsq_ref, k_ref, vt_ref, ksq_ref, g_ref, gt_ref, u_ref, v_ref, o_ref, h_ref, *,
                   n_ctx_tiles, q_scale):
    shift = mod_ref[0:1, :]
    scale = mod_ref[1:2, :]
    is_ctx = pl.program_id(1) < n_ctx_tiles
    rows = jnp.where(is_ctx, ctx_ref[...], x_ref[...])
    h_ref[...] = (rows * (1.0 + scale) + shift).astype(bf16)

    def sect(a, n):
        return _dot(h_ref[...], w_ref[:, a:a + n])

    zq = sect(P_Q, Q_LORA)
    cq = zq * lax.rsqrt(jnp.mean(zq * zq, axis=-1, keepdims=True) + EPS) * gq_ref[...]
    _q_outputs(cq.astype(bf16), wm_ref, ct_ref, st_ref, qt_ref, qsq_ref, q_scale)
    zkv = sect(P_CKV, KV_LORA)
    ckv = zkv * lax.rsqrt(jnp.mean(zkv * zkv, axis=-1, keepdims=True) + EPS) * gkv_ref[...]
    zkr = sect(P_KR, LANES)
    kr = zkr * ck_ref[...] + pltpu.roll(zkr, LANES - MLA_ROPE, 1) * sk_ref[...]
    _kv_outputs(ckv.astype(bf16), kr.astype(bf16), wk_ref, wvt_ref, k_ref, vt_ref, ksq_ref)
    g = sect(P_G, LANES) + bg_ref[...]
    tm = g.shape[0]
    lf = jax.nn.log_sigmoid(g)
    tri = jnp.where(lax.broadcasted_iota(jnp.int32, (tm, tm), 0)
                    >= lax.broadcasted_iota(jnp.int32, (tm, tm), 1), 1.0, 0.0).astype(bf16)
    hi = lf.astype(bf16)
    r1 = lf - hi.astype(f32)
    mid = r1.astype(bf16)
    lo = (r1 - mid.astype(f32)).astype(bf16)
    pre = _dot(tri, hi) + _dot(tri, mid) + _dot(tri, lo)
    suf = pre[tm - 1:tm, :] - pre + lf
    grp = lax.broadcasted_iota(jnp.int32, (1, LANES), 1) // GATE_STRIDE
    g2 = jnp.where(grp == 1, pre, jnp.where(grp == 3, suf, g))
    g_ref[...] = g2
    gt_ref[...] = g2.T[0:4 * GATE_STRIDE, :]
    u_ref[...] = sect(P_U, ML_INNER).astype(bf16)
    v_ref[...] = sect(P_V, ML_INNER).astype(bf16)
    o_ref[...] = sect(P_O, ML_INNER).astype(bf16)


def _in_projection(ctx, x, modr, w_in_p, g_q, g_kv, ck, sk, bg, wk, wvt, wm, ct, st, n_ctx_tiles):
    b, s, d = x.shape
    tm = ROW_TILE
    t = s + n_ctx_tiles * tm
    nt = t // tm
    n_batch = b
    q_scale = float((MLA_NOPE + MLA_ROPE) ** -0.5 * 1.4426950408889634)

    def mod_idx(bi, i):
        return (jnp.where(i < n_ctx_tiles, n_batch, bi), 0, 0)

    lat = lambda i: jnp.maximum(i - n_ctx_tiles, 0)
    row = lambda w: pl.BlockSpec((None, tm, w), lambda bi, i: (bi, i, 0))
    const = lambda r, c: pl.BlockSpec((r, c), lambda bi, i: (0, 0))
    out_shapes = (
        jax.ShapeDtypeStruct((b, MLA_HEADS, QK_PAD, s), bf16),
        jax.ShapeDtypeStruct((b, MLA_HEADS, s), f32),
        jax.ShapeDtypeStruct((b, MLA_HEADS, t, QK_PAD), bf16),
        jax.ShapeDtypeStruct((b, MLA_HEADS, MLA_V, t), bf16),
        jax.ShapeDtypeStruct((b, MLA_HEADS, t), f32),
        jax.ShapeDtypeStruct((b, t, LANES), f32),
        jax.ShapeDtypeStruct((b, nt, 4 * GATE_STRIDE, tm), f32),
        jax.ShapeDtypeStruct((b, t, ML_INNER), bf16),
        jax.ShapeDtypeStruct((b, t, ML_INNER), bf16),
        jax.ShapeDtypeStruct((b, s, ML_INNER), bf16),
    )
    out_specs = (
        pl.BlockSpec((None, MLA_HEADS, QK_PAD, tm), lambda bi, i: (bi, 0, 0, lat(i))),
        pl.BlockSpec((None, MLA_HEADS, tm), lambda bi, i: (bi, 0, lat(i))),
        pl.BlockSpec((None, MLA_HEADS, tm, QK_PAD), lambda bi, i: (bi, 0, i, 0)),
        pl.BlockSpec((None, MLA_HEADS, MLA_V, tm), lambda bi, i: (bi, 0, 0, i)),
        pl.BlockSpec((None, MLA_HEADS, tm), lambda bi, i: (bi, 0, i)),
        row(LANES),
        pl.BlockSpec((None, None, 4 * GATE_STRIDE, tm), lambda bi, i: (bi, i, 0, 0)),
        row(ML_INNER), row(ML_INNER),
        pl.BlockSpec((None, tm, ML_INNER), lambda bi, i: (bi, lat(i), 0)),
    )
    return pl.pallas_call(
        functools.partial(_inproj_kernel, n_ctx_tiles=n_ctx_tiles, q_scale=q_scale),
        out_shape=out_shapes,
        grid=(b, nt),
        in_specs=[
            pl.BlockSpec((None, tm, d), lambda bi, i: (bi, jnp.minimum(i, n_ctx_tiles - 1), 0)),
            pl.BlockSpec((None, tm, d), lambda bi, i: (bi, jnp.maximum(i - n_ctx_tiles, 0), 0)),
            pl.BlockSpec((None, N_MOD, d), mod_idx),
            pl.BlockSpec((d, P_COLS), lambda bi, i: (0, 0), pipeline_mode=pl.Buffered(1)),
            const(1, Q_LORA),
            const(1, KV_LORA),
            pl.BlockSpec((tm, LANES), lambda bi, i: (i, 0)),
            pl.BlockSpec((tm, LANES), lambda bi, i: (i, 0)),
            const(1, LANES),
            pl.BlockSpec((KV_LORA, MLA_HEADS * MLA_NOPE), lambda bi, i: (0, 0), pipeline_mode=pl.Buffered(1)),
            pl.BlockSpec((MLA_HEADS, MLA_V, KV_LORA), lambda bi, i: (0, 0, 0), pipeline_mode=pl.Buffered(1)),
            pl.BlockSpec((MLA_HEADS, QK_PAD, Q_LORA), lambda bi, i: (0, 0, 0), pipeline_mode=pl.Buffered(1)),
            pl.BlockSpec((MLA_ROPE, tm), lambda bi, i: (0, lat(i))),
            pl.BlockSpec((MLA_ROPE, tm), lambda bi, i: (0, lat(i))),
        ],
        out_specs=out_specs,
        scratch_shapes=[pltpu.VMEM((tm, d), bf16)],
        compiler_params=_cparams(("arbitrary", "arbitrary")),
        name="in_projection",
    )(ctx, x, modr, w_in_p, g_q, g_kv, ck, sk, bg, wk, wvt, wm, ct, st)


def _kv_outputs(ckv, kr, wk_ref, wvt_ref, k_ref, vt_ref, ksq_ref):
    kn = _dot(ckv, wk_ref[...]).astype(bf16)
    knf = kn.astype(f32)
    krf = kr.astype(f32)
    hsel = lax.broadcasted_iota(jnp.int32, (MLA_HEADS, MLA_HEADS * MLA_NOPE), 0)
    csel = lax.broadcasted_iota(jnp.int32, (MLA_HEADS, MLA_HEADS * MLA_NOPE), 1) // MLA_NOPE
    sel = jnp.where(hsel == csel, 1.0, 0.0).astype(bf16)
    ksq_ref[...] = (_dot_nt(sel, (knf * knf).astype(bf16))
                    + _dot_nt(jnp.ones((MLA_HEADS, LANES), bf16), (krf * krf).astype(bf16)))
    lane = lax.broadcasted_iota(jnp.int32, kr.shape, 1)
    stab = jnp.logical_and(lane >= MLA_ROPE, lane < MLA_ROPE + STAB_ROWS)
    kr_aug = jnp.where(stab, 1.0, krf).astype(bf16)
    for h in range(MLA_HEADS):
        k_ref[h, :, 0:MLA_NOPE] = kn[:, h * MLA_NOPE:(h + 1) * MLA_NOPE]
        k_ref[h, :, MLA_NOPE:QK_PAD] = kr_aug
        vt_ref[h] = _dot_nt(wvt_ref[h], ckv).astype(bf16)


def _q_outputs(cq, wm_ref, ct_ref, st_ref, qt_ref, qsq_ref, scale):
    ct = ct_ref[...]
    st = st_ref[...]
    tm = cq.shape[0]
    r0, r1 = MLA_NOPE, MLA_NOPE + MLA_ROPE
    for h in range(MLA_HEADS):
        qm = _dot_nt(wm_ref[h], cq)
        qn = (qm[0:r0] * scale).astype(bf16)
        qr = ((qm[r0:r1] * ct + qm[r1:QK_PAD] * st) * scale).astype(bf16)
        qnf = qn.astype(f32)
        qrf = qr.astype(f32)
        qsq_ref[h:h + 1, :] = (jnp.sum(qnf * qnf, axis=0, keepdims=True)
                               + jnp.sum(qrf * qrf, axis=0, keepdims=True))
        qt_ref[h, 0:r0, :] = qn
        qt_ref[h, r0:r1, :] = qr
        qt_ref[h, r1:QK_PAD, :] = jnp.zeros((QK_PAD - r1, tm), bf16)


def _attn_bound_kernel(qt_ref, qsq_ref, ksq_ref, k_ref, vt_ref, o_ref, l_ref, *, tq, tk, nk):
    kmax2 = jnp.max(ksq_ref[...], axis=1, keepdims=True)
    r1 = MLA_NOPE + MLA_ROPE
    bound = jnp.sqrt(qsq_ref[...] * kmax2) * STAB_SLACK
    stab = jnp.broadcast_to(bound * (-1.0 / STAB_ROWS), (STAB_ROWS, tq)).astype(bf16)
    qt = jnp.concatenate([qt_ref[0:r1, :], stab,
                          jnp.zeros((QK_PAD - r1 - STAB_ROWS, tq), bf16)], axis=0)
    l = jnp.zeros((1, tq), f32)
    acc = jnp.zeros((MLA_V, tq), f32)
    s = _dot(k_ref[0:tk, :], qt)
    for j in range(nk):
        s_next = _dot(k_ref[(j + 1) * tk:(j + 2) * tk, :], qt) if j + 1 < nk else None
        p = jnp.exp2(s)
        l = l + jnp.sum(p, axis=0, keepdims=True)
        acc = acc + _dot(vt_ref[:, j * tk:(j + 1) * tk], p.astype(bf16))
        s = s_next
    o_ref[...] = (acc / l).T.astype(o_ref.dtype)
    l_ref[...] = l


def _attn_online_kernel(qt_ref, k_ref, vt_ref, o_ref, *, tk, nk):
    qt = qt_ref[...]
    tq = qt.shape[1]
    m = jnp.full((1, tq), -jnp.inf, f32)
    l = jnp.zeros((1, tq), f32)
    acc = jnp.zeros((MLA_V, tq), f32)
    for j in range(nk):
        s = _dot(k_ref[j * tk:(j + 1) * tk, :], qt)
        m_new = jnp.maximum(m, jnp.max(s, axis=0, keepdims=True))
        alpha = jnp.exp2(m - m_new)
        p = jnp.exp2(s - m_new)
        l = alpha * l + jnp.sum(p, axis=0, keepdims=True)
        acc = alpha * acc + _dot(vt_ref[:, j * tk:(j + 1) * tk], p.astype(bf16))
        m = m_new
    o_ref[...] = (acc / l).T.astype(o_ref.dtype)


def _attention(qt, qsq, ksq, k, vt):
    b, hh, _, s = qt.shape
    t = k.shape[2]
    tq = min(ATT_TQ, s)
    tk = MXU_DIM
    for cand in range(MXU_DIM, ATT_TK_MAX + 1, MXU_DIM):
        if t % cand == 0:
            tk = cand
    assert s % tq == 0 and t % tk == 0
    in_specs = [
        pl.BlockSpec((None, None, QK_PAD, tq), lambda bi, h, i: (bi, h, 0, i)),
        pl.BlockSpec((None, None, t, QK_PAD), lambda bi, h, i: (bi, h, 0, 0)),
        pl.BlockSpec((None, None, MLA_V, t), lambda bi, h, i: (bi, h, 0, 0)),
    ]
    a_shape = jax.ShapeDtypeStruct((b, s, hh * MLA_V), bf16)
    a_spec = pl.BlockSpec((None, tq, MLA_V), lambda bi, h, i: (bi, i, h))
    sem = ("arbitrary", "arbitrary", "arbitrary")
    a, l = pl.pallas_call(
        functools.partial(_attn_bound_kernel, tq=tq, tk=tk, nk=t // tk),
        out_shape=(a_shape, jax.ShapeDtypeStruct((b, hh, 1, s), f32)),
        grid=(b, hh, s // tq),
        in_specs=[
            in_specs[0],
            pl.BlockSpec((None, None, 1, tq), lambda bi, h, i: (bi, h, 0, i)),
            pl.BlockSpec((None, None, 1, t), lambda bi, h, i: (bi, h, 0, 0)),
            in_specs[1],
            in_specs[2],
        ],
        out_specs=(a_spec, pl.BlockSpec((None, None, 1, tq), lambda bi, h, i: (bi, h, 0, i))),
        compiler_params=_cparams(sem),
        name="attention_bound",
    )(qt, qsq[:, :, None, :], ksq[:, :, None, :], k, vt)

    def online(_):
        return pl.pallas_call(
            functools.partial(_attn_online_kernel, tk=tk, nk=t // tk),
            out_shape=a_shape,
            grid=(b, hh, s // tq),
            in_specs=in_specs,
            out_specs=a_spec,
            compiler_params=_cparams(sem),
            name="attention_online",
        )(qt, k, vt)

    ok = jnp.min(l) >= L_MIN
    return lax.cond(ok, lambda _: a, online, None)


def _mproj_kernel(u_ref, up_ref, un_ref, cw_ref, cb_ref, wq_ref, wkt_ref, q_ref, kt_ref, *,
                  n_ctx_tiles, n_tiles):
    i = pl.program_id(1)
    u = u_ref[...].astype(f32)
    tm = u.shape[0]
    first = jnp.logical_or(i == 0, i == n_ctx_tiles)
    last = jnp.logical_or(i == n_ctx_tiles - 1, i == n_tiles - 1)
    prev_row = jnp.where(first, 0.0, up_ref[HALO - 1:HALO, :].astype(f32))
    next_row = jnp.where(last, 0.0, un_ref[0:1, :].astype(f32))
    ridx = lax.broadcasted_iota(jnp.int32, (tm, 1), 0)
    u_prev = jnp.where(ridx == 0, prev_row, pltpu.roll(u, 1, 0))
    u_next = jnp.where(ridx == tm - 1, next_row, pltpu.roll(u, tm - 1, 0))
    y = cw_ref[0:1, :] * u_prev + cw_ref[1:2, :] * u + cw_ref[2:3, :] * u_next + cb_ref[...]
    act = _silu(y).astype(bf16)
    for h in range(ML_HEADS):
        a_h = act[:, h * ML_V:(h + 1) * ML_V]
        q_ref[:, h * ML_QK:(h + 1) * ML_QK] = (_dot(a_h, wq_ref[h]) * float(ML_QK ** -0.5)).astype(bf16)
        kt_ref[h * ML_QK:(h + 1) * ML_QK, :] = _dot_nt(wkt_ref[h], a_h).astype(bf16)


def _mlstm_projection(u, conv_w, conv_b, w_q, w_kt, n_ctx_tiles):
    b, t, _ = u.shape
    tm = ROW_TILE
    nt = t // tm
    hb = tm // HALO
    return pl.pallas_call(
        functools.partial(_mproj_kernel, n_ctx_tiles=n_ctx_tiles, n_tiles=nt),
        out_shape=(jax.ShapeDtypeStruct((b, t, ML_HEADS * ML_QK), bf16),
                   jax.ShapeDtypeStruct((b, nt, ML_HEADS * ML_QK, tm), bf16)),
        grid=(b, nt),
        in_specs=[
            pl.BlockSpec((None, tm, ML_INNER), lambda bi, i: (bi, i, 0)),
            pl.BlockSpec((None, HALO, ML_INNER), lambda bi, i: (bi, jnp.maximum(i * hb - 1, 0), 0)),
            pl.BlockSpec((None, HALO, ML_INNER),
                         lambda bi, i: (bi, jnp.minimum((i + 1) * hb, nt * hb - 1), 0)),
            pl.BlockSpec((3, ML_INNER), lambda bi, i: (0, 0)),
            pl.BlockSpec((1, ML_INNER), lambda bi, i: (0, 0)),
            pl.BlockSpec((ML_HEADS, ML_V, ML_QK), lambda bi, i: (0, 0, 0)),
            pl.BlockSpec((ML_HEADS, ML_QK, ML_V), lambda bi, i: (0, 0, 0)),
        ],
        out_specs=(pl.BlockSpec((None, tm, ML_HEADS * ML_QK), lambda bi, i: (bi, i, 0)),
                   pl.BlockSpec((None, None, ML_HEADS * ML_QK, tm), lambda bi, i: (bi, i, 0, 0))),
        compiler_params=_cparams(("arbitrary", "arbitrary")),
        name="mlstm_projection",
    )(u, u, u, conv_w, conv_b, w_q, w_kt)


def _scan_kernel(qf_ref, ktf_ref, vf_ref, gf_ref, gtf_ref, qb_ref, ktb_ref, vb_ref, gb_ref, gtb_ref,
                 hf_ref, hb_ref, c_scr, m_scr):
    j = pl.program_id(1)
    L = qf_ref.shape[0]
    GS = GATE_STRIDE

    rr = lax.broadcasted_iota(jnp.int32, (L, L), 0)
    cc = lax.broadcasted_iota(jnp.int32, (L, L), 1)
    visible = (rr >= cc, rr <= cc)
    dirs = ((qf_ref, ktf_ref, vf_ref, gf_ref, gtf_ref, hf_ref), (qb_ref, ktb_ref, vb_ref, gb_ref, gtb_ref, hb_ref))
    ones_col = jnp.where(lax.broadcasted_iota(jnp.int32, (L, LANES), 1) == 0, 1.0, 0.0).astype(bf16)

    chains = []
    for d, (q_ref, kt_ref, v_ref, g_ref, gt_ref, h_ref) in enumerate(dirs):
        g = g_ref[...]
        gt = gt_ref[...]
        for h in range(ML_HEADS):
            ci = 2 * d * GS + h
            cf = ci + GS
            bc_r = gt[cf:cf + 1, :]
            chains.append(dict(
                d=d, h=h, st=d * ML_HEADS + h, h_ref=h_ref,
                q=q_ref[:, h * ML_QK:(h + 1) * ML_QK],
                kt=kt_ref[h * ML_QK:(h + 1) * ML_QK, :],
                v=jnp.concatenate([v_ref[:, h * ML_V:(h + 1) * ML_V], ones_col], axis=1),
                bc_c=g[:, cf:cf + 1],
                ig_r=gt[ci:ci + 1, :], bc_r=bc_r,
                a=bc_r[:, L - 1:L] if d == 0 else bc_r[:, 0:1],
            ))

    def step(first):
        for c in chains:
            c["s"] = _dot(c["q"], c["kt"])
        for c in chains:
            if not first:
                c["c_st"] = c_scr[c["st"]]
                c["m_st"] = m_scr[c["st"]]
                c["qc"] = _dot(c["q"], c["c_st"].astype(bf16))
            else:
                c["m_st"] = 0.0

        for c in chains:
            drow = jnp.where(visible[c["d"]], c["ig_r"] - c["bc_r"], -jnp.inf)
            mm = jnp.maximum(c["m_st"], jnp.max(drow, axis=1, keepdims=True))
            c["w_inter"] = jnp.exp(c["m_st"] - mm)
            c["floor"] = jnp.exp(-(c["bc_c"] + mm))
            c["p"] = jnp.exp(drow - mm) * c["s"]

        for c in chains:
            nd = _dot(c["p"].astype(bf16), c["v"])
            if not first:
                nd = nd + c["w_inter"] * c["qc"]
            den = nd[:, ML_V:ML_V + 1]
            h = c["h"]
            hv = nd[:, 0:ML_V] / jnp.maximum(jnp.abs(den), c["floor"])
            c["h_ref"][:, h * ML_V:(h + 1) * ML_V] = hv.astype(c["h_ref"].dtype)

        for c in chains:
            a, m_st = c["a"], c["m_st"]
            g_log_r = a - c["bc_r"] + c["ig_r"]
            m_new = jnp.maximum(a + m_st, jnp.max(g_log_r, axis=1, keepdims=True))
            kwt = c["kt"].astype(f32) * jnp.exp(g_log_r - m_new)
            c_new = _dot(kwt.astype(bf16), c["v"])
            if not first:
                c_new = c_new + jnp.exp(a + m_st - m_new) * c["c_st"]
            c_scr[c["st"]] = c_new
            m_scr[c["st"]] = m_new

    @pl.when(j == 0)
    def _():
        step(True)

    @pl.when(j != 0)
    def _():
        step(False)


def _mlstm_scan(q, kt, v, g, gt):
    b, t, _ = q.shape
    L = CHUNK
    nc = t // L

    fwd = lambda j: j
    bwd = lambda j: jnp.where(j == 0, 0, nc - j)

    def specs(cidx):
        return [
            pl.BlockSpec((None, L, ML_HEADS * ML_QK), lambda bi, j: (bi, cidx(j), 0)),
            pl.BlockSpec((None, None, ML_HEADS * ML_QK, L), lambda bi, j: (bi, cidx(j), 0, 0)),
            pl.BlockSpec((None, L, ML_INNER), lambda bi, j: (bi, cidx(j), 0)),
            pl.BlockSpec((None, L, LANES), lambda bi, j: (bi, cidx(j), 0)),
            pl.BlockSpec((None, None, 4 * GATE_STRIDE, L), lambda bi, j: (bi, cidx(j), 0, 0)),
        ]

    n_chain = 2 * ML_HEADS
    h_shape = jax.ShapeDtypeStruct((b, t - L, ML_INNER), bf16)
    return pl.pallas_call(
        _scan_kernel,
        out_shape=(h_shape, h_shape),
        grid=(b, nc),
        in_specs=specs(fwd) + specs(bwd),
        out_specs=(pl.BlockSpec((None, L, ML_INNER), lambda bi, j: (bi, jnp.maximum(j - 1, 0), 0)),
                   pl.BlockSpec((None, L, ML_INNER), lambda bi, j: (bi, nc - 1 - jnp.maximum(j, 1), 0))),
        scratch_shapes=[
            pltpu.VMEM((n_chain, ML_QK, ML_V + LANES), f32),
            pltpu.VMEM((n_chain, 1, 1), f32),
        ],
        compiler_params=_cparams(("arbitrary", "arbitrary")),
        name="mlstm_scan",
    )(q, kt, v, g, gt, q, kt, v, g, gt)


def _outproj_kernel(a_ref, hf_ref, hb_ref, o_ref, x_ref, mod_ref, ghn_ref, w_ref, lg_ref, lb_ref,
                    out_ref):
    ghn = ghn_ref[...]
    gate = mod_ref[2:3, :]
    tm = x_ref.shape[0]
    halves = [slice(r, r + tm // 2) for r in (0, tm // 2)]
    lhs = []
    for rows in halves:
        hm = hf_ref[rows, :].astype(f32) + hb_ref[rows, :].astype(f32)
        og = jax.nn.sigmoid(o_ref[rows, :].astype(f32))
        parts = [a_ref[rows, :]]
        for h in range(ML_HEADS):
            sl = slice(h * ML_V, (h + 1) * ML_V)
            seg = hm[:, sl]
            mu = jnp.mean(seg, axis=-1, keepdims=True)
            dd = seg - mu
            var = jnp.mean(dd * dd, axis=-1, keepdims=True)
            parts.append((og[:, sl] * (dd * lax.rsqrt(var + EPS) * ghn[:, sl])).astype(bf16))
        lhs.append(jnp.concatenate(parts, axis=1))
    ys = [_dot(l, w_ref[...]) for l in lhs]
    for rows, y in zip(halves, ys):
        r = ALPHA * x_ref[rows, :] + gate * y
        out_ref[rows, :] = _layer_norm_rows(r, lg_ref[...], lb_ref[...])


def _out_projection(a, hf, hb, o, x, modr, g_hn, w_out, ln_g, ln_b, n_ctx_tiles):
    b, s, d = x.shape
    tm = min(OUT_TM, s)
    assert s % tm == 0
    mix = w_out.shape[0]
    return pl.pallas_call(
        _outproj_kernel,
        out_shape=jax.ShapeDtypeStruct((b, s, d), f32),
        grid=(b, s // tm),
        in_specs=[
            pl.BlockSpec((None, tm, MLA_HEADS * MLA_V), lambda bi, i: (bi, i, 0)),
            pl.BlockSpec((None, tm, ML_INNER), lambda bi, i: (bi, i, 0)),
            pl.BlockSpec((None, tm, ML_INNER), lambda bi, i: (bi, i, 0)),
            pl.BlockSpec((None, tm, ML_INNER), lambda bi, i: (bi, i, 0)),
            pl.BlockSpec((None, tm, d), lambda bi, i: (bi, i, 0)),
            pl.BlockSpec((None, N_MOD, d), lambda bi, i: (bi, 0, 0)),
            pl.BlockSpec((1, ML_INNER), lambda bi, i: (0, 0)),
            pl.BlockSpec((mix, d), lambda bi, i: (0, 0)),
            pl.BlockSpec((1, d), lambda bi, i: (0, 0)),
            pl.BlockSpec((1, d), lambda bi, i: (0, 0)),
        ],
        out_specs=pl.BlockSpec((None, tm, d), lambda bi, i: (bi, i, 0)),
        compiler_params=_cparams(("arbitrary", "arbitrary")),
        name="out_projection",
    )(a, hf, hb, o, x, modr, g_hn, w_out, ln_g, ln_b)


def _ffn_kernel(x_ref, xp_ref, xn_ref, mod_ref, wup_hbm, cw_ref, cb_ref, wd_hbm, lg_ref, lb_ref,
                out_ref, hs_ref, gs_ref, wg_buf, wu_buf, wd_buf, sem, *, n_row_tiles, nf, tf, nbuf):
    i = pl.program_id(1)
    step = pl.program_id(0) * n_row_tiles + i
    n_steps = pl.num_programs(0) * n_row_tiles
    tm = x_ref.shape[0]
    dff = nf * tf

    def weight_copies(j):
        slot = j % nbuf
        return (
            pltpu.make_async_copy(wup_hbm.at[:, pl.ds(j * tf, tf)], wg_buf.at[slot], sem.at[0, slot]),
            pltpu.make_async_copy(wup_hbm.at[:, pl.ds(dff + j * tf, tf)], wu_buf.at[slot], sem.at[1, slot]),
            pltpu.make_async_copy(wd_hbm.at[pl.ds(j * tf, tf), :], wd_buf.at[slot], sem.at[2, slot]),
        )

    @pl.when(step == 0)
    def _():
        for c in weight_copies(0):
            c.start()

    shift = mod_ref[3:4, :]
    scale = mod_ref[4:5, :]
    hs_ref[0:HALO, :] = (xp_ref[...] * (1.0 + scale) + shift).astype(bf16)
    hs_ref[HALO:HALO + tm, :] = (x_ref[...] * (1.0 + scale) + shift).astype(bf16)
    hs_ref[HALO + tm:2 * HALO + tm, :] = (xn_ref[...] * (1.0 + scale) + shift).astype(bf16)
    ridx = lax.broadcasted_iota(jnp.int32, (tm, 1), 0)
    first_row = jnp.logical_and(ridx == 0, i == 0)
    last_row = jnp.logical_and(ridx == tm - 1, i == n_row_tiles - 1)

    for j in range(nf):
        for c in weight_copies((j + 1) % nf):
            c.start()
        for c in weight_copies(j):
            c.wait()
        slot = j % nbuf
        gs = gs_ref.at[j % 2]
        cols = slice(j * tf, (j + 1) * tf)
        gs[...] = _dot(hs_ref[...], wg_buf[slot])
        g_prev = jnp.where(first_row, 0.0, gs[HALO - 1:HALO - 1 + tm, :])
        g_next = jnp.where(last_row, 0.0, gs[HALO + 1:HALO + 1 + tm, :])
        gate = (cw_ref[0:1, cols] * g_prev + cw_ref[1:2, cols] * gs[HALO:HALO + tm, :]
                + cw_ref[2:3, cols] * g_next + cb_ref[:, cols])
        up = _dot(hs_ref[HALO:HALO + tm, :], wu_buf[slot])
        act = (_silu(gate) * up).astype(bf16)
        contrib = _dot(act, wd_buf[slot])
        if j == 0:
            out_ref[...] = contrib
        else:
            out_ref[...] += contrib

    r = ALPHA * x_ref[...] + mod_ref[5:6, :] * out_ref[...]
    out_ref[...] = _layer_norm_rows(r, lg_ref[...], lb_ref[...])

    @pl.when(step == n_steps - 1)
    def _():
        for c in weight_copies(0):
            c.wait()


def _conv_ffn(x, modr, w_up, conv_w, conv_b, w_down, ln_g, ln_b):
    b, s, d = x.shape
    dff = w_down.shape[0]
    tm = min(FFN_TM, s)
    tf = min(FFN_TF, dff)
    assert s % tm == 0 and dff % tf == 0
    nr = s // tm
    nf = dff // tf
    assert nf >= 2
    nbuf = next(n for n in range(2, nf + 1) if (nf - 1) % n != 0)
    hb = tm // HALO
    return pl.pallas_call(
        functools.partial(_ffn_kernel, n_row_tiles=nr, nf=nf, tf=tf, nbuf=nbuf),
        out_shape=jax.ShapeDtypeStruct((b, s, d), f32),
        grid=(b, nr),
        in_specs=[
            pl.BlockSpec((None, tm, d), lambda bi, i: (bi, i, 0)),
            pl.BlockSpec((None, HALO, d), lambda bi, i: (bi, jnp.maximum(i * hb - 1, 0), 0)),
            pl.BlockSpec((None, HALO, d), lambda bi, i: (bi, jnp.minimum((i + 1) * hb, nr * hb - 1), 0)),
            pl.BlockSpec((None, N_MOD, d), lambda bi, i: (bi, 0, 0)),
            pl.BlockSpec(memory_space=pl.ANY),
            pl.BlockSpec((3, dff), lambda bi, i: (0, 0)),
            pl.BlockSpec((1, dff), lambda bi, i: (0, 0)),
            pl.BlockSpec(memory_space=pl.ANY),
            pl.BlockSpec((1, d), lambda bi, i: (0, 0)),
            pl.BlockSpec((1, d), lambda bi, i: (0, 0)),
        ],
        out_specs=pl.BlockSpec((None, tm, d), lambda bi, i: (bi, i, 0)),
        scratch_shapes=[
            pltpu.VMEM((tm + 2 * HALO, d), bf16),
            pltpu.VMEM((2, tm + 2 * HALO, tf), f32),
            pltpu.VMEM((nbuf, d, tf), bf16),
            pltpu.VMEM((nbuf, d, tf), bf16),
            pltpu.VMEM((nbuf, tf, d), bf16),
            pltpu.SemaphoreType.DMA((3, nbuf)),
        ],
        compiler_params=_cparams(("arbitrary", "arbitrary")),
        name="conv_ffn",
    )(x, x, x, modr, w_up, conv_w, conv_b, w_down, ln_g, ln_b)


def _rope_perm():
    q = MLA_ROPE // 4
    return jnp.concatenate([jnp.arange(q, 2 * q), jnp.arange(0, q),
                            jnp.arange(3 * q, 4 * q), jnp.arange(2 * q, 3 * q)])


def _rope_tables(s, n_ctx):
    n_rows = s // GRID_W
    row = np.repeat(np.arange(n_rows), GRID_W).astype(np.float32)
    col = np.tile(np.arange(GRID_W), n_rows).astype(np.float32)
    n_freq = MLA_ROPE // 4
    inv = (np.float32(ROPE_THETA) ** (-np.arange(n_freq, dtype=np.float32) / np.float32(n_freq))).astype(np.float32)
    ar = row[:, None] * inv
    ac = col[:, None] * inv
    cos = np.concatenate([np.cos(ar), np.cos(ar), np.cos(ac), np.cos(ac)], axis=-1).astype(np.float32)
    sin = np.concatenate([-np.sin(ar), np.sin(ar), -np.sin(ac), np.sin(ac)], axis=-1).astype(np.float32)
    pad = np.zeros((s, LANES - MLA_ROPE), np.float32)
    ctx_c = np.concatenate([np.ones((n_ctx, MLA_ROPE), np.float32),
                            np.zeros((n_ctx, LANES - MLA_ROPE), np.float32)], axis=1)
    ck = np.concatenate([ctx_c, np.concatenate([cos, pad], axis=1)], axis=0)
    sk = np.concatenate([np.zeros((n_ctx, LANES), np.float32), np.concatenate([sin, pad], axis=1)], axis=0)
    return ck, sk, np.ascontiguousarray(cos.T), np.ascontiguousarray(sin.T)


def _wprep_kernel(wuq_ref, wukv_ref, wmq_ref, wmk_ref, wm_ref, wk_ref, wvt_ref, wq_ref, wkt_ref):
    dq = MLA_NOPE + MLA_ROPE
    g = MLA_ROPE // 4
    tq = wuq_ref[...].T
    for h in range(MLA_HEADS):
        wm_ref[h, 0:dq, :] = tq[h * dq:(h + 1) * dq].astype(bf16)
        rope = tq[h * dq + MLA_NOPE:(h + 1) * dq]
        for dst, src in enumerate((1, 0, 3, 2)):
            wm_ref[h, dq + dst * g:dq + (dst + 1) * g, :] = rope[src * g:(src + 1) * g].astype(bf16)
    tkv = wukv_ref[...].T
    dkv = MLA_NOPE + MLA_V
    for h in range(MLA_HEADS):
        wk_ref[:, h * MLA_NOPE:(h + 1) * MLA_NOPE] = wukv_ref[:, h * dkv:h * dkv + MLA_NOPE].astype(bf16)
        wvt_ref[h] = tkv[h * dkv + MLA_NOPE:(h + 1) * dkv].astype(bf16)
    for h in range(ML_HEADS):
        wq_ref[h] = wmq_ref[h].astype(bf16)
        wkt_ref[h] = wmk_ref[h].T.astype(bf16)


def _weight_prep(w_uq, w_ukv, w_mq, w_mk):
    return pl.pallas_call(
        _wprep_kernel,
        out_shape=(
            jax.ShapeDtypeStruct((MLA_HEADS, QK_PAD, Q_LORA), bf16),
            jax.ShapeDtypeStruct((KV_LORA, MLA_HEADS * MLA_NOPE), bf16),
            jax.ShapeDtypeStruct((MLA_HEADS, MLA_V, KV_LORA), bf16),
            jax.ShapeDtypeStruct((ML_HEADS, ML_V, ML_QK), bf16),
            jax.ShapeDtypeStruct((ML_HEADS, ML_QK, ML_V), bf16),
        ),
        compiler_params=pltpu.CompilerParams(vmem_limit_bytes=VMEM_LIMIT),
        name="weight_prep",
    )(w_uq, w_ukv, w_mq, w_mk)


def _pack_kernel(w_ref, krp_ref, g_ref, o_ref):
    o_ref[:, 0:OFF_U] = w_ref[:, 0:OFF_U].astype(bf16)
    o_ref[:, OFF_U:P_U] = krp_ref[...].astype(bf16)
    o_ref[:, P_U:P_G] = w_ref[:, OFF_U:OFF_G].astype(bf16)
    o_ref[:, P_G:P_COLS] = g_ref[...].astype(bf16)


def _pack_w_in(w_in):
    d = w_in.shape[0]
    z = lambda n: jnp.zeros((d, n), w_in.dtype)
    gates = []
    for gi in range(4):
        gates += [w_in[:, OFF_G + gi * ML_HEADS:OFF_G + (gi + 1) * ML_HEADS], z(GATE_STRIDE - ML_HEADS)]
    gates = jnp.concatenate(gates + [z(LANES - 4 * GATE_STRIDE)], axis=1)
    krp = w_in[:, OFF_KR:OFF_U][:, _rope_perm()]
    tr = min(ROW_TILE, d)
    assert d % tr == 0
    return pl.pallas_call(
        _pack_kernel,
        out_shape=jax.ShapeDtypeStruct((d, P_COLS), bf16),
        grid=(d // tr,),
        in_specs=[
            pl.BlockSpec((tr, w_in.shape[1]), lambda i: (i, 0)),
            pl.BlockSpec((tr, MLA_ROPE), lambda i: (i, 0)),
            pl.BlockSpec((tr, LANES), lambda i: (i, 0)),
        ],
        out_specs=pl.BlockSpec((tr, P_COLS), lambda i: (i, 0)),
        compiler_params=_cparams(("arbitrary",)),
        name="pack_w_in",
    )(w_in, krp, gates)


def _pack_b_gate(b_gate):
    parts = []
    for gi in range(4):
        parts += [b_gate[gi * ML_HEADS:(gi + 1) * ML_HEADS], jnp.zeros((GATE_STRIDE - ML_HEADS,), f32)]
    parts.append(jnp.zeros((LANES - 4 * GATE_STRIDE,), f32))
    return jnp.concatenate(parts)[None, :]


def kernel(x, c, ctx, c_ctx, w_ada, b_ada, w_in, g_q, w_uq, g_kv, w_ukv, ml_conv_w, ml_conv_b, w_mq, w_mk, b_gate, g_hn, w_out, ln1_g, ln1_b, w_up, ffn_conv_w, ffn_conv_b, w_down, ln2_g, ln2_b):
    b, s, d = x.shape
    n_ctx = ctx.shape[1]
    assert n_ctx == CHUNK and s % ROW_TILE == 0
    n_ctx_tiles = n_ctx // ROW_TILE
    layer = 0

    rows = -(-(b + 1) // SUBLANES) * SUBLANES
    cc = jnp.concatenate([c, c_ctx[None, :], jnp.zeros((rows - b - 1, d), f32)], axis=0)
    modr = _modulation(cc, w_ada[layer], b_ada[layer][None, :]).reshape(rows, N_MOD, d)

    ck, sk, ct, st = (jnp.asarray(tbl) for tbl in _rope_tables(s, n_ctx))

    wm, wk, wvt, w_q, w_kt = _weight_prep(w_uq[layer], w_ukv[layer], w_mq[layer], w_mk[layer])
    qt, qsq, kk, vt, ksq, g, gt, u, v, o = _in_projection(
        ctx, x, modr, _pack_w_in(w_in[layer]), g_q[layer][None, :], g_kv[layer][None, :], ck, sk,
        _pack_b_gate(b_gate[layer]), wk, wvt, wm, ct, st, n_ctx_tiles)

    a = _attention(qt, qsq, ksq, kk, vt)

    mq, mkt = _mlstm_projection(u, ml_conv_w[layer], ml_conv_b[layer][None, :], w_q, w_kt, n_ctx_tiles)
    hf, hb = _mlstm_scan(mq, mkt, v, g, gt)

    x1 = _out_projection(a, hf, hb, o, x, modr, g_hn[layer].reshape(1, ML_INNER), w_out[layer].astype(bf16),
                         ln1_g[layer][None, :], ln1_b[layer][None, :], n_ctx_tiles)
    return _conv_ffn(x1, modr, w_up[layer].astype(bf16), ffn_conv_w[layer], ffn_conv_b[layer][None, :],
                     w_down[layer].astype(bf16), ln2_g[layer][None, :], ln2_b[layer][None, :])
```

```python
import functools

import numpy as np
import jax
import jax.numpy as jnp
from jax import lax
from jax.experimental import pallas as pl
from jax.experimental.pallas import tpu as pltpu

f32 = jnp.float32
bf16 = jnp.bfloat16

GRID_W = 64
MLA_HEADS = 8
MLA_NOPE = 128
MLA_ROPE = 64
MLA_V = 128
Q_LORA = 512
KV_LORA = 256
ROPE_THETA = 10000.0
ML_HEADS = 4
ML_QK = 128
ML_V = 256
ML_INNER = ML_HEADS * ML_V
N_GATES = 4 * ML_HEADS
DEPTH = 1
OFF_CKV = Q_LORA
OFF_KR = Q_LORA + KV_LORA
OFF_U = OFF_KR + MLA_ROPE
OFF_V = OFF_U + ML_INNER
OFF_O = OFF_V + ML_INNER
OFF_G = OFF_O + ML_INNER
ALPHA = (2.0 * DEPTH) ** 0.25
EPS = 1e-6
N_MOD = 6

LANES = 128
SUBLANES = 8
MXU_DIM = 256
VMEM_LIMIT = 56 * 1024 * 1024

QK_PAD = 2 * LANES
GATE_STRIDE = SUBLANES
P_Q = 0
P_CKV = P_Q + Q_LORA
P_KR = P_CKV + KV_LORA
P_U = P_KR + LANES
P_V = P_U + ML_INNER
P_O = P_V + ML_INNER
P_G = P_O + ML_INNER
P_COLS = P_G + LANES
STAB_ROWS = 16
STAB_SLACK = 1.02
L_MIN = 2.0 ** -100

ROW_TILE = 256
CHUNK = ROW_TILE
ATT_TQ = 1024
ATT_TK_MAX = 2816
OUT_TM = 512
FFN_TM = 512
FFN_TF = 512
HALO = SUBLANES


def _cparams(sem):
    return pltpu.CompilerParams(dimension_semantics=sem, vmem_limit_bytes=VMEM_LIMIT)


def _silu(x):
    return x * jax.nn.sigmoid(x)


def _dot(a, b):
    return jnp.dot(a, b, preferred_element_type=f32)


def _dot_nt(a, b):
    return lax.dot_general(a, b, (((1,), (1,)), ((), ())), preferred_element_type=f32)


def _layer_norm_rows(r, g, b):
    mu = jnp.mean(r, axis=-1, keepdims=True)
    d = r - mu
    var = jnp.mean(d * d, axis=-1, keepdims=True)
    return d * lax.rsqrt(var + EPS) * g + b


def _mod_kernel(c_ref, w_ref, b_ref, o_ref):
    s = _silu(c_ref[...])
    o_ref[...] = _dot(s.astype(bf16), w_ref[...].astype(bf16)) + b_ref[...]


def _modulation(cc, w_ada, b_ada):
    rows, d = cc.shape
    n = w_ada.shape[1]
    tn = min(n, 1536)
    assert n % tn == 0
    return pl.pallas_call(
        _mod_kernel,
        out_shape=jax.ShapeDtypeStruct((rows, n), f32),
        grid=(n // tn,),
        in_specs=[
            pl.BlockSpec((rows, d), lambda j: (0, 0)),
            pl.BlockSpec((d, tn), lambda j: (0, j)),
            pl.BlockSpec((1, tn), lambda j: (0, j)),
        ],
        out_specs=pl.BlockSpec((rows, tn), lambda j: (0, j)),
        compiler_params=_cparams(("arbitrary",)),
        name="adaln_mod",
    )(cc, w_ada, b_ada)


def _inproj_kernel(ctx_ref, x_ref, mod_ref, w_ref, gq_ref, gkv_ref, ck_ref, sk_ref, bg_ref, wk_ref, wvt_ref,
                   cq_ref, k_ref, vt_ref, ksq_ref, g_ref, gt_ref, u_ref, v_ref, o_ref, h_ref, *, n_ctx_tiles):
    shift = mod_ref[0:1, :]
    scale = mod_ref[1:2, :]
    is_ctx = pl.program_id(1) < n_ctx_tiles
    rows = jnp.where(is_ctx, ctx_ref[...], x_ref[...])
    h_ref[...] = (rows * (1.0 + scale) + shift).astype(bf16)

    def sect(a, n):
        return _dot(h_ref[...], w_ref[:, a:a + n])

    zq = sect(P_Q, Q_LORA)
    cq = zq * lax.rsqrt(jnp.mean(zq * zq, axis=-1, keepdims=True) + EPS) * gq_ref[...]
    cq_ref[...] = cq.astype(bf16)
    zkv = sect(P_CKV, KV_LORA)
    ckv = zkv * lax.rsqrt(jnp.mean(zkv * zkv, axis=-1, keepdims=True) + EPS) * gkv_ref[...]
    zkr = sect(P_KR, LANES)
    kr = zkr * ck_ref[...] + pltpu.roll(zkr, LANES - MLA_ROPE, 1) * sk_ref[...]
    _kv_outputs(ckv.astype(bf16), kr.astype(bf16), wk_ref, wvt_ref, k_ref, vt_ref, ksq_ref)
    g = sect(P_G, LANES) + bg_ref[...]
    tm = g.shape[0]
    lf = jax.nn.log_sigmoid(g)
    tri = jnp.where(lax.broadcasted_iota(jnp.int32, (tm, tm), 0)
                    >= lax.broadcasted_iota(jnp.int32, (tm, tm), 1), 1.0, 0.0).astype(bf16)
    hi = lf.astype(bf16)
    r1 = lf - hi.astype(f32)
    mid = r1.astype(bf16)
    lo = (r1 - mid.astype(f32)).astype(bf16)
    pre = _dot(tri, hi) + _dot(tri, mid) + _dot(tri, lo)
    suf = pre[tm - 1:tm, :] - pre + lf
    grp = lax.broadcasted_iota(jnp.int32, (1, LANES), 1) // GATE_STRIDE
    g2 = jnp.where(grp == 1, pre, jnp.where(grp == 3, suf, g))
    g_ref[...] = g2
    gt_ref[...] = g2.T[0:4 * GATE_STRIDE, :]
    u_ref[...] = sect(P_U, ML_INNER).astype(bf16)
    v_ref[...] = sect(P_V, ML_INNER).astype(bf16)
    o_ref[...] = sect(P_O, ML_INNER).astype(bf16)


def _in_projection(ctx, x, modr, w_in_p, g_q, g_kv, ck, sk, bg, wk, wvt, n_ctx_tiles):
    b, s, d = x.shape
    tm = ROW_TILE
    t = s + n_ctx_tiles * tm
    nt = t // tm
    n_batch = b

    def mod_idx(bi, i):
        return (jnp.where(i < n_ctx_tiles, n_batch, bi), 0, 0)

    row = lambda w: pl.BlockSpec((None, tm, w), lambda bi, i: (bi, i, 0))
    const = lambda r, c: pl.BlockSpec((r, c), lambda bi, i: (0, 0))
    out_shapes = (
        jax.ShapeDtypeStruct((b, t, Q_LORA), bf16),
        jax.ShapeDtypeStruct((b, MLA_HEADS, t, QK_PAD), bf16),
        jax.ShapeDtypeStruct((b, MLA_HEADS, MLA_V, t), bf16),
        jax.ShapeDtypeStruct((b, MLA_HEADS, t), f32),
        jax.ShapeDtypeStruct((b, t, LANES), f32),
        jax.ShapeDtypeStruct((b, nt, 4 * GATE_STRIDE, tm), f32),
        jax.ShapeDtypeStruct((b, t, ML_INNER), bf16),
        jax.ShapeDtypeStruct((b, t, ML_INNER), bf16),
        jax.ShapeDtypeStruct((b, s, ML_INNER), bf16),
    )
    out_specs = (
        row(Q_LORA),
        pl.BlockSpec((None, MLA_HEADS, tm, QK_PAD), lambda bi, i: (bi, 0, i, 0)),
        pl.BlockSpec((None, MLA_HEADS, MLA_V, tm), lambda bi, i: (bi, 0, 0, i)),
        pl.BlockSpec((None, MLA_HEADS, tm), lambda bi, i: (bi, 0, i)),
        row(LANES),
        pl.BlockSpec((None, None, 4 * GATE_STRIDE, tm), lambda bi, i: (bi, i, 0, 0)),
        row(ML_INNER), row(ML_INNER),
        pl.BlockSpec((None, tm, ML_INNER), lambda bi, i: (bi, jnp.maximum(i - n_ctx_tiles, 0), 0)),
    )
    return pl.pallas_call(
        functools.partial(_inproj_kernel, n_ctx_tiles=n_ctx_tiles),
        out_shape=out_shapes,
        grid=(b, nt),
        in_specs=[
            pl.BlockSpec((None, tm, d), lambda bi, i: (bi, jnp.minimum(i, n_ctx_tiles - 1), 0)),
            pl.BlockSpec((None, tm, d), lambda bi, i: (bi, jnp.maximum(i - n_ctx_tiles, 0), 0)),
            pl.BlockSpec((None, N_MOD, d), mod_idx),
            pl.BlockSpec((d, P_COLS), lambda bi, i: (0, 0), pipeline_mode=pl.Buffered(1)),
            const(1, Q_LORA),
            const(1, KV_LORA),
            pl.BlockSpec((tm, LANES), lambda bi, i: (i, 0)),
            pl.BlockSpec((tm, LANES), lambda bi, i: (i, 0)),
            const(1, LANES),
            pl.BlockSpec((KV_LORA, MLA_HEADS * MLA_NOPE), lambda bi, i: (0, 0), pipeline_mode=pl.Buffered(1)),
            pl.BlockSpec((MLA_HEADS, MLA_V, KV_LORA), lambda bi, i: (0, 0, 0), pipeline_mode=pl.Buffered(1)),
        ],
        out_specs=out_specs,
        scratch_shapes=[pltpu.VMEM((tm, d), bf16)],
        compiler_params=_cparams(("arbitrary", "arbitrary")),
        name="in_projection",
    )(ctx, x, modr, w_in_p, g_q, g_kv, ck, sk, bg, wk, wvt)


def _kv_outputs(ckv, kr, wk_ref, wvt_ref, k_ref, vt_ref, ksq_ref):
    kn = _dot(ckv, wk_ref[...]).astype(bf16)
    knf = kn.astype(f32)
    krf = kr.astype(f32)
    hsel = lax.broadcasted_iota(jnp.int32, (MLA_HEADS, MLA_HEADS * MLA_NOPE), 0)
    csel = lax.broadcasted_iota(jnp.int32, (MLA_HEADS, MLA_HEADS * MLA_NOPE), 1) // MLA_NOPE
    sel = jnp.where(hsel == csel, 1.0, 0.0).astype(bf16)
    ksq_ref[...] = (_dot_nt(sel, (knf * knf).astype(bf16))
                    + _dot_nt(jnp.ones((MLA_HEADS, LANES), bf16), (krf * krf).astype(bf16)))
    lane = lax.broadcasted_iota(jnp.int32, kr.shape, 1)
    stab = jnp.logical_and(lane >= MLA_ROPE, lane < MLA_ROPE + STAB_ROWS)
    kr_aug = jnp.where(stab, 1.0, krf).astype(bf16)
    for h in range(MLA_HEADS):
        k_ref[h, :, 0:MLA_NOPE] = kn[:, h * MLA_NOPE:(h + 1) * MLA_NOPE]
        k_ref[h, :, MLA_NOPE:QK_PAD] = kr_aug
        vt_ref[h] = _dot_nt(wvt_ref[h], ckv).astype(bf16)


def _q_kernel(cq_ref, ksq_ref, wm_ref, ct_ref, st_ref, qt_ref, *, scale):
    cq = cq_ref[...]
    ct = ct_ref[...]
    st = st_ref[...]
    tm = cq.shape[0]
    kmax2 = jnp.max(ksq_ref[...], axis=1, keepdims=True)
    r0, r1 = MLA_NOPE, MLA_NOPE + MLA_ROPE
    for h in range(MLA_HEADS):
        qm = _dot_nt(wm_ref[h], cq)
        qn = (qm[0:r0] * scale).astype(bf16)
        qr = ((qm[r0:r1] * ct + qm[r1:QK_PAD] * st) * scale).astype(bf16)
        qnf = qn.astype(f32)
        qrf = qr.astype(f32)
        qsq = jnp.sum(qnf * qnf, axis=0, keepdims=True) + jnp.sum(qrf * qrf, axis=0, keepdims=True)
        bound = jnp.sqrt(qsq * kmax2[h:h + 1, :]) * STAB_SLACK
        qt_ref[h, 0:r0, :] = qn
        qt_ref[h, r0:r1, :] = qr
        qt_ref[h, r1:r1 + STAB_ROWS, :] = jnp.broadcast_to(bound * (-1.0 / STAB_ROWS), (STAB_ROWS, tm)).astype(bf16)
        qt_ref[h, r1 + STAB_ROWS:QK_PAD, :] = jnp.zeros((QK_PAD - r1 - STAB_ROWS, tm), bf16)


def _q_prep(cq, ksq, wm, ct, st, n_ctx_tiles):
    b, t, _ = cq.shape
    tm = ROW_TILE
    s = t - n_ctx_tiles * tm
    scale = float((MLA_NOPE + MLA_ROPE) ** -0.5 * 1.4426950408889634)
    return pl.pallas_call(
        functools.partial(_q_kernel, scale=scale),
        out_shape=jax.ShapeDtypeStruct((b, MLA_HEADS, QK_PAD, s), bf16),
        grid=(b, s // tm),
        in_specs=[
            pl.BlockSpec((None, tm, Q_LORA), lambda bi, i: (bi, i + n_ctx_tiles, 0)),
            pl.BlockSpec((None, MLA_HEADS, t), lambda bi, i: (bi, 0, 0)),
            pl.BlockSpec((MLA_HEADS, QK_PAD, Q_LORA), lambda bi, i: (0, 0, 0)),
            pl.BlockSpec((MLA_ROPE, tm), lambda bi, i: (0, i)),
            pl.BlockSpec((MLA_ROPE, tm), lambda bi, i: (0, i)),
        ],
        out_specs=pl.BlockSpec((None, MLA_HEADS, QK_PAD, tm), lambda bi, i: (bi, 0, 0, i)),
        compiler_params=_cparams(("arbitrary", "arbitrary")),
        name="q_prep",
    )(cq, ksq, wm, ct, st)


def _attn_bound_kernel(qt_ref, k_ref, vt_ref, o_ref, l_ref, *, tq, tk, nk):
    def q_tile(i, carry):
        q0 = pl.multiple_of(i * tq, tq)
        qt = qt_ref[:, pl.ds(q0, tq)]
        l = jnp.zeros((1, tq), f32)
        acc = jnp.zeros((MLA_V, tq), f32)
        s = _dot(k_ref[0:tk, :], qt)
        for j in range(nk):
            s_next = _dot(k_ref[(j + 1) * tk:(j + 2) * tk, :], qt) if j + 1 < nk else None
            p = jnp.exp2(s)
            l = l + jnp.sum(p, axis=0, keepdims=True)
            acc = acc + _dot(vt_ref[:, j * tk:(j + 1) * tk], p.astype(bf16))
            s = s_next
        o_ref[pl.ds(q0, tq), :] = (acc / l).T.astype(o_ref.dtype)
        l_ref[:, pl.ds(q0, tq)] = l
        return carry

    lax.fori_loop(0, qt_ref.shape[1] // tq, q_tile, 0)


def _attn_online_kernel(qt_ref, k_ref, vt_ref, o_ref, *, tk, nk):
    qt = qt_ref[...]
    tq = qt.shape[1]
    m = jnp.full((1, tq), -jnp.inf, f32)
    l = jnp.zeros((1, tq), f32)
    acc = jnp.zeros((MLA_V, tq), f32)
    for j in range(nk):
        s = _dot(k_ref[j * tk:(j + 1) * tk, :], qt)
        m_new = jnp.maximum(m, jnp.max(s, axis=0, keepdims=True))
        alpha = jnp.exp2(m - m_new)
        p = jnp.exp2(s - m_new)
        l = alpha * l + jnp.sum(p, axis=0, keepdims=True)
        acc = alpha * acc + _dot(vt_ref[:, j * tk:(j + 1) * tk], p.astype(bf16))
        m = m_new
    o_ref[...] = (acc / l).T.astype(o_ref.dtype)


def _attention(qt, k, vt):
    b, hh, _, s = qt.shape
    t = k.shape[2]
    tq = min(ATT_TQ, s)
    tk = MXU_DIM
    for cand in range(MXU_DIM, ATT_TK_MAX + 1, MXU_DIM):
        if t % cand == 0:
            tk = cand
    assert s % tq == 0 and t % tk == 0
    in_specs = [
        pl.BlockSpec((None, None, QK_PAD, tq), lambda bi, h, i: (bi, h, 0, i)),
        pl.BlockSpec((None, None, t, QK_PAD), lambda bi, h, i: (bi, h, 0, 0)),
        pl.BlockSpec((None, None, MLA_V, t), lambda bi, h, i: (bi, h, 0, 0)),
    ]
    a_shape = jax.ShapeDtypeStruct((b, s, hh * MLA_V), bf16)
    a_spec = pl.BlockSpec((None, tq, MLA_V), lambda bi, h, i: (bi, i, h))
    sem = ("arbitrary", "arbitrary", "arbitrary")
    a, l = pl.pallas_call(
        functools.partial(_attn_bound_kernel, tq=tq, tk=tk, nk=t // tk),
        out_shape=(a_shape, jax.ShapeDtypeStruct((b, hh, 1, s), f32)),
        grid=(b, hh),
        in_specs=[
            pl.BlockSpec((None, None, QK_PAD, s), lambda bi, h: (bi, h, 0, 0)),
            pl.BlockSpec((None, None, t, QK_PAD), lambda bi, h: (bi, h, 0, 0)),
            pl.BlockSpec((None, None, MLA_V, t), lambda bi, h: (bi, h, 0, 0)),
        ],
        out_specs=(pl.BlockSpec((None, s, MLA_V), lambda bi, h: (bi, 0, h)),
                   pl.BlockSpec((None, None, 1, s), lambda bi, h: (bi, h, 0, 0))),
        compiler_params=_cparams(("arbitrary", "arbitrary")),
        name="attention_bound",
    )(qt, k, vt)

    def online(_):
        return pl.pallas_call(
            functools.partial(_attn_online_kernel, tk=tk, nk=t // tk),
            out_shape=a_shape,
            grid=(b, hh, s // tq),
            in_specs=in_specs,
            out_specs=a_spec,
            compiler_params=_cparams(sem),
            name="attention_online",
        )(qt, k, vt)

    ok = jnp.min(l) >= L_MIN
    return lax.cond(ok, lambda _: a, online, None)


def _mproj_kernel(u_ref, up_ref, un_ref, cw_ref, cb_ref, wq_ref, wkt_ref, q_ref, kt_ref, *,
                  n_ctx_tiles, n_tiles):
    i = pl.program_id(1)
    u = u_ref[...].astype(f32)
    tm = u.shape[0]
    first = jnp.logical_or(i == 0, i == n_ctx_tiles)
    last = jnp.logical_or(i == n_ctx_tiles - 1, i == n_tiles - 1)
    prev_row = jnp.where(first, 0.0, up_ref[HALO - 1:HALO, :].astype(f32))
    next_row = jnp.where(last, 0.0, un_ref[0:1, :].astype(f32))
    ridx = lax.broadcasted_iota(jnp.int32, (tm, 1), 0)
    u_prev = jnp.where(ridx == 0, prev_row, pltpu.roll(u, 1, 0))
    u_next = jnp.where(ridx == tm - 1, next_row, pltpu.roll(u, tm - 1, 0))
    y = cw_ref[0:1, :] * u_prev + cw_ref[1:2, :] * u + cw_ref[2:3, :] * u_next + cb_ref[...]
    act = _silu(y).astype(bf16)
    for h in range(ML_HEADS):
        a_h = act[:, h * ML_V:(h + 1) * ML_V]
        q_ref[:, h * ML_QK:(h + 1) * ML_QK] = (_dot(a_h, wq_ref[h]) * float(ML_QK ** -0.5)).astype(bf16)
        kt_ref[h * ML_QK:(h + 1) * ML_QK, :] = _dot_nt(wkt_ref[h], a_h).astype(bf16)


def _mlstm_projection(u, conv_w, conv_b, w_q, w_kt, n_ctx_tiles):
    b, t, _ = u.shape
    tm = ROW_TILE
    nt = t // tm
    hb = tm // HALO
    return pl.pallas_call(
        functools.partial(_mproj_kernel, n_ctx_tiles=n_ctx_tiles, n_tiles=nt),
        out_shape=(jax.ShapeDtypeStruct((b, t, ML_HEADS * ML_QK), bf16),
                   jax.ShapeDtypeStruct((b, nt, ML_HEADS * ML_QK, tm), bf16)),
        grid=(b, nt),
        in_specs=[
            pl.BlockSpec((None, tm, ML_INNER), lambda bi, i: (bi, i, 0)),
            pl.BlockSpec((None, HALO, ML_INNER), lambda bi, i: (bi, jnp.maximum(i * hb - 1, 0), 0)),
            pl.BlockSpec((None, HALO, ML_INNER),
                         lambda bi, i: (bi, jnp.minimum((i + 1) * hb, nt * hb - 1), 0)),
            pl.BlockSpec((3, ML_INNER), lambda bi, i: (0, 0)),
            pl.BlockSpec((1, ML_INNER), lambda bi, i: (0, 0)),
            pl.BlockSpec((ML_HEADS, ML_V, ML_QK), lambda bi, i: (0, 0, 0)),
            pl.BlockSpec((ML_HEADS, ML_QK, ML_V), lambda bi, i: (0, 0, 0)),
        ],
        out_specs=(pl.BlockSpec((None, tm, ML_HEADS * ML_QK), lambda bi, i: (bi, i, 0)),
                   pl.BlockSpec((None, None, ML_HEADS * ML_QK, tm), lambda bi, i: (bi, i, 0, 0))),
        compiler_params=_cparams(("arbitrary", "arbitrary")),
        name="mlstm_projection",
    )(u, u, u, conv_w, conv_b, w_q, w_kt)


def _scan_kernel(qf_ref, ktf_ref, vf_ref, gf_ref, gtf_ref, qb_ref, ktb_ref, vb_ref, gb_ref, gtb_ref,
                 hf_ref, hb_ref, c_scr, m_scr):
    j = pl.program_id(1)
    L = qf_ref.shape[0]
    GS = GATE_STRIDE

    rr = lax.broadcasted_iota(jnp.int32, (L, L), 0)
    cc = lax.broadcasted_iota(jnp.int32, (L, L), 1)
    visible = (rr >= cc, rr <= cc)
    dirs = ((qf_ref, ktf_ref, vf_ref, gf_ref, gtf_ref, hf_ref), (qb_ref, ktb_ref, vb_ref, gb_ref, gtb_ref, hb_ref))
    ones_col = jnp.where(lax.broadcasted_iota(jnp.int32, (L, LANES), 1) == 0, 1.0, 0.0).astype(bf16)

    chains = []
    for d, (q_ref, kt_ref, v_ref, g_ref, gt_ref, h_ref) in enumerate(dirs):
        g = g_ref[...]
        gt = gt_ref[...]
        for h in range(ML_HEADS):
            ci = 2 * d * GS + h
            cf = ci + GS
            bc_r = gt[cf:cf + 1, :]
            chains.append(dict(
                d=d, h=h, st=d * ML_HEADS + h, h_ref=h_ref,
                q=q_ref[:, h * ML_QK:(h + 1) * ML_QK],
                kt=kt_ref[h * ML_QK:(h + 1) * ML_QK, :],
                v=jnp.concatenate([v_ref[:, h * ML_V:(h + 1) * ML_V], ones_col], axis=1),
                bc_c=g[:, cf:cf + 1],
                ig_r=gt[ci:ci + 1, :], bc_r=bc_r,
                a=bc_r[:, L - 1:L] if d == 0 else bc_r[:, 0:1],
            ))

    def step(first):
        for c in chains:
            c["s"] = _dot(c["q"], c["kt"])
        for c in chains:
            if not first:
                c["c_st"] = c_scr[c["st"]]
                c["m_st"] = m_scr[c["st"]]
                c["qc"] = _dot(c["q"], c["c_st"].astype(bf16))
            else:
                c["m_st"] = 0.0

        for c in chains:
            drow = jnp.where(visible[c["d"]], c["ig_r"] - c["bc_r"], -jnp.inf)
            mm = jnp.maximum(c["m_st"], jnp.max(drow, axis=1, keepdims=True))
            c["w_inter"] = jnp.exp(c["m_st"] - mm)
            c["floor"] = jnp.exp(-(c["bc_c"] + mm))
            c["p"] = jnp.exp(drow - mm) * c["s"]

        for c in chains:
            nd = _dot(c["p"].astype(bf16), c["v"])
            if not first:
                nd = nd + c["w_inter"] * c["qc"]
            den = nd[:, ML_V:ML_V + 1]
            h = c["h"]
            hv = nd[:, 0:ML_V] / jnp.maximum(jnp.abs(den), c["floor"])
            c["h_ref"][:, h * ML_V:(h + 1) * ML_V] = hv.astype(c["h_ref"].dtype)

        for c in chains:
            a, m_st = c["a"], c["m_st"]
            g_log_r = a - c["bc_r"] + c["ig_r"]
            m_new = jnp.maximum(a + m_st, jnp.max(g_log_r, axis=1, keepdims=True))
            kwt = c["kt"].astype(f32) * jnp.exp(g_log_r - m_new)
            c_new = _dot(kwt.astype(bf16), c["v"])
            if not first:
                c_new = c_new + jnp.exp(a + m_st - m_new) * c["c_st"]
            c_scr[c["st"]] = c_new
            m_scr[c["st"]] = m_new

    @pl.when(j == 0)
    def _():
        step(True)

    @pl.when(j != 0)
    def _():
        step(False)


def _mlstm_scan(q, kt, v, g, gt):
    b, t, _ = q.shape
    L = CHUNK
    nc = t // L

    fwd = lambda j: j
    bwd = lambda j: jnp.where(j == 0, 0, nc - j)

    def specs(cidx):
        return [
            pl.BlockSpec((None, L, ML_HEADS * ML_QK), lambda bi, j: (bi, cidx(j), 0)),
            pl.BlockSpec((None, None, ML_HEADS * ML_QK, L), lambda bi, j: (bi, cidx(j), 0, 0)),
            pl.BlockSpec((None, L, ML_INNER), lambda bi, j: (bi, cidx(j), 0)),
            pl.BlockSpec((None, L, LANES), lambda bi, j: (bi, cidx(j), 0)),
            pl.BlockSpec((None, None, 4 * GATE_STRIDE, L), lambda bi, j: (bi, cidx(j), 0, 0)),
        ]

    n_chain = 2 * ML_HEADS
    h_shape = jax.ShapeDtypeStruct((b, t - L, ML_INNER), bf16)
    return pl.pallas_call(
        _scan_kernel,
        out_shape=(h_shape, h_shape),
        grid=(b, nc),
        in_specs=specs(fwd) + specs(bwd),
        out_specs=(pl.BlockSpec((None, L, ML_INNER), lambda bi, j: (bi, jnp.maximum(j - 1, 0), 0)),
                   pl.BlockSpec((None, L, ML_INNER), lambda bi, j: (bi, nc - 1 - jnp.maximum(j, 1), 0))),
        scratch_shapes=[
            pltpu.VMEM((n_chain, ML_QK, ML_V + LANES), f32),
            pltpu.VMEM((n_chain, 1, 1), f32),
        ],
        compiler_params=_cparams(("arbitrary", "arbitrary")),
        name="mlstm_scan",
    )(q, kt, v, g, gt, q, kt, v, g, gt)


def _outproj_kernel(a_ref, hf_ref, hb_ref, o_ref, x_ref, mod_ref, ghn_ref, w_ref, lg_ref, lb_ref,
                    out_ref):
    ghn = ghn_ref[...]
    gate = mod_ref[2:3, :]
    tm = x_ref.shape[0]
    halves = [slice(r, r + tm // 2) for r in (0, tm // 2)]
    lhs = []
    for rows in halves:
        hm = hf_ref[rows, :].astype(f32) + hb_ref[rows, :].astype(f32)
        og = jax.nn.sigmoid(o_ref[rows, :].astype(f32))
        parts = [a_ref[rows, :]]
        for h in range(ML_HEADS):
            sl = slice(h * ML_V, (h + 1) * ML_V)
            seg = hm[:, sl]
            mu = jnp.mean(seg, axis=-1, keepdims=True)
            dd = seg - mu
            var = jnp.mean(dd * dd, axis=-1, keepdims=True)
            parts.append((og[:, sl] * (dd * lax.rsqrt(var + EPS) * ghn[:, sl])).astype(bf16))
        lhs.append(jnp.concatenate(parts, axis=1))
    ys = [_dot(l, w_ref[...]) for l in lhs]
    for rows, y in zip(halves, ys):
        r = ALPHA * x_ref[rows, :] + gate * y
        out_ref[rows, :] = _layer_norm_rows(r, lg_ref[...], lb_ref[...])


def _out_projection(a, hf, hb, o, x, modr, g_hn, w_out, ln_g, ln_b, n_ctx_tiles):
    b, s, d = x.shape
    tm = min(OUT_TM, s)
    assert s % tm == 0
    mix = w_out.shape[0]
    return pl.pallas_call(
        _outproj_kernel,
        out_shape=jax.ShapeDtypeStruct((b, s, d), f32),
        grid=(b, s // tm),
        in_specs=[
            pl.BlockSpec((None, tm, MLA_HEADS * MLA_V), lambda bi, i: (bi, i, 0)),
            pl.BlockSpec((None, tm, ML_INNER), lambda bi, i: (bi, i, 0)),
            pl.BlockSpec((None, tm, ML_INNER), lambda bi, i: (bi, i, 0)),
            pl.BlockSpec((None, tm, ML_INNER), lambda bi, i: (bi, i, 0)),
            pl.BlockSpec((None, tm, d), lambda bi, i: (bi, i, 0)),
            pl.BlockSpec((None, N_MOD, d), lambda bi, i: (bi, 0, 0)),
            pl.BlockSpec((1, ML_INNER), lambda bi, i: (0, 0)),
            pl.BlockSpec((mix, d), lambda bi, i: (0, 0)),
            pl.BlockSpec((1, d), lambda bi, i: (0, 0)),
            pl.BlockSpec((1, d), lambda bi, i: (0, 0)),
        ],
        out_specs=pl.BlockSpec((None, tm, d), lambda bi, i: (bi, i, 0)),
        compiler_params=_cparams(("arbitrary", "arbitrary")),
        name="out_projection",
    )(a, hf, hb, o, x, modr, g_hn, w_out, ln_g, ln_b)


def _ffn_kernel(x_ref, xp_ref, xn_ref, mod_ref, wup_hbm, cw_ref, cb_ref, wd_hbm, lg_ref, lb_ref,
                out_ref, hs_ref, gs_ref, wg_buf, wu_buf, wd_buf, sem, *, n_row_tiles, nf, tf, nbuf):
    i = pl.program_id(1)
    step = pl.program_id(0) * n_row_tiles + i
    n_steps = pl.num_programs(0) * n_row_tiles
    tm = x_ref.shape[0]
    dff = nf * tf

    def weight_copies(j):
        slot = j % nbuf
        return (
            pltpu.make_async_copy(wup_hbm.at[:, pl.ds(j * tf, tf)], wg_buf.at[slot], sem.at[0, slot]),
            pltpu.make_async_copy(wup_hbm.at[:, pl.ds(dff + j * tf, tf)], wu_buf.at[slot], sem.at[1, slot]),
            pltpu.make_async_copy(wd_hbm.at[pl.ds(j * tf, tf), :], wd_buf.at[slot], sem.at[2, slot]),
        )

    @pl.when(step == 0)
    def _():
        for c in weight_copies(0) + weight_copies(1):
            c.start()

    shift = mod_ref[3:4, :]
    scale = mod_ref[4:5, :]
    hs_ref[0:HALO, :] = (xp_ref[...] * (1.0 + scale) + shift).astype(bf16)
    hs_ref[HALO:HALO + tm, :] = (x_ref[...] * (1.0 + scale) + shift).astype(bf16)
    hs_ref[HALO + tm:2 * HALO + tm, :] = (xn_ref[...] * (1.0 + scale) + shift).astype(bf16)
    ridx = lax.broadcasted_iota(jnp.int32, (tm, 1), 0)
    first_row = jnp.logical_and(ridx == 0, i == 0)
    last_row = jnp.logical_and(ridx == tm - 1, i == n_row_tiles - 1)

    for j in range(nf):
        for c in weight_copies((j + 2) % nf):
            c.start()
        for c in weight_copies(j):
            c.wait()
        slot = j % nbuf
        gs = gs_ref.at[j % 2]
        cols = slice(j * tf, (j + 1) * tf)
        gs[...] = _dot(hs_ref[...], wg_buf[slot])
        g_prev = jnp.where(first_row, 0.0, gs[HALO - 1:HALO - 1 + tm, :])
        g_next = jnp.where(last_row, 0.0, gs[HALO + 1:HALO + 1 + tm, :])
        gate = (cw_ref[0:1, cols] * g_prev + cw_ref[1:2, cols] * gs[HALO:HALO + tm, :]
                + cw_ref[2:3, cols] * g_next + cb_ref[:, cols])
        up = _dot(hs_ref[HALO:HALO + tm, :], wu_buf[slot])
        act = (_silu(gate) * up).astype(bf16)
        contrib = _dot(act, wd_buf[slot])
        if j == 0:
            out_ref[...] = contrib
        else:
            out_ref[...] += contrib

    r = ALPHA * x_ref[...] + mod_ref[5:6, :] * out_ref[...]
    out_ref[...] = _layer_norm_rows(r, lg_ref[...], lb_ref[...])

    @pl.when(step == n_steps - 1)
    def _():
        for c in weight_copies(0) + weight_copies(1):
            c.wait()


def _conv_ffn(x, modr, w_up, conv_w, conv_b, w_down, ln_g, ln_b):
    b, s, d = x.shape
    dff = w_down.shape[0]
    tm = min(FFN_TM, s)
    tf = min(FFN_TF, dff)
    assert s % tm == 0 and dff % tf == 0
    nr = s // tm
    nf = dff // tf
    assert nf >= 4
    nbuf = next(n for n in range(3, nf + 1)
                if 0 not in ((nf - 2) % n, (nf - 1) % n) and 1 % n not in ((nf - 1) % n, 0))
    hb = tm // HALO
    return pl.pallas_call(
        functools.partial(_ffn_kernel, n_row_tiles=nr, nf=nf, tf=tf, nbuf=nbuf),
        out_shape=jax.ShapeDtypeStruct((b, s, d), f32),
        grid=(b, nr),
        in_specs=[
            pl.BlockSpec((None, tm, d), lambda bi, i: (bi, i, 0)),
            pl.BlockSpec((None, HALO, d), lambda bi, i: (bi, jnp.maximum(i * hb - 1, 0), 0)),
            pl.BlockSpec((None, HALO, d), lambda bi, i: (bi, jnp.minimum((i + 1) * hb, nr * hb - 1), 0)),
            pl.BlockSpec((None, N_MOD, d), lambda bi, i: (bi, 0, 0)),
            pl.BlockSpec(memory_space=pl.ANY),
            pl.BlockSpec((3, dff), lambda bi, i: (0, 0)),
            pl.BlockSpec((1, dff), lambda bi, i: (0, 0)),
            pl.BlockSpec(memory_space=pl.ANY),
            pl.BlockSpec((1, d), lambda bi, i: (0, 0)),
            pl.BlockSpec((1, d), lambda bi, i: (0, 0)),
        ],
        out_specs=pl.BlockSpec((None, tm, d), lambda bi, i: (bi, i, 0)),
        scratch_shapes=[
            pltpu.VMEM((tm + 2 * HALO, d), bf16),
            pltpu.VMEM((2, tm + 2 * HALO, tf), f32),
            pltpu.VMEM((nbuf, d, tf), bf16),
            pltpu.VMEM((nbuf, d, tf), bf16),
            pltpu.VMEM((nbuf, tf, d), bf16),
            pltpu.SemaphoreType.DMA((3, nbuf)),
        ],
        compiler_params=_cparams(("arbitrary", "arbitrary")),
        name="conv_ffn",
    )(x, x, x, modr, w_up, conv_w, conv_b, w_down, ln_g, ln_b)


def _rope_perm():
    q = MLA_ROPE // 4
    return jnp.concatenate([jnp.arange(q, 2 * q), jnp.arange(0, q),
                            jnp.arange(3 * q, 4 * q), jnp.arange(2 * q, 3 * q)])


def _rope_tables(s, n_ctx):
    n_rows = s // GRID_W
    row = np.repeat(np.arange(n_rows), GRID_W).astype(np.float32)
    col = np.tile(np.arange(GRID_W), n_rows).astype(np.float32)
    n_freq = MLA_ROPE // 4
    inv = (np.float32(ROPE_THETA) ** (-np.arange(n_freq, dtype=np.float32) / np.float32(n_freq))).astype(np.float32)
    ar = row[:, None] * inv
    ac = col[:, None] * inv
    cos = np.concatenate([np.cos(ar), np.cos(ar), np.cos(ac), np.cos(ac)], axis=-1).astype(np.float32)
    sin = np.concatenate([-np.sin(ar), np.sin(ar), -np.sin(ac), np.sin(ac)], axis=-1).astype(np.float32)
    pad = np.zeros((s, LANES - MLA_ROPE), np.float32)
    ctx_c = np.concatenate([np.ones((n_ctx, MLA_ROPE), np.float32),
                            np.zeros((n_ctx, LANES - MLA_ROPE), np.float32)], axis=1)
    ck = np.concatenate([ctx_c, np.concatenate([cos, pad], axis=1)], axis=0)
    sk = np.concatenate([np.zeros((n_ctx, LANES), np.float32), np.concatenate([sin, pad], axis=1)], axis=0)
    return ck, sk, np.ascontiguousarray(cos.T), np.ascontiguousarray(sin.T)


def _wprep_kernel(wuq_ref, wukv_ref, wmq_ref, wmk_ref, wm_ref, wk_ref, wvt_ref, wq_ref, wkt_ref):
    dq = MLA_NOPE + MLA_ROPE
    g = MLA_ROPE // 4
    tq = wuq_ref[...].T
    for h in range(MLA_HEADS):
        wm_ref[h, 0:dq, :] = tq[h * dq:(h + 1) * dq].astype(bf16)
        rope = tq[h * dq + MLA_NOPE:(h + 1) * dq]
        for dst, src in enumerate((1, 0, 3, 2)):
            wm_ref[h, dq + dst * g:dq + (dst + 1) * g, :] = rope[src * g:(src + 1) * g].astype(bf16)
    tkv = wukv_ref[...].T
    dkv = MLA_NOPE + MLA_V
    for h in range(MLA_HEADS):
        wk_ref[:, h * MLA_NOPE:(h + 1) * MLA_NOPE] = wukv_ref[:, h * dkv:h * dkv + MLA_NOPE].astype(bf16)
        wvt_ref[h] = tkv[h * dkv + MLA_NOPE:(h + 1) * dkv].astype(bf16)
    for h in range(ML_HEADS):
        wq_ref[h] = wmq_ref[h].astype(bf16)
        wkt_ref[h] = wmk_ref[h].T.astype(bf16)


def _weight_prep(w_uq, w_ukv, w_mq, w_mk):
    return pl.pallas_call(
        _wprep_kernel,
        out_shape=(
            jax.ShapeDtypeStruct((MLA_HEADS, QK_PAD, Q_LORA), bf16),
            jax.ShapeDtypeStruct((KV_LORA, MLA_HEADS * MLA_NOPE), bf16),
            jax.ShapeDtypeStruct((MLA_HEADS, MLA_V, KV_LORA), bf16),
            jax.ShapeDtypeStruct((ML_HEADS, ML_V, ML_QK), bf16),
            jax.ShapeDtypeStruct((ML_HEADS, ML_QK, ML_V), bf16),
        ),
        compiler_params=pltpu.CompilerParams(vmem_limit_bytes=VMEM_LIMIT),
        name="weight_prep",
    )(w_uq, w_ukv, w_mq, w_mk)


def _pack_kernel(w_ref, krp_ref, g_ref, o_ref):
    o_ref[:, 0:OFF_U] = w_ref[:, 0:OFF_U].astype(bf16)
    o_ref[:, OFF_U:P_U] = krp_ref[...].astype(bf16)
    o_ref[:, P_U:P_G] = w_ref[:, OFF_U:OFF_G].astype(bf16)
    o_ref[:, P_G:P_COLS] = g_ref[...].astype(bf16)


def _pack_w_in(w_in):
    d = w_in.shape[0]
    z = lambda n: jnp.zeros((d, n), w_in.dtype)
    gates = []
    for gi in range(4):
        gates += [w_in[:, OFF_G + gi * ML_HEADS:OFF_G + (gi + 1) * ML_HEADS], z(GATE_STRIDE - ML_HEADS)]
    gates = jnp.concatenate(gates + [z(LANES - 4 * GATE_STRIDE)], axis=1)
    krp = w_in[:, OFF_KR:OFF_U][:, _rope_perm()]
    tr = min(ROW_TILE, d)
    assert d % tr == 0
    return pl.pallas_call(
        _pack_kernel,
        out_shape=jax.ShapeDtypeStruct((d, P_COLS), bf16),
        grid=(d // tr,),
        in_specs=[
            pl.BlockSpec((tr, w_in.shape[1]), lambda i: (i, 0)),
            pl.BlockSpec((tr, MLA_ROPE), lambda i: (i, 0)),
            pl.BlockSpec((tr, LANES), lambda i: (i, 0)),
        ],
        out_specs=pl.BlockSpec((tr, P_COLS), lambda i: (i, 0)),
        compiler_params=_cparams(("arbitrary",)),
        name="pack_w_in",
    )(w_in, krp, gates)


def _pack_b_gate(b_gate):
    parts = []
    for gi in range(4):
        parts += [b_gate[gi * ML_HEADS:(gi + 1) * ML_HEADS], jnp.zeros((GATE_STRIDE - ML_HEADS,), f32)]
    parts.append(jnp.zeros((LANES - 4 * GATE_STRIDE,), f32))
    return jnp.concatenate(parts)[None, :]


def kernel(x, c, ctx, c_ctx, w_ada, b_ada, w_in, g_q, w_uq, g_kv, w_ukv, ml_conv_w, ml_conv_b, w_mq, w_mk, b_gate, g_hn, w_out, ln1_g, ln1_b, w_up, ffn_conv_w, ffn_conv_b, w_down, ln2_g, ln2_b):
    b, s, d = x.shape
    n_ctx = ctx.shape[1]
    assert n_ctx == CHUNK and s % ROW_TILE == 0
    n_ctx_tiles = n_ctx // ROW_TILE
    layer = 0

    rows = -(-(b + 1) // SUBLANES) * SUBLANES
    cc = jnp.concatenate([c, c_ctx[None, :], jnp.zeros((rows - b - 1, d), f32)], axis=0)
    modr = _modulation(cc, w_ada[layer], b_ada[layer][None, :]).reshape(rows, N_MOD, d)

    ck, sk, ct, st = (jnp.asarray(tbl) for tbl in _rope_tables(s, n_ctx))

    wm, wk, wvt, w_q, w_kt = _weight_prep(w_uq[layer], w_ukv[layer], w_mq[layer], w_mk[layer])
    cq, kk, vt, ksq, g, gt, u, v, o = _in_projection(
        ctx, x, modr, _pack_w_in(w_in[layer]), g_q[layer][None, :], g_kv[layer][None, :], ck, sk,
        _pack_b_gate(b_gate[layer]), wk, wvt, n_ctx_tiles)

    qt = _q_prep(cq, ksq, wm, ct, st, n_ctx_tiles)
    a = _attention(qt, kk, vt)

    mq, mkt = _mlstm_projection(u, ml_conv_w[layer], ml_conv_b[layer][None, :], w_q, w_kt, n_ctx_tiles)
    hf, hb = _mlstm_scan(mq, mkt, v, g, gt)

    x1 = _out_projection(a, hf, hb, o, x, modr, g_hn[layer].reshape(1, ML_INNER), w_out[layer].astype(bf16),
                         ln1_g[layer][None, :], ln1_b[layer][None, :], n_ctx_tiles)
    return _conv_ffn(x1, modr, w_up[layer].astype(bf16), ffn_conv_w[layer], ffn_conv_b[layer][None, :],
                     w_down[layer].astype(bf16), ln2_g[layer][None, :], ln2_b[layer][None, :])
```
